```python
import jax
import jax.numpy as jnp
from jax import lax
import numpy as np

D_MODEL = 1024
BATCH = 4
SEQ = 4096
DEPTH = 2
DEC_BATCH = 32
DEC_SEQ = 8
PAST_LEN = 8192
PAGE_SIZE = 128

HEAD_DIM = 64
A_HEADS = D_MODEL // (2 * HEAD_DIM)
DIL_PAIRS = ((128, 1), (512, 4), (2048, 16))
DIL_MAX = 2048
RNN_WIDTH = D_MODEL // 2
RNN_BLOCKS = RNN_WIDTH // HEAD_DIM
CONV_W = 4
LRU_C = 8.0
C_HEADS = D_MODEL // HEAD_DIM
C_KV_HEADS = C_HEADS // 4
CMP_LEN = 32
CMP_STRIDE = 16
SEL_BLOCK = 64
SEL_TOP = 16
WIN = 512
FFN_HIDDEN = -(-8 * D_MODEL // (3 * 256)) * 256
N_EVEN = (DEPTH + 1) // 2
N_ODD = DEPTH // 2
A_WIDTH = A_HEADS * HEAD_DIM
AB_IN = 3 * A_WIDTH + 2 * RNN_WIDTH
AB_MIX = A_WIDTH + RNN_WIDTH
C_Q = C_HEADS * HEAD_DIM
C_KV = 2 * C_KV_HEADS * HEAD_DIM
C_IN = C_Q + 3 * C_KV + 3 * C_HEADS
DIL_Q_BLOCK = 128
SEL_Q_BLOCK = 64
WIN_Q_BLOCK = 128
NEG = -1e30
FORCE_BONUS = 1e3
EPS = 1e-6

kernel_name = 'hybrid_dilated_rglru_nsa_step'


def _rms_norm(x, g):
    xf = x.astype(jnp.float32)
    y = xf * lax.rsqrt(jnp.mean(xf * xf, axis=-1, keepdims=True) + EPS) * g.astype(jnp.float32)
    return y.astype(x.dtype)


def _alibi_slopes(n):
    return 2.0 ** (-8.0 * jnp.arange(1, n + 1, dtype=jnp.float32) / n)


def _masked_softmax(s, mask):
    s = jnp.where(mask, s, NEG)
    m = jnp.max(s, axis=-1, keepdims=True)
    p = jnp.where(mask, jnp.exp(s - m), 0.0)
    return p / jnp.maximum(jnp.sum(p, axis=-1, keepdims=True), 1e-30)


def _blocked_map(fn, q, pos, block):
    b, t = q.shape[0], q.shape[1]
    if t <= block or t % block:
        return fn((q, pos))
    nb = t // block
    qb = jnp.moveaxis(q.reshape((b, nb, block) + q.shape[2:]), 1, 0)
    out = lax.map(fn, (qb, pos.reshape(nb, block)))

    def unblock(o):
        o = jnp.moveaxis(o, 0, 1)
        return o.reshape((b, t) + o.shape[3:])
    return jax.tree_util.tree_map(unblock, out)


def _gather_pages(pool, page_table):
    rows = pool[page_table]
    return rows.reshape((page_table.shape[0], -1) + pool.shape[2:])


def _dilated_attention(q, kv_ctx, q_off):
    slopes = _alibi_slopes(A_HEADS)
    scale = HEAD_DIM ** -0.5

    def block_fn(args):
        qb, qc = args
        lses, outs = [], []
        for window, dil in DIL_PAIRS:
            dist = jnp.arange(window // dil + 1) * dil
            idx = qc[:, None] - dist[None, :]
            kvg = jnp.take(kv_ctx, jnp.maximum(idx, 0), axis=1)
            s = jnp.einsum('bthd,btkhd->bhtk', qb, kvg[:, :, :, 0], preferred_element_type=jnp.float32) * scale
            s = s - slopes[:, None, None] * dist.astype(jnp.float32)
            s = jnp.where(idx >= 0, s, NEG)
            m = jnp.max(s, axis=-1, keepdims=True)
            p = jnp.exp(s - m)
            l = jnp.sum(p, axis=-1, keepdims=True)
            outs.append(jnp.einsum('bhtk,btkhd->bthd', p / l, kvg[:, :, :, 1].astype(jnp.float32)))
            lses.append(jnp.swapaxes((m + jnp.log(l))[..., 0], 1, 2))
        wts = jax.nn.softmax(jnp.stack(lses), axis=0)[..., None]
        return jnp.sum(wts * jnp.stack(outs), axis=0).astype(qb.dtype)

    pos = q_off + jnp.arange(q.shape[1])
    return _blocked_map(block_fn, q, pos, DIL_Q_BLOCK)


def _rg_lru(xr, gate, conv_prev, h_prev, conv_w, conv_b, gate_a_w, gate_a_b, gate_x_w, gate_x_b, lru_lambda):
    b, t, _ = xr.shape
    xin = jnp.concatenate([conv_prev.astype(xr.dtype), xr], axis=1)
    xc = conv_b + sum(xin[:, k:k + t] * conv_w[k] for k in range(CONV_W))
    xb = xc.reshape(b, t, RNN_BLOCKS, RNN_WIDTH // RNN_BLOCKS)
    rg = jax.nn.sigmoid((jnp.einsum('btni,nij->btnj', xb, gate_a_w).reshape(b, t, RNN_WIDTH) + gate_a_b).astype(jnp.float32))
    ig = jax.nn.sigmoid((jnp.einsum('btni,nij->btnj', xb, gate_x_w).reshape(b, t, RNN_WIDTH) + gate_x_b).astype(jnp.float32))
    log_a = -LRU_C * rg * jax.nn.softplus(-lru_lambda.astype(jnp.float32))
    a = jnp.exp(log_a)
    u = jnp.sqrt(-jnp.expm1(2.0 * log_a)) * ig * xc.astype(jnp.float32)

    def step(hc, au):
        hc = au[0] * hc + au[1]
        return hc, hc
    h_last, hs = lax.scan(step, h_prev.astype(jnp.float32), (jnp.swapaxes(a, 0, 1), jnp.swapaxes(u, 0, 1)))
    y = jnp.swapaxes(hs, 0, 1) * jax.nn.gelu(gate.astype(jnp.float32))
    return y.astype(xr.dtype), xin[:, t:], h_last.astype(h_prev.dtype)


def _mixer_ab(h, w_in, w_out, conv_w, conv_b, gate_a_w, gate_a_b, gate_x_w, gate_x_b, lru_lambda, kv_buf, conv_prev, h_prev):
    b, t, _ = h.shape
    proj = h @ w_in
    q = proj[..., :A_WIDTH].reshape(b, t, A_HEADS, HEAD_DIM)
    kv = proj[..., A_WIDTH:3 * A_WIDTH].reshape(b, t, 2, A_HEADS, HEAD_DIM)
    xr = proj[..., 3 * A_WIDTH:3 * A_WIDTH + RNN_WIDTH]
    gate = proj[..., 3 * A_WIDTH + RNN_WIDTH:]
    if kv_buf is None:
        ctx, keep = kv, min(DIL_MAX, t)
    else:
        ctx, keep = jnp.concatenate([kv_buf.astype(kv.dtype), kv], axis=1), kv_buf.shape[1]
    o_att = _dilated_attention(q, ctx, ctx.shape[1] - t)
    o_rnn, new_conv, new_h = _rg_lru(xr, gate, conv_prev, h_prev, conv_w, conv_b, gate_a_w, gate_a_b, gate_x_w, gate_x_b, lru_lambda)
    y = jnp.concatenate([o_att.reshape(b, t, A_WIDTH), o_rnn], axis=-1) @ w_out
    return y, ctx[:, ctx.shape[1] - keep:], new_conv, new_h


def _compress(ctx, w_cmp, pe_cmp):
    b, tc = ctx.shape[:2]
    n_r = CMP_LEN // CMP_STRIDE
    n_ch = tc // CMP_STRIDE
    n_c = n_ch - n_r + 1
    chunks = ctx[:, :n_ch * CMP_STRIDE].reshape((b, n_ch, CMP_STRIDE) + ctx.shape[2:])
    w = w_cmp.reshape(n_r, CMP_STRIDE, 2, HEAD_DIM, HEAD_DIM)
    out = jnp.einsum('lcd,lcde->ce', pe_cmp, w_cmp)[None, None, :, None, :]
    for i in range(n_r):
        out = out + jnp.einsum('bnscgd,scde->bncge', chunks[:, i:i + n_c], w[i])
    end = jnp.arange(n_c) * CMP_STRIDE + CMP_LEN - 1
    return out.astype(ctx.dtype), end


def _cmp_to_sel(n_c, n_sel):
    cs = jnp.arange(n_c)[:, None] * CMP_STRIDE
    ss = jnp.arange(n_sel)[None, :] * SEL_BLOCK
    ov = jnp.minimum(cs + CMP_LEN, ss + SEL_BLOCK) - jnp.maximum(cs, ss)
    return jnp.maximum(ov, 0).astype(jnp.float32) / CMP_STRIDE


def _band_attention(q, kv_ctx, q_off, slopes):
    scale = HEAD_DIM ** -0.5
    kv_pad = jnp.pad(kv_ctx, ((0, 0), (WIN, 0), (0, 0), (0, 0), (0, 0)))

    def block_fn(args):
        qb, qc = args
        tb = qb.shape[1]
        kvb = lax.dynamic_slice_in_dim(kv_pad, qc[0], tb + WIN, axis=1)
        kc = qc[0] - WIN + jnp.arange(tb + WIN)
        dist = qc[:, None] - kc[None, :]
        ok = (dist >= 0) & (dist <= WIN) & (kc >= 0)[None, :]
        s = jnp.einsum('btgrd,bsgd->bgrts', qb, kvb[:, :, 0], preferred_element_type=jnp.float32) * scale
        s = s - slopes[:, :, None, None] * dist.astype(jnp.float32)
        p = _masked_softmax(s, ok)
        return jnp.einsum('bgrts,bsgd->btgrd', p, kvb[:, :, 1])

    pos = q_off + jnp.arange(q.shape[1])
    return _blocked_map(block_fn, q, pos, WIN_Q_BLOCK)


def _mixer_nsa(h, w_in, w_out, w_cmp, pe_cmp, past_cmp, past_sel, win_buf):
    b, t, _ = h.shape
    g, r = C_KV_HEADS, C_HEADS // C_KV_HEADS
    scale = HEAD_DIM ** -0.5
    slopes = _alibi_slopes(C_HEADS).reshape(g, r)
    proj = h @ w_in
    q = proj[..., :C_Q].reshape(b, t, g, r, HEAD_DIM)
    kv_cmp, kv_sel, kv_win = [proj[..., C_Q + i * C_KV:C_Q + (i + 1) * C_KV].reshape(b, t, 2, g, HEAD_DIM) for i in range(3)]
    gates = jax.nn.sigmoid(proj[..., C_Q + 3 * C_KV:].astype(jnp.float32)).reshape(b, t, g, r, 3)
    if past_cmp is None:
        ctx_cmp, ctx_sel = kv_cmp, kv_sel
    else:
        ctx_cmp = jnp.concatenate([past_cmp.astype(kv_cmp.dtype), kv_cmp], axis=1)
        ctx_sel = jnp.concatenate([past_sel.astype(kv_sel.dtype), kv_sel], axis=1)
    if win_buf is None:
        ctx_win, keep = kv_win, min(WIN, t)
    else:
        ctx_win, keep = jnp.concatenate([win_buf.astype(kv_win.dtype), kv_win], axis=1), win_buf.shape[1]
    tc = ctx_cmp.shape[1]
    pos = (tc - t) + jnp.arange(t)
    kvc, cmp_end = _compress(ctx_cmp, w_cmp, pe_cmp)
    kc, vc = kvc[:, :, 0], kvc[:, :, 1]
    n_c = kvc.shape[1]
    n_sel = -(-tc // SEL_BLOCK)
    n_top = min(SEL_TOP, n_sel)
    sel_blocks = jnp.pad(ctx_sel, ((0, 0), (0, n_sel * SEL_BLOCK - tc), (0, 0), (0, 0), (0, 0)))
    sel_blocks = sel_blocks.reshape(b, n_sel, SEL_BLOCK, 2, g, HEAD_DIM).transpose(0, 4, 1, 2, 3, 5)
    cmp2sel = _cmp_to_sel(n_c, n_sel)
    bi = jnp.arange(b)[:, None, None, None]
    gi = jnp.arange(g)[None, :, None, None]
    blk = jnp.arange(n_sel)
    tok_off = jnp.arange(SEL_BLOCK)

    def block_fn(args):
        qb, qp = args
        tb = qb.shape[1]
        s = jnp.einsum('btgrd,bngd->bgrtn', qb, kc, preferred_element_type=jnp.float32) * scale
        s = s - slopes[:, :, None, None] * (qp[:, None] - cmp_end[None, :]).astype(jnp.float32)
        p = _masked_softmax(s, cmp_end[None, :] <= qp[:, None])
        o_cmp = jnp.einsum('bgrtn,bngd->btgrd', p, vc)
        imp = jnp.einsum('bgrtn,nj->bgtj', p, cmp2sel)
        cb = qp // SEL_BLOCK
        blk_ok = blk[None, :] <= cb[:, None]
        forced = (blk[None, :] == 0) | (blk[None, :] == cb[:, None]) | (blk[None, :] == cb[:, None] - 1)
        score = jnp.where(blk_ok, imp + FORCE_BONUS * forced.astype(jnp.float32), NEG)
        top_s, top_i = lax.top_k(score, n_top)
        kvs = sel_blocks[bi, gi, top_i]
        tok = top_i[..., None] * SEL_BLOCK + tok_off
        ok = (top_s > 0.5 * NEG)[..., None] & (tok <= qp[None, None, :, None, None])
        ks = kvs[..., 0, :].reshape(b, g, tb, n_top * SEL_BLOCK, HEAD_DIM)
        vs = kvs[..., 1, :].reshape(b, g, tb, n_top * SEL_BLOCK, HEAD_DIM)
        dist = (qp[None, None, :, None, None] - tok).reshape(b, g, tb, n_top * SEL_BLOCK)
        s2 = jnp.einsum('btgrd,bgtkd->bgrtk', qb, ks, preferred_element_type=jnp.float32) * scale
        s2 = s2 - slopes[None, :, :, None, None] * dist[:, :, None].astype(jnp.float32)
        p2 = _masked_softmax(s2, ok.reshape(b, g, tb, n_top * SEL_BLOCK)[:, :, None])
        o_sel = jnp.einsum('bgrtk,bgtkd->btgrd', p2, vs)
        return o_cmp, o_sel

    o_cmp, o_sel = _blocked_map(block_fn, q, pos, SEL_Q_BLOCK)
    o_win = _band_attention(q, ctx_win, ctx_win.shape[1] - t, slopes)
    o = gates[..., 0:1] * o_cmp + gates[..., 1:2] * o_sel + gates[..., 2:3] * o_win
    y = o.astype(h.dtype).reshape(b, t, C_Q) @ w_out
    return y, kv_cmp, kv_sel, ctx_win[:, ctx_win.shape[1] - keep:]


def _swiglu(h, w1, w3, w2):
    return (jax.nn.silu(h @ w1) * (h @ w3)) @ w2


def setup_inputs(seed: int = 0) -> dict:
    key = jax.random.key(seed)
    ks = iter(jax.random.split(key, 40))

    def nrm(shape, s):
        return s * jax.random.normal(next(ks), shape, jnp.float32)
    n_pages = PAST_LEN // PAGE_SIZE
    n_used = DEC_BATCH * n_pages
    n_phys = n_used + n_used // 4
    l_dil = min(DIL_MAX, PAST_LEN)
    l_win = min(WIN, PAST_LEN)
    bw = RNN_WIDTH // RNN_BLOCKS
    a0 = jax.random.uniform(next(ks), (N_EVEN, RNN_WIDTH), jnp.float32, 0.9, 0.999) ** (1.0 / LRU_C)
    page_table = jax.random.permutation(next(ks), n_phys)[:n_used].reshape(DEC_BATCH, n_pages).astype(jnp.int32)
    return {
        'x_prompt': nrm((BATCH, SEQ, D_MODEL), 1.0),
        'x_sample': nrm((DEC_BATCH, DEC_SEQ, D_MODEL), 1.0),
        'cache_dil_kv': nrm((N_EVEN, DEC_BATCH, l_dil, 2, A_HEADS, HEAD_DIM), 1.0),
        'state_conv': nrm((N_EVEN, DEC_BATCH, CONV_W - 1, RNN_WIDTH), 1.0),
        'state_rnn': nrm((N_EVEN, DEC_BATCH, RNN_WIDTH), 0.5),
        'cache_win_kv': nrm((N_ODD, DEC_BATCH, l_win, 2, C_KV_HEADS, HEAD_DIM), 1.0),
        'cache_cmp_kv': nrm((N_ODD, n_phys, PAGE_SIZE, 2, C_KV_HEADS, HEAD_DIM), 1.0),
        'cache_sel_kv': nrm((N_ODD, n_phys, PAGE_SIZE, 2, C_KV_HEADS, HEAD_DIM), 1.0),
        'page_table': page_table,
        'norm_mix': 1.0 + nrm((DEPTH, D_MODEL), 0.1),
        'norm_ffn': 1.0 + nrm((DEPTH, D_MODEL), 0.1),
        'norm_out': 1.0 + nrm((D_MODEL,), 0.1),
        'w_in_ab': nrm((N_EVEN, D_MODEL, AB_IN), D_MODEL ** -0.5),
        'w_out_ab': nrm((N_EVEN, AB_MIX, D_MODEL), AB_MIX ** -0.5),
        'conv_w': nrm((N_EVEN, CONV_W, RNN_WIDTH), CONV_W ** -0.5),
        'conv_b': nrm((N_EVEN, RNN_WIDTH), 0.1),
        'gate_a_w': nrm((N_EVEN, RNN_BLOCKS, bw, bw), bw ** -0.5),
        'gate_a_b': nrm((N_EVEN, RNN_WIDTH), 0.1),
        'gate_x_w': nrm((N_EVEN, RNN_BLOCKS, bw, bw), bw ** -0.5),
        'gate_x_b': nrm((N_EVEN, RNN_WIDTH), 0.1),
        'lru_lambda': jnp.log(a0) - jnp.log1p(-a0),
        'w_in_c': nrm((N_ODD, D_MODEL, C_IN), D_MODEL ** -0.5),
        'w_out_c': nrm((N_ODD, C_Q, D_MODEL), C_Q ** -0.5),
        'w_cmp': nrm((N_ODD, CMP_LEN, 2, HEAD_DIM, HEAD_DIM), (CMP_LEN * HEAD_DIM) ** -0.5),
        'pe_cmp': nrm((N_ODD, CMP_LEN, 2, HEAD_DIM), 0.1),
        'ffn_w1': nrm((DEPTH, D_MODEL, FFN_HIDDEN), D_MODEL ** -0.5),
        'ffn_w3': nrm((DEPTH, D_MODEL, FFN_HIDDEN), D_MODEL ** -0.5),
        'ffn_w2': nrm((DEPTH, FFN_HIDDEN, D_MODEL), FFN_HIDDEN ** -0.5),
    }


def reference(x_prompt, x_sample, cache_dil_kv, state_conv, state_rnn, cache_win_kv, cache_cmp_kv, cache_sel_kv, page_table, norm_mix, norm_ffn, norm_out, w_in_ab, w_out_ab, conv_w, conv_b, gate_a_w, gate_a_b, gate_x_w, gate_x_b, lru_lambda, w_in_c, w_out_c, w_cmp, pe_cmp, ffn_w1, ffn_w3, ffn_w2):
    yp, ys = x_prompt, x_sample
    bp = x_prompt.shape[0]
    dil_p, dil_s, conv_p, conv_s, rnn_p, rnn_s = [], [], [], [], [], []
    win_p, win_s, cmp_p, cmp_s, sel_p, sel_s = [], [], [], [], [], []
    for layer in range(DEPTH):
        hp = _rms_norm(yp, norm_mix[layer])
        hs = _rms_norm(ys, norm_mix[layer])
        li = layer // 2
        if layer % 2 == 0:
            wts = (w_in_ab[li], w_out_ab[li], conv_w[li], conv_b[li], gate_a_w[li], gate_a_b[li], gate_x_w[li], gate_x_b[li], lru_lambda[li])
            zc = jnp.zeros((bp, CONV_W - 1, RNN_WIDTH), x_prompt.dtype)
            zh = jnp.zeros((bp, RNN_WIDTH), x_prompt.dtype)
            op, kvp, cvp, hpn = _mixer_ab(hp, *wts, None, zc, zh)
            osm, kvs, cvs, hsn = _mixer_ab(hs, *wts, cache_dil_kv[li], state_conv[li], state_rnn[li])
            dil_p.append(kvp)
            dil_s.append(kvs)
            conv_p.append(cvp)
            conv_s.append(cvs)
            rnn_p.append(hpn)
            rnn_s.append(hsn)
        else:
            wts = (w_in_c[li], w_out_c[li], w_cmp[li], pe_cmp[li])
            op, cp, sp, wp = _mixer_nsa(hp, *wts, None, None, None)
            past_cmp = _gather_pages(cache_cmp_kv[li], page_table)
            past_sel = _gather_pages(cache_sel_kv[li], page_table)
            osm, cs, ss, wsn = _mixer_nsa(hs, *wts, past_cmp, past_sel, cache_win_kv[li])
            cmp_p.append(cp)
            cmp_s.append(cs)
            sel_p.append(sp)
            sel_s.append(ss)
            win_p.append(wp)
            win_s.append(wsn)
        yp = yp + op
        ys = ys + osm
        yp = yp + _swiglu(_rms_norm(yp, norm_ffn[layer]), ffn_w1[layer], ffn_w3[layer], ffn_w2[layer])
        ys = ys + _swiglu(_rms_norm(ys, norm_ffn[layer]), ffn_w1[layer], ffn_w3[layer], ffn_w2[layer])
    y_prompt = _rms_norm(yp, norm_out)
    y_sample = _rms_norm(ys, norm_out)
    dil_kv_p = jnp.stack(dil_p)
    dil_kv_s = jnp.stack(dil_s)
    conv_pr = jnp.stack(conv_p)
    conv_sa = jnp.stack(conv_s)
    rnn_pr = jnp.stack(rnn_p)
    rnn_sa = jnp.stack(rnn_s)
    win_kv_p = jnp.stack(win_p)
    win_kv_s = jnp.stack(win_s)
    cmp_kv_p = jnp.stack(cmp_p)
    cmp_kv_s = jnp.stack(cmp_s)
    sel_kv_p = jnp.stack(sel_p)
    sel_kv_s = jnp.stack(sel_s)
    return (y_prompt, y_sample, dil_kv_p, dil_kv_s, conv_pr, conv_sa, rnn_pr, rnn_sa, win_kv_p, win_kv_s, cmp_kv_p, cmp_kv_s, sel_kv_p, sel_kv_s)
```

```python
import functools

import jax
import jax.numpy as jnp
from jax import lax
from jax.experimental import pallas as pl
from jax.experimental.pallas import tpu as pltpu

HEAD_DIM = 64
LANES = 128
DIL_PAIRS = ((128, 1), (512, 4), (2048, 16))
DIL_MAX = 2048
DIL_KEYS = 128
CONV_W = 4
LRU_C = 8.0
CMP_LEN = 32
CMP_STRIDE = 16
SEL_BLOCK = 64
SEL_TOP = 16
WIN = 512
NEG = -1e30
FORCE_BONUS = 1e3
EPS = 1e-6
VMEM_LIMIT = 56 * 1024 * 1024

F32 = jnp.float32
BF16 = jnp.bfloat16


def _cparams(n_grid):
    return pltpu.CompilerParams(dimension_semantics=("arbitrary",) * n_grid,
                                vmem_limit_bytes=VMEM_LIMIT)


def _resident(shape):
    return pl.BlockSpec(shape, lambda *_: (0,) * len(shape), pipeline_mode=pl.Buffered(1))


def _rms(x, g):
    return x * lax.rsqrt(jnp.mean(x * x, axis=-1, keepdims=True) + EPS) * g


def _alibi_slopes(n):
    return 2.0 ** (-8.0 * jnp.arange(1, n + 1, dtype=F32) / n)


def _norm_proj_kernel(x_ref, g_ref, w_ref, *out_refs):
    h = _rms(x_ref[...], g_ref[...]).astype(BF16)
    off = 0
    for o_ref in out_refs:
        n = o_ref.shape[-1]
        o_ref[...] = jnp.dot(h, w_ref[:, off:off + n], preferred_element_type=F32)
        off += n


def _norm_proj(x, g, w, splits, tm):
    n, d = x.shape
    assert n % tm == 0 and sum(splits) == w.shape[1]
    return pl.pallas_call(
        _norm_proj_kernel,
        grid=(n // tm,),
        in_specs=[pl.BlockSpec((tm, d), lambda i: (i, 0)),
                  _resident((1, d)),
                  _resident(w.shape)],
        out_specs=[pl.BlockSpec((tm, s), lambda i: (i, 0)) for s in splits],
        out_shape=[jax.ShapeDtypeStruct((n, s), F32) for s in splits],
        compiler_params=_cparams(1),
        name="norm_proj",
    )(x, g.reshape(1, d), w)


def _out_ffn_kernel(*refs, n_mix, n_chunks, final_norm):
    res_ref = refs[0]
    a_refs = refs[1:1 + n_mix]
    wo_ref, gf_ref, w1_ref, w3_ref, w2_ref = refs[1 + n_mix:6 + n_mix]
    rest = refs[6 + n_mix:]
    go_ref = rest[0] if final_norm else None
    o_ref = rest[-1]
    mix = jnp.concatenate([a_ref[...].astype(BF16) for a_ref in a_refs], axis=1)
    y = res_ref[...] + jnp.dot(mix, wo_ref[...], preferred_element_type=F32)
    h = _rms(y, gf_ref[...]).astype(BF16)
    ch = w1_ref.shape[1] // n_chunks
    for c in range(n_chunks):
        a = jnp.dot(h, w1_ref[:, c * ch:(c + 1) * ch], preferred_element_type=F32)
        b = jnp.dot(h, w3_ref[:, c * ch:(c + 1) * ch], preferred_element_type=F32)
        act = (a * (1.0 / (1.0 + jnp.exp(-a))) * b).astype(BF16)
        y = y + jnp.dot(act, w2_ref[c * ch:(c + 1) * ch, :], preferred_element_type=F32)
    if final_norm:
        y = _rms(y, go_ref[...])
    o_ref[...] = y


def _out_ffn(res, mixes, w_out, g_ffn, w1, w3, w2, g_out, tm):
    n, d = res.shape
    hidden = w1.shape[1]
    n_chunks = hidden // 256
    assert n % tm == 0 and hidden % 256 == 0
    final_norm = g_out is not None
    row = lambda width: pl.BlockSpec((tm, width), lambda i: (i, 0))
    in_specs = [row(d)] + [row(m.shape[1]) for m in mixes] + [_resident(w_out.shape)]
    in_specs += [_resident((1, d)), _resident(w1.shape), _resident(w3.shape), _resident(w2.shape)]
    args = [res, *mixes, w_out, g_ffn.reshape(1, d), w1, w3, w2]
    if final_norm:
        in_specs.append(_resident((1, d)))
        args.append(g_out.reshape(1, d))
    return pl.pallas_call(
        functools.partial(_out_ffn_kernel, n_mix=len(mixes), n_chunks=n_chunks, final_norm=final_norm),
        grid=(n // tm,),
        in_specs=in_specs,
        out_specs=row(d),
        out_shape=jax.ShapeDtypeStruct((n, d), F32),
        compiler_params=_cparams(1),
        name="out_ffn",
    )(*args)


def _dil_attn_kernel(slope_ref, q_ref, k_ref, v_ref, o_ref, *scratch, seq):
    ob_refs, lse_refs = scratch[:3], scratch[3:]
    hp = pl.program_id(1)
    lane_lo = lax.broadcasted_iota(jnp.int32, (1, LANES), 1) < HEAD_DIM
    slopes = [slope_ref[pl.ds(2 * hp + j, 1), 0:1] for j in range(2)]
    tq, tk = DIL_KEYS, 2 * DIL_KEYS
    iq = lax.broadcasted_iota(jnp.int32, (tq, tk), 0)
    ik = lax.broadcasted_iota(jnp.int32, (tq, tk), 1)

    for br, (_, dil) in enumerate(DIL_PAIRS):
        n_blk = seq // dil // tq

        def block(idx, carry, br=br, dil=dil, n_blk=n_blk):
            r = idx // n_blk
            blk = idx % n_blk
            a0 = blk * tq
            ka0 = jnp.maximum(blk - 1, 0) * tq
            q_rows = pl.ds(r + dil * a0, tq, stride=dil) if dil > 1 else pl.ds(a0, tq)
            k_rows = pl.ds(r + dil * ka0, tk, stride=dil) if dil > 1 else pl.ds(ka0, tk)
            q = q_ref[q_rows, :] * (HEAD_DIM ** -0.5)
            k = k_ref[k_rows, :].astype(BF16)
            v = v_ref[k_rows, :].astype(BF16)
            da = (a0 - ka0) + iq - ik
            valid = (da >= 0) & (da <= DIL_KEYS)
            dist = (dil * da).astype(F32)
            outs, lses = [], []
            for j in range(2):
                qj = jnp.where(lane_lo if j == 0 else ~lane_lo, q, 0.0).astype(BF16)
                s = lax.dot_general(qj, k, (((1,), (1,)), ((), ())), preferred_element_type=F32)
                s = jnp.where(valid, s - slopes[j] * dist, NEG)
                m = jnp.max(s, axis=-1, keepdims=True)
                p = jnp.exp(s - m)
                l = jnp.sum(p, axis=-1, keepdims=True)
                outs.append(jnp.dot(p.astype(BF16), v, preferred_element_type=F32) / l)
                lses.append(m + jnp.log(l))
            ob_refs[br][q_rows, :] = jnp.where(lane_lo, outs[0], outs[1])
            lse_refs[br][q_rows, :] = jnp.where(lane_lo, lses[0], lses[1])
            return carry

        lax.fori_loop(0, dil * n_blk, block, 0)

    rows = 512
    def merge(i, carry):
        sl = pl.ds(i * rows, rows)
        ls = [lse_refs[b][sl, :] for b in range(3)]
        m = jnp.maximum(jnp.maximum(ls[0], ls[1]), ls[2])
        es = [jnp.exp(x - m) for x in ls]
        num = es[0] * ob_refs[0][sl, :] + es[1] * ob_refs[1][sl, :] + es[2] * ob_refs[2][sl, :]
        o_ref[sl, :] = num / (es[0] + es[1] + es[2])
        return carry
    lax.fori_loop(0, seq // rows, merge, 0)


def _dil_attn(q, kv, n_heads):
    b, t, width = q.shape
    n_hp = width // LANES
    assert t % (16 * 2 * DIL_KEYS) == 0
    slopes = jnp.broadcast_to(_alibi_slopes(n_heads)[:, None], (n_heads, LANES))
    blk = lambda off: pl.BlockSpec((None, t, LANES), lambda i, j: (i, 0, off + j))
    return pl.pallas_call(
        functools.partial(_dil_attn_kernel, seq=t),
        grid=(b, n_hp),
        in_specs=[_resident((n_heads, LANES)), blk(0), blk(0), blk(n_hp)],
        out_specs=blk(0),
        out_shape=jax.ShapeDtypeStruct((b, t, width), F32),
        scratch_shapes=[pltpu.VMEM((t, LANES), F32)] * 6,
        compiler_params=_cparams(2),
        name="dil_attn",
    )(slopes, q, kv, kv)


def _rglru_kernel(xr_ref, gate_ref, cprev_ref, hprev_ref, cw_ref, cb_ref, wa_ref, ba_ref,
                  wx_ref, bx_ref, lam_ref, y_ref, cnew_ref, hnew_ref,
                  xin_ref, a_ref, u_ref, h_ref):
    ti = pl.program_id(1)
    bb, tc, width = xr_ref.shape
    pad = 8
    tail = CONV_W - 1

    @pl.when(ti == 0)
    def _():
        xin_ref[:, pad - tail:pad, :] = cprev_ref[...]
        h_ref[...] = hprev_ref[...]

    xin_ref[:, pad:pad + tc, :] = xr_ref[...]
    lam = lam_ref[...]
    neg_softplus_c = -LRU_C * (jnp.maximum(-lam, 0.0) + jnp.log1p(jnp.exp(-jnp.abs(lam))))
    for b in range(bb):
        xc = cb_ref[...] + sum(xin_ref[b, pad - tail + k:pad - tail + k + tc, :] * cw_ref[k:k + 1, :]
                               for k in range(CONV_W))
        xcb = xc.astype(BF16)
        ra = jnp.dot(xcb, wa_ref[...], preferred_element_type=F32) + ba_ref[...]
        rx = jnp.dot(xcb, wx_ref[...], preferred_element_type=F32) + bx_ref[...]
        rg = 1.0 / (1.0 + jnp.exp(-ra))
        ig = 1.0 / (1.0 + jnp.exp(-rx))
        log_a = neg_softplus_c * rg
        a = jnp.exp(log_a)
        a_ref[b] = a
        u_ref[b] = jnp.sqrt(jnp.tanh(-log_a) * (1.0 + a * a)) * ig * xc

    def step(t, hs):
        new = []
        for b in range(bb):
            hb = a_ref[b, pl.ds(t, 1), :] * hs[b] + u_ref[b, pl.ds(t, 1), :]
            u_ref[b, pl.ds(t, 1), :] = hb
            new.append(hb)
        return tuple(new)

    hs = lax.fori_loop(0, tc, step, tuple(h_ref[b:b + 1, :] for b in range(bb)), unroll=8)
    for b in range(bb):
        h_ref[b:b + 1, :] = hs[b]
    g = gate_ref[...]
    cdf = 0.5 * (1.0 + jnp.tanh(0.7978845608028654 * (g + 0.044715 * (g * g * g))))
    y_ref[...] = u_ref[...] * (g * cdf)
    xin_ref[:, pad - tail:pad, :] = xin_ref[:, pad + tc - tail:pad + tc, :]
    cnew_ref[...] = xin_ref[:, pad - tail:pad, :]
    hnew_ref[...] = h_ref[...]


def _rglru(xr, gate, conv_prev, h_prev, conv_w, conv_b, wa_bd, ba, wx_bd, bx, lam, bb, tc):
    b, t, width = xr.shape
    assert b % bb == 0 and t % tc == 0 and tc >= CONV_W - 1
    seq_blk = pl.BlockSpec((bb, tc, width), lambda i, j: (i, j, 0))
    vec = _resident((1, width))
    return pl.pallas_call(
        _rglru_kernel,
        grid=(b // bb, t // tc),
        in_specs=[seq_blk, seq_blk,
                  pl.BlockSpec((bb, CONV_W - 1, width), lambda i, j: (i, 0, 0)),
                  pl.BlockSpec((bb, width), lambda i, j: (i, 0)),
                  _resident((CONV_W, width)), vec, _resident(wa_bd.shape), vec,
                  _resident(wx_bd.shape), vec, vec],
        out_specs=[seq_blk,
                   pl.BlockSpec((bb, CONV_W - 1, width), lambda i, j: (i, 0, 0)),
                   pl.BlockSpec((bb, width), lambda i, j: (i, 0))],
        out_shape=[jax.ShapeDtypeStruct((b, t, width), F32),
                   jax.ShapeDtypeStruct((b, CONV_W - 1, width), F32),
                   jax.ShapeDtypeStruct((b, width), F32)],
        scratch_shapes=[pltpu.VMEM((bb, tc + 8, width), F32), pltpu.VMEM((bb, tc, width), F32),
                        pltpu.VMEM((bb, tc, width), F32), pltpu.VMEM((bb, width), F32)],
        compiler_params=_cparams(2),
        name="rglru",
    )(xr, gate, conv_prev, h_prev, conv_w, conv_b.reshape(1, width), wa_bd, ba.reshape(1, width),
      wx_bd, bx.reshape(1, width), lam.reshape(1, width))


def _block_diag(w):
    n, bi, bj = w.shape
    eye = jnp.eye(n, dtype=w.dtype)
    return (eye[:, None, :, None] * w[:, :, None, :]).reshape(n * bi, n * bj)


def _lane_lo():
    return lax.broadcasted_iota(jnp.int32, (1, LANES), 1) < HEAD_DIM


def _spread_kv(k2, v2, half):
    lo = _lane_lo()
    own = lo if half == 0 else ~lo
    k_own = jnp.where(own, k2, 0.0)
    v_own = jnp.where(own, v2, 0.0)
    k_other = pltpu.roll(k_own, HEAD_DIM, axis=1)
    v_both = v_own + pltpu.roll(v_own, HEAD_DIM, axis=1)
    k_lo, k_hi = (k_own, k_other) if half == 0 else (k_other, k_own)
    return k_lo.astype(BF16), k_hi.astype(BF16), v_both.astype(BF16)


def _group_heads(q_ref, gl):
    base = gl * 4 * HEAD_DIM
    qa = (q_ref[:, base:base + LANES] * (HEAD_DIM ** -0.5)).astype(BF16)
    qb = (q_ref[:, base + LANES:base + 2 * LANES] * (HEAD_DIM ** -0.5)).astype(BF16)
    return ((qa, 0), (qa, 1), (qb, 0), (qb, 1))


def _nt_dot(a, b):
    return lax.dot_general(a, b, (((1,), (1,)), ((), ())), preferred_element_type=F32)


def _compress_kernel(*refs, n_ch):
    x_refs, (w_ref, pe_ref, o_ref) = refs[:-3], refs[-3:]
    half = 4 * HEAD_DIM
    per_half = len(x_refs) // 2
    for c in range(2):
        acc = [None, None]
        pe_acc = None
        for s in range(CMP_STRIDE):
            xs = jnp.concatenate([x_ref[pl.ds(s, n_ch, stride=CMP_STRIDE), :]
                                  for x_ref in x_refs[c * per_half:(c + 1) * per_half]], axis=1).astype(BF16)
            for i in range(2):
                w = w_ref[(i * CMP_STRIDE + s) * 2 + c]
                d = jnp.dot(xs, w, preferred_element_type=F32)
                acc[i] = d if acc[i] is None else acc[i] + d
                pe_row = jnp.broadcast_to(pe_ref[i * CMP_STRIDE + s:i * CMP_STRIDE + s + 1, c * half:(c + 1) * half],
                                          (8, half)).astype(BF16)
                dp = jnp.dot(pe_row, w, preferred_element_type=F32)
                pe_acc = dp if pe_acc is None else pe_acc + dp
        o_ref[:, c * half:(c + 1) * half] = acc[0] + pltpu.roll(acc[1], n_ch - 1, axis=0) + pe_acc[0:1, :]


def _compress_weights(w_cmp, pe_cmp, n_groups):
    n_r = CMP_LEN // CMP_STRIDE
    w = w_cmp.reshape(n_r * CMP_STRIDE * 2, HEAD_DIM, HEAD_DIM)
    eye = jnp.eye(n_groups, dtype=w.dtype)
    wbd = (eye[None, :, None, :, None] * w[:, None, :, None, :]).reshape(-1, n_groups * HEAD_DIM, n_groups * HEAD_DIM)
    pe = jnp.broadcast_to(pe_cmp[:, :, None, :], (CMP_LEN, 2, n_groups, HEAD_DIM)).reshape(CMP_LEN, -1)
    return wbd.astype(BF16), pe


def _compress(ctx, wbd, pe):
    b, tc, width = ctx.shape
    n_ch = tc // CMP_STRIDE
    return pl.pallas_call(
        functools.partial(_compress_kernel, n_ch=n_ch),
        grid=(b,),
        in_specs=[pl.BlockSpec((None, tc, LANES), functools.partial(lambda j, i: (i, 0, j), j))
                  for j in range(width // LANES)] + [_resident(wbd.shape), _resident(pe.shape)],
        out_specs=pl.BlockSpec((None, n_ch, width), lambda i: (i, 0, 0)),
        out_shape=jax.ShapeDtypeStruct((b, n_ch, width), F32),
        compiler_params=_cparams(1),
        name="nsa_compress",
    )(*([ctx] * (width // LANES)), wbd, pe)


def _cmp_select_kernel(slope_ref, c2s_ref, q_ref, kc_ref, vc_ref, o_ref, sel_ref, *, n_c, n_sel):
    gp, qi = pl.program_id(1), pl.program_id(2)
    tq = q_ref.shape[0]
    n_ch = kc_ref.shape[0]
    lo = _lane_lo()
    qp = qi * tq + lax.broadcasted_iota(jnp.int32, (tq, 1), 0)
    n_idx = lax.broadcasted_iota(jnp.int32, (1, n_ch), 1)
    cmp_end = n_idx * CMP_STRIDE + (CMP_LEN - 1)
    valid = (cmp_end <= qp) & (n_idx < n_c)
    dist = (qp - cmp_end).astype(F32)
    blk = lax.broadcasted_iota(jnp.int32, (1, LANES), 1)
    cb = qp // SEL_BLOCK
    blk_ok = (blk <= cb) & (blk < n_sel)
    forced = (blk == 0) | (blk == cb) | (blk == cb - 1)
    blk_t = lax.broadcasted_iota(jnp.int32, (n_sel, tq), 0)
    for gl in range(2):
        keys = _spread_kv(kc_ref[...], vc_ref[...], gl)
        imp = jnp.zeros((tq, LANES), F32)
        outs = []
        for r, (qh, variant) in enumerate(_group_heads(q_ref, gl)):
            slope = slope_ref[pl.ds((2 * gp + gl) * 4 + r, 1), 0:1]
            s = jnp.where(valid, _nt_dot(qh, keys[variant]) - slope * dist, NEG)
            m = jnp.max(s, axis=-1, keepdims=True)
            p = jnp.where(valid, jnp.exp(s - m), 0.0)
            p = (p / jnp.maximum(jnp.sum(p, axis=-1, keepdims=True), 1e-30)).astype(BF16)
            outs.append(jnp.dot(p, keys[2], preferred_element_type=F32))
            imp = imp + jnp.dot(p, c2s_ref[...], preferred_element_type=F32)
        base = gl * 4 * HEAD_DIM
        o_ref[:, base:base + LANES] = jnp.where(lo, outs[0], outs[1])
        o_ref[:, base + LANES:base + 2 * LANES] = jnp.where(lo, outs[2], outs[3])
        score = jnp.where(blk_ok, imp + FORCE_BONUS * forced.astype(F32), NEG)
        score_t = score.T[0:n_sel, :]
        rank = jnp.zeros((n_sel, tq), F32)
        for k in range(n_sel):
            row = score_t[k:k + 1, :]
            ahead = (row > score_t) | ((row == score_t) & (k < blk_t))
            rank = rank + ahead.astype(F32)
        chosen = ((rank < SEL_TOP) & (score_t > 0.5 * NEG)).astype(F32)
        chosen = jnp.concatenate([chosen, jnp.zeros((LANES - n_sel, tq), F32)], axis=0) if n_sel < LANES else chosen
        sel_ref[gl] = chosen.T.astype(BF16)


def _cmp_select(q, kvc, n_c, n_sel, n_groups, tq):
    b, t, width = q.shape
    n_ch = kvc.shape[1]
    n_gp = n_groups // 2
    n_heads = width // HEAD_DIM
    assert tq == LANES and n_sel <= LANES and n_sel % 8 == 0 and t % tq == 0
    slopes = jnp.broadcast_to(_alibi_slopes(n_heads)[:, None], (n_heads, LANES))
    cs = jnp.arange(n_ch)[:, None] * CMP_STRIDE
    ss = jnp.arange(LANES)[None, :] * SEL_BLOCK
    ov = jnp.maximum(jnp.minimum(cs + CMP_LEN, ss + SEL_BLOCK) - jnp.maximum(cs, ss), 0).astype(F32) / CMP_STRIDE
    c2s = jnp.where((jnp.arange(n_ch)[:, None] < n_c) & (jnp.arange(LANES)[None, :] < n_sel), ov, 0.0).astype(BF16)
    return pl.pallas_call(
        functools.partial(_cmp_select_kernel, n_c=n_c, n_sel=n_sel),
        grid=(b, n_gp, t // tq),
        in_specs=[_resident(slopes.shape), _resident(c2s.shape),
                  pl.BlockSpec((None, tq, 2 * 4 * HEAD_DIM), lambda i, j, k: (i, k, j)),
                  pl.BlockSpec((None, n_ch, LANES), lambda i, j, k: (i, 0, j)),
                  pl.BlockSpec((None, n_ch, LANES), lambda i, j, k: (i, 0, n_gp + j))],
        out_specs=[pl.BlockSpec((None, tq, 2 * 4 * HEAD_DIM), lambda i, j, k: (i, k, j)),
                   pl.BlockSpec((None, 2, tq, LANES), lambda i, j, k: (i, j, k, 0))],
        out_shape=[jax.ShapeDtypeStruct((b, t, width), F32),
                   jax.ShapeDtypeStruct((b, n_groups, t, LANES), BF16)],
        compiler_params=_cparams(3),
        name="nsa_cmp_select",
    )(slopes, c2s, q, kvc, kvc)


def _stage_kv(k_ref, v_ref, kv_scr):
    for gl in range(2):
        k_lo, k_hi, v_both = _spread_kv(k_ref[...], v_ref[...], gl)
        kv_scr[3 * gl + 0][...] = k_lo
        kv_scr[3 * gl + 1][...] = k_hi
        kv_scr[3 * gl + 2][...] = v_both


def _sel_attn_kernel(slope_ref, exp_ref, q_ref, sel_ref, k_ref, v_ref, o_ref, *kv_scr, tk):
    gp, qi = pl.program_id(1), pl.program_id(2)
    tq = q_ref.shape[0]
    lo = _lane_lo()

    @pl.when(qi == 0)
    def _():
        _stage_kv(k_ref, v_ref, kv_scr)

    qp = qi * tq + lax.broadcasted_iota(jnp.int32, (tq, 1), 0)
    n_kt = (qi * tq + tq - 1) // tk + 1
    for gl in range(2):
        heads = _group_heads(q_ref, gl)
        slopes = [slope_ref[pl.ds((2 * gp + gl) * 4 + r, 1), 0:1] for r in range(4)]
        sel = sel_ref[gl]
        k_scr = (kv_scr[3 * gl], kv_scr[3 * gl + 1])
        v_scr = kv_scr[3 * gl + 2]

        def tile(kt, carry, heads=heads, slopes=slopes, sel=sel, k_scr=k_scr, v_scr=v_scr):
            rows = pl.ds(pl.multiple_of(kt * tk, tk), tk)
            kpos = kt * tk + lax.broadcasted_iota(jnp.int32, (1, tk), 1)
            picked = _nt_dot(sel, exp_ref[rows, :]) > 0.5
            ok = picked & (kpos <= qp)
            dist = (qp - kpos).astype(F32)
            v = v_scr[rows, :]
            new = []
            for r, (qh, variant) in enumerate(heads):
                m_old, l_old, acc_old = carry[r]
                s = jnp.where(ok, _nt_dot(qh, k_scr[variant][rows, :]) - slopes[r] * dist, NEG)
                m_new = jnp.maximum(m_old, jnp.max(s, axis=-1, keepdims=True))
                alpha = jnp.exp(m_old - m_new)
                p = jnp.where(ok, jnp.exp(s - m_new), 0.0)
                l_new = alpha * l_old + jnp.sum(p, axis=-1, keepdims=True)
                acc_new = alpha * acc_old + jnp.dot(p.astype(BF16), v, preferred_element_type=F32)
                new.append((m_new, l_new, acc_new))
            return tuple(new)

        init = tuple((jnp.full((tq, 1), NEG, F32), jnp.zeros((tq, 1), F32), jnp.zeros((tq, LANES), F32))
                     for _ in range(4))
        res = lax.fori_loop(0, n_kt, tile, init)
        outs = [acc / jnp.maximum(l, 1e-30) for (_, l, acc) in res]
        base = gl * 4 * HEAD_DIM
        o_ref[:, base:base + LANES] = jnp.where(lo, outs[0], outs[1])
        o_ref[:, base + LANES:base + 2 * LANES] = jnp.where(lo, outs[2], outs[3])


def _sel_attn(q, kv, sel, n_groups, tq, tk):
    b, t, width = q.shape
    n_gp = n_groups // 2
    n_heads = width // HEAD_DIM
    assert t % tq == 0 and t % tk == 0 and tk % SEL_BLOCK == 0
    slopes = jnp.broadcast_to(_alibi_slopes(n_heads)[:, None], (n_heads, LANES))
    expand = (jnp.arange(t)[:, None] // SEL_BLOCK == jnp.arange(LANES)[None, :]).astype(BF16)
    return pl.pallas_call(
        functools.partial(_sel_attn_kernel, tk=tk),
        grid=(b, n_gp, t // tq),
        in_specs=[_resident(slopes.shape), _resident(expand.shape),
                  pl.BlockSpec((None, tq, 2 * 4 * HEAD_DIM), lambda i, j, k: (i, k, j)),
                  pl.BlockSpec((None, 2, tq, LANES), lambda i, j, k: (i, j, k, 0)),
                  pl.BlockSpec((None, t, LANES), lambda i, j, k: (i, 0, j)),
                  pl.BlockSpec((None, t, LANES), lambda i, j, k: (i, 0, n_gp + j))],
        out_specs=pl.BlockSpec((None, tq, 2 * 4 * HEAD_DIM), lambda i, j, k: (i, k, j)),
        out_shape=jax.ShapeDtypeStruct((b, t, width), F32),
        scratch_shapes=[pltpu.VMEM((t, LANES), BF16)] * 6,
        compiler_params=_cparams(3),
        name="nsa_sel_attn",
    )(slopes, expand, q, sel, kv, kv)


def _win_combine_kernel(slope_ref, q_ref, gate_ref, ocmp_ref, osel_ref, k_ref, v_ref, o_ref, *kv_scr, seq):
    gp, qi = pl.program_id(1), pl.program_id(2)
    tq = q_ref.shape[0]
    tk = WIN + tq
    lo = _lane_lo()

    @pl.when(qi == 0)
    def _():
        _stage_kv(k_ref, v_ref, kv_scr)

    k0 = jnp.clip(qi * tq - WIN, 0, seq - tk)
    rows = pl.ds(pl.multiple_of(k0, tq), tk)
    qp = qi * tq + lax.broadcasted_iota(jnp.int32, (tq, 1), 0)
    kpos = k0 + lax.broadcasted_iota(jnp.int32, (1, tk), 1)
    dist_i = qp - kpos
    ok = (dist_i >= 0) & (dist_i <= WIN)
    dist = dist_i.astype(F32)
    gates = 1.0 / (1.0 + jnp.exp(-gate_ref[...]))
    for gl in range(2):
        outs = []
        v = kv_scr[3 * gl + 2][rows, :]
        for r, (qh, variant) in enumerate(_group_heads(q_ref, gl)):
            slope = slope_ref[pl.ds((2 * gp + gl) * 4 + r, 1), 0:1]
            s = jnp.where(ok, _nt_dot(qh, kv_scr[3 * gl + variant][rows, :]) - slope * dist, NEG)
            m = jnp.max(s, axis=-1, keepdims=True)
            p = jnp.where(ok, jnp.exp(s - m), 0.0)
            l = jnp.sum(p, axis=-1, keepdims=True)
            outs.append(jnp.dot(p.astype(BF16), v, preferred_element_type=F32) / jnp.maximum(l, 1e-30))
        for half in range(2):
            cols = slice(gl * 4 * HEAD_DIM + half * LANES, gl * 4 * HEAD_DIM + (half + 1) * LANES)
            o_win = jnp.where(lo, outs[2 * half], outs[2 * half + 1])
            c0 = (gl * 4 + 2 * half) * 3
            g_cmp, g_sel, g_win = (jnp.where(lo, gates[:, c0 + i:c0 + i + 1], gates[:, c0 + 3 + i:c0 + 4 + i])
                                   for i in range(3))
            o_ref[:, cols] = g_cmp * ocmp_ref[:, cols] + g_sel * osel_ref[:, cols] + g_win * o_win


def _win_combine(q, kv, gates, o_cmp, o_sel, n_groups, tq):
    b, t, width = q.shape
    n_gp = n_groups // 2
    n_heads = width // HEAD_DIM
    assert t % tq == 0 and t >= WIN + tq
    slopes = jnp.broadcast_to(_alibi_slopes(n_heads)[:, None], (n_heads, LANES))
    qblk = pl.BlockSpec((None, tq, 2 * 4 * HEAD_DIM), lambda i, j, k: (i, k, j))
    return pl.pallas_call(
        functools.partial(_win_combine_kernel, seq=t),
        grid=(b, n_gp, t // tq),
        in_specs=[_resident(slopes.shape), qblk,
                  pl.BlockSpec((None, tq, LANES), lambda i, j, k: (i, k, j)),
                  qblk, qblk,
                  pl.BlockSpec((None, t, LANES), lambda i, j, k: (i, 0, j)),
                  pl.BlockSpec((None, t, LANES), lambda i, j, k: (i, 0, n_gp + j))],
        out_specs=qblk,
        out_shape=jax.ShapeDtypeStruct((b, t, width), F32),
        scratch_shapes=[pltpu.VMEM((t, LANES), BF16)] * 6,
        compiler_params=_cparams(3),
        name="nsa_win_combine",
    )(slopes, q, gates, o_cmp, o_sel, kv, kv)


def _nsa_prompt(q, kv_cmp, kv_sel, kv_win, gates, wbd, pe, n_groups):
    t = q.shape[1]
    kvc = _compress(kv_cmp, wbd, pe)
    n_c = t // CMP_STRIDE - CMP_LEN // CMP_STRIDE + 1
    n_sel = -(-t // SEL_BLOCK)
    o_cmp, sel = _cmp_select(q, kvc, n_c, n_sel, n_groups, LANES)
    o_sel = _sel_attn(q, kv_sel, sel, n_groups, LANES, 4 * LANES)
    return _win_combine(q, kv_win, gates, o_cmp, o_sel, n_groups, LANES)


def _masked_softmax(s, mask):
    s = jnp.where(mask, s, NEG)
    m = jnp.max(s, axis=-1, keepdims=True)
    p = jnp.where(mask, jnp.exp(s - m), 0.0)
    return p / jnp.maximum(jnp.sum(p, axis=-1, keepdims=True), 1e-30)


def _dil_attn_sample(q, ctx, n_heads):
    t = q.shape[1]
    slopes = _alibi_slopes(n_heads)
    pos = ctx.shape[1] - t + jnp.arange(t)
    lses, outs = [], []
    for window, dil in DIL_PAIRS:
        dist = jnp.arange(window // dil + 1) * dil
        idx = pos[:, None] - dist[None, :]
        kvg = jnp.take(ctx, jnp.maximum(idx, 0), axis=1)
        s = jnp.einsum('bthd,btkhd->bhtk', q, kvg[:, :, :, 0], preferred_element_type=F32) * HEAD_DIM ** -0.5
        s = s - slopes[:, None, None] * dist.astype(F32)
        s = jnp.where(idx >= 0, s, NEG)
        m = jnp.max(s, axis=-1, keepdims=True)
        p = jnp.exp(s - m)
        l = jnp.sum(p, axis=-1, keepdims=True)
        outs.append(jnp.einsum('bhtk,btkhd->bthd', p / l, kvg[:, :, :, 1]))
        lses.append(jnp.swapaxes((m + jnp.log(l))[..., 0], 1, 2))
    wts = jax.nn.softmax(jnp.stack(lses), axis=0)[..., None]
    return jnp.sum(wts * jnp.stack(outs), axis=0)


def _nsa_sample(q, kv_cmp, kv_sel, kv_win, gate_logits, past_cmp, past_sel, win_buf, w_cmp, pe_cmp):
    b, t, g, r, _ = q.shape
    scale = HEAD_DIM ** -0.5
    slopes = _alibi_slopes(g * r).reshape(g, r)
    gates = jax.nn.sigmoid(gate_logits)
    ctx_cmp = jnp.concatenate([past_cmp, kv_cmp], axis=1)
    ctx_sel = jnp.concatenate([past_sel, kv_sel], axis=1)
    ctx_win = jnp.concatenate([win_buf, kv_win], axis=1)
    tc = ctx_cmp.shape[1]
    qp = (tc - t) + jnp.arange(t)
    n_r = CMP_LEN // CMP_STRIDE
    n_ch = tc // CMP_STRIDE
    n_c = n_ch - n_r + 1
    chunks = ctx_cmp[:, :n_ch * CMP_STRIDE].reshape((b, n_ch, CMP_STRIDE) + ctx_cmp.shape[2:])
    w = w_cmp.reshape(n_r, CMP_STRIDE, 2, HEAD_DIM, HEAD_DIM)
    kvc = jnp.einsum('lcd,lcde->ce', pe_cmp, w_cmp)[None, None, :, None, :]
    for i in range(n_r):
        kvc = kvc + jnp.einsum('bnscgd,scde->bncge', chunks[:, i:i + n_c], w[i])
    cmp_end = jnp.arange(n_c) * CMP_STRIDE + CMP_LEN - 1
    kc, vc = kvc[:, :, 0], kvc[:, :, 1]
    n_sel = -(-tc // SEL_BLOCK)
    n_top = min(SEL_TOP, n_sel)
    sel_blocks = jnp.pad(ctx_sel, ((0, 0), (0, n_sel * SEL_BLOCK - tc), (0, 0), (0, 0), (0, 0)))
    sel_blocks = sel_blocks.reshape(b, n_sel, SEL_BLOCK, 2, g, HEAD_DIM).transpose(0, 4, 1, 2, 3, 5)
    cs = jnp.arange(n_c)[:, None] * CMP_STRIDE
    ss = jnp.arange(n_sel)[None, :] * SEL_BLOCK
    cmp2sel = jnp.maximum(jnp.minimum(cs + CMP_LEN, ss + SEL_BLOCK) - jnp.maximum(cs, ss), 0).astype(F32) / CMP_STRIDE
    blk = jnp.arange(n_sel)
    s = jnp.einsum('btgrd,bngd->bgrtn', q, kc, preferred_element_type=F32) * scale
    s = s - slopes[:, :, None, None] * (qp[:, None] - cmp_end[None, :]).astype(F32)
    p = _masked_softmax(s, cmp_end[None, :] <= qp[:, None])
    o_cmp = jnp.einsum('bgrtn,bngd->btgrd', p, vc)
    imp = jnp.einsum('bgrtn,nj->bgtj', p, cmp2sel)
    cb = qp // SEL_BLOCK
    blk_ok = blk[None, :] <= cb[:, None]
    forced = (blk[None, :] == 0) | (blk[None, :] == cb[:, None]) | (blk[None, :] == cb[:, None] - 1)
    score = jnp.where(blk_ok, imp + FORCE_BONUS * forced.astype(F32), NEG)
    top_s, top_i = lax.top_k(score, n_top)
    picked = jnp.sum((top_i[..., None] == blk) & (top_s > 0.5 * NEG)[..., None], axis=-2) > 0
    kpos = jnp.arange(tc)
    ok = picked[..., kpos // SEL_BLOCK] & (kpos <= qp[:, None])
    s2 = jnp.einsum('btgrd,bkgd->bgrtk', q, ctx_sel[:, :, 0], preferred_element_type=F32) * scale
    s2 = s2 - slopes[None, :, :, None, None] * (qp[:, None] - kpos).astype(F32)
    p2 = _masked_softmax(s2, ok[:, :, None])
    o_sel = jnp.einsum('bgrtk,bkgd->btgrd', p2, ctx_sel[:, :, 1])
    tw = ctx_win.shape[1]
    dw = (tw - t + jnp.arange(t))[:, None] - jnp.arange(tw)[None, :]
    okw = (dw >= 0) & (dw <= WIN)
    s3 = jnp.einsum('btgrd,bsgd->bgrts', q, ctx_win[:, :, 0], preferred_element_type=F32) * scale
    s3 = s3 - slopes[:, :, None, None] * dw.astype(F32)
    p3 = _masked_softmax(s3, okw)
    o_win = jnp.einsum('bgrts,bsgd->btgrd', p3, ctx_win[:, :, 1])
    o = gates[..., 0:1] * o_cmp + gates[..., 1:2] * o_sel + gates[..., 2:3] * o_win
    return o, ctx_win[:, t:]


def _gate_columns(w_gate, n_groups):
    d = w_gate.shape[0]
    per_pair = w_gate.reshape(d, n_groups // 2, 2 * 4 * 3)
    return jnp.pad(per_pair, ((0, 0), (0, 0), (0, LANES - 2 * 4 * 3))).reshape(d, -1)


def kernel(x_prompt, x_sample, cache_dil_kv, state_conv, state_rnn, cache_win_kv, cache_cmp_kv, cache_sel_kv, page_table, norm_mix, norm_ffn, norm_out, w_in_ab, w_out_ab, conv_w, conv_b, gate_a_w, gate_a_b, gate_x_w, gate_x_b, lru_lambda, w_in_c, w_out_c, w_cmp, pe_cmp, ffn_w1, ffn_w3, ffn_w2):
    bp, t, d = x_prompt.shape
    bs, ts, _ = x_sample.shape
    depth = norm_mix.shape[0]
    bf = lambda z: z.astype(BF16)
    tm_p, tm_s = 512, bs * ts
    yp = x_prompt.reshape(bp * t, d)
    ys = x_sample.reshape(bs * ts, d)
    outs = {k: [] for k in ("dil_p", "dil_s", "conv_p", "conv_s", "rnn_p", "rnn_s",
                            "win_p", "win_s", "cmp_p", "cmp_s", "sel_p", "sel_s")}
    for layer in range(depth):
        li = layer // 2
        last = layer == depth - 1
        ffn = (norm_ffn[layer], bf(ffn_w1[layer]), bf(ffn_w3[layer]), bf(ffn_w2[layer]), norm_out if last else None)
        if layer % 2 == 0:
            rw = conv_w.shape[2]
            aw = (w_in_ab.shape[2] - 2 * rw) // 3
            n_heads = aw // HEAD_DIM
            w_in = bf(w_in_ab[li])
            splits = (aw, 2 * aw, rw, rw)
            wa, wx = bf(_block_diag(gate_a_w[li])), bf(_block_diag(gate_x_w[li]))
            lru = (conv_w[li], conv_b[li], wa, gate_a_b[li], wx, gate_x_b[li], lru_lambda[li])
            w_outs = bf(w_out_ab[li])
            q, kv, xr, gate = _norm_proj(yp, norm_mix[layer], w_in, splits, tm_p)
            o_att = _dil_attn(q.reshape(bp, t, aw), kv.reshape(bp, t, 2 * aw), n_heads)
            o_rnn, conv_new, h_new = _rglru(xr.reshape(bp, t, rw), gate.reshape(bp, t, rw),
                                            jnp.zeros((bp, CONV_W - 1, rw), F32), jnp.zeros((bp, rw), F32),
                                            *lru, bp, 256)
            yp = _out_ffn(yp, [o_att.reshape(bp * t, aw), o_rnn.reshape(bp * t, rw)], w_outs, *ffn, tm_p)
            keep = min(DIL_MAX, t)
            outs["dil_p"].append(kv.reshape(bp, t, 2, n_heads, HEAD_DIM)[:, t - keep:])
            outs["conv_p"].append(conv_new)
            outs["rnn_p"].append(h_new)
            q, kv, xr, gate = _norm_proj(ys, norm_mix[layer], w_in, splits, tm_s)
            ctx = jnp.concatenate([cache_dil_kv[li], kv.reshape(bs, ts, 2, n_heads, HEAD_DIM)], axis=1)
            o_att = _dil_attn_sample(q.reshape(bs, ts, n_heads, HEAD_DIM), ctx, n_heads)
            o_rnn, conv_new, h_new = _rglru(xr.reshape(bs, ts, rw), gate.reshape(bs, ts, rw),
                                            state_conv[li], state_rnn[li], *lru, 8, ts)
            ys = _out_ffn(ys, [o_att.reshape(bs * ts, aw), o_rnn.reshape(bs * ts, rw)], w_outs, *ffn, tm_s)
            outs["dil_s"].append(ctx[:, ts:])
            outs["conv_s"].append(conv_new)
            outs["rnn_s"].append(h_new)
        else:
            n_groups = cache_win_kv.shape[4]
            kvw = 2 * n_groups * HEAD_DIM
            qw = w_out_c.shape[1]
            rep = qw // HEAD_DIM // n_groups
            w_in = bf(jnp.concatenate([w_in_c[li][:, :qw + 3 * kvw],
                                       _gate_columns(w_in_c[li][:, qw + 3 * kvw:], n_groups)], axis=1))
            gw = n_groups // 2 * LANES
            splits = (qw, kvw, kvw, kvw, gw)
            wbd, pe = _compress_weights(w_cmp[li], pe_cmp[li], n_groups)
            w_outs = bf(w_out_c[li])
            kv5 = lambda z, n: z.reshape(n, -1, 2, n_groups, HEAD_DIM)
            q, kv_cmp, kv_sel, kv_win, gates = _norm_proj(yp, norm_mix[layer], w_in, splits, tm_p)
            o = _nsa_prompt(q.reshape(bp, t, qw), kv_cmp.reshape(bp, t, kvw), kv_sel.reshape(bp, t, kvw),
                            kv_win.reshape(bp, t, kvw), gates.reshape(bp, t, gw), wbd, pe, n_groups)
            yp = _out_ffn(yp, [o.reshape(bp * t, qw)], w_outs, *ffn, tm_p)
            keep = min(WIN, t)
            outs["win_p"].append(kv5(kv_win, bp)[:, t - keep:])
            outs["cmp_p"].append(kv5(kv_cmp, bp))
            outs["sel_p"].append(kv5(kv_sel, bp))
            q, kv_cmp, kv_sel, kv_win, gates = _norm_proj(ys, norm_mix[layer], w_in, splits, tm_s)
            past_cmp = cache_cmp_kv[li][page_table].reshape((bs, -1) + cache_cmp_kv.shape[3:])
            past_sel = cache_sel_kv[li][page_table].reshape((bs, -1) + cache_sel_kv.shape[3:])
            gate_logits = gates.reshape(bs, ts, n_groups // 2, LANES)[..., :2 * rep * 3].reshape(bs, ts, n_groups, rep, 3)
            o, win_new = _nsa_sample(q.reshape(bs, ts, n_groups, rep, HEAD_DIM), kv5(kv_cmp, bs), kv5(kv_sel, bs),
                                     kv5(kv_win, bs), gate_logits, past_cmp, past_sel, cache_win_kv[li],
                                     w_cmp[li], pe_cmp[li])
            ys = _out_ffn(ys, [o.reshape(bs * ts, qw)], w_outs, *ffn, tm_s)
            outs["win_s"].append(win_new)
            outs["cmp_s"].append(kv5(kv_cmp, bs))
            outs["sel_s"].append(kv5(kv_sel, bs))
    st = lambda k: jnp.stack(outs[k])
    return (yp.reshape(bp, t, d), ys.reshape(bs, ts, d), st("dil_p"), st("dil_s"), st("conv_p"), st("conv_s"),
            st("rnn_p"), st("rnn_s"), st("win_p"), st("win_s"), st("cmp_p"), st("cmp_s"), st("sel_p"), st("sel_s"))
```

```python
import functools

import jax
import jax.numpy as jnp
from jax import lax
from jax.experimental import pallas as pl
from jax.experimental.pallas import tpu as pltpu

HEAD_DIM = 64
LANES = 128
DIL_PAIRS = ((128, 1), (512, 4), (2048, 16))
DIL_MAX = 2048
DIL_KEYS = 128
CONV_W = 4
LRU_C = 8.0
CMP_LEN = 32
CMP_STRIDE = 16
SEL_BLOCK = 64
SEL_TOP = 16
WIN = 512
NEG = -1e30
FORCE_BONUS = 1e3
EPS = 1e-6
VMEM_LIMIT = 56 * 1024 * 1024

F32 = jnp.float32
BF16 = jnp.bfloat16


def _cparams(n_grid):
    return pltpu.CompilerParams(dimension_semantics=("arbitrary",) * n_grid,
                                vmem_limit_bytes=VMEM_LIMIT)


def _resident(shape):
    return pl.BlockSpec(shape, lambda *_: (0,) * len(shape), pipeline_mode=pl.Buffered(1))


def _rms(x, g):
    return x * lax.rsqrt(jnp.mean(x * x, axis=-1, keepdims=True) + EPS) * g


def _alibi_slopes(n):
    return 2.0 ** (-8.0 * jnp.arange(1, n + 1, dtype=F32) / n)


def _norm_proj_kernel(x_ref, g_ref, w_ref, *out_refs):
    h = _rms(x_ref[...], g_ref[...]).astype(BF16)
    off = 0
    for o_ref in out_refs:
        n = o_ref.shape[-1]
        o_ref[...] = jnp.dot(h, w_ref[:, off:off + n], preferred_element_type=F32)
        off += n


def _norm_proj(x, g, w, splits, tm):
    n, d = x.shape
    assert n % tm == 0 and sum(splits) == w.shape[1]
    return pl.pallas_call(
        _norm_proj_kernel,
        grid=(n // tm,),
        in_specs=[pl.BlockSpec((tm, d), lambda i: (i, 0)),
                  _resident((1, d)),
                  _resident(w.shape)],
        out_specs=[pl.BlockSpec((tm, s), lambda i: (i, 0)) for s in splits],
        out_shape=[jax.ShapeDtypeStruct((n, s), F32) for s in splits],
        compiler_params=_cparams(1),
        name="norm_proj",
    )(x, g.reshape(1, d), w)


def _out_ffn_kernel(*refs, n_mix, n_chunks, final_norm):
    res_ref = refs[0]
    a_refs = refs[1:1 + n_mix]
    wo_ref, gf_ref, w1_ref, w3_ref, w2_ref = refs[1 + n_mix:6 + n_mix]
    rest = refs[6 + n_mix:]
    go_ref = rest[0] if final_norm else None
    o_ref = rest[-1]
    mix = jnp.concatenate([a_ref[...].astype(BF16) for a_ref in a_refs], axis=1)
    y = res_ref[...] + jnp.dot(mix, wo_ref[...], preferred_element_type=F32)
    h = _rms(y, gf_ref[...]).astype(BF16)
    ch = w1_ref.shape[1] // n_chunks
    for c in range(n_chunks):
        a = jnp.dot(h, w1_ref[:, c * ch:(c + 1) * ch], preferred_element_type=F32)
        b = jnp.dot(h, w3_ref[:, c * ch:(c + 1) * ch], preferred_element_type=F32)
        act = (a * (1.0 / (1.0 + jnp.exp(-a))) * b).astype(BF16)
        y = y + jnp.dot(act, w2_ref[c * ch:(c + 1) * ch, :], preferred_element_type=F32)
    if final_norm:
        y = _rms(y, go_ref[...])
    o_ref[...] = y


def _out_ffn(res, mixes, w_out, g_ffn, w1, w3, w2, g_out, tm):
    n, d = res.shape
    hidden = w1.shape[1]
    n_chunks = hidden // 256
    assert n % tm == 0 and hidden % 256 == 0
    final_norm = g_out is not None
    row = lambda width: pl.BlockSpec((tm, width), lambda i: (i, 0))
    in_specs = [row(d)] + [row(m.shape[1]) for m in mixes] + [_resident(w_out.shape)]
    in_specs += [_resident((1, d)), _resident(w1.shape), _resident(w3.shape), _resident(w2.shape)]
    args = [res, *mixes, w_out, g_ffn.reshape(1, d), w1, w3, w2]
    if final_norm:
        in_specs.append(_resident((1, d)))
        args.append(g_out.reshape(1, d))
    return pl.pallas_call(
        functools.partial(_out_ffn_kernel, n_mix=len(mixes), n_chunks=n_chunks, final_norm=final_norm),
        grid=(n // tm,),
        in_specs=in_specs,
        out_specs=row(d),
        out_shape=jax.ShapeDtypeStruct((n, d), F32),
        compiler_params=_cparams(1),
        name="out_ffn",
    )(*args)


def _dil_attn_kernel(slope_ref, q_ref, k_ref, v_ref, o_ref, *scratch, seq):
    ob_refs, lse_refs = scratch[:3], scratch[3:]
    hp = pl.program_id(1)
    lane_lo = lax.broadcasted_iota(jnp.int32, (1, LANES), 1) < HEAD_DIM
    slopes = [slope_ref[pl.ds(2 * hp + j, 1), 0:1] for j in range(2)]
    tq, tk = DIL_KEYS, 2 * DIL_KEYS
    iq = lax.broadcasted_iota(jnp.int32, (tq, tk), 0)
    ik = lax.broadcasted_iota(jnp.int32, (tq, tk), 1)

    for br, (_, dil) in enumerate(DIL_PAIRS):
        n_blk = seq // dil // tq

        def block(idx, carry, br=br, dil=dil, n_blk=n_blk):
            r = idx // n_blk
            blk = idx % n_blk
            a0 = blk * tq
            ka0 = jnp.maximum(blk - 1, 0) * tq
            q_rows = pl.ds(r + dil * a0, tq, stride=dil) if dil > 1 else pl.ds(a0, tq)
            k_rows = pl.ds(r + dil * ka0, tk, stride=dil) if dil > 1 else pl.ds(ka0, tk)
            q = q_ref[q_rows, :] * (HEAD_DIM ** -0.5)
            k = k_ref[k_rows, :].astype(BF16)
            v = v_ref[k_rows, :].astype(BF16)
            da = (a0 - ka0) + iq - ik
            valid = (da >= 0) & (da <= DIL_KEYS)
            dist = (dil * da).astype(F32)
            outs, lses = [], []
            for j in range(2):
                qj = jnp.where(lane_lo if j == 0 else ~lane_lo, q, 0.0).astype(BF16)
                s = lax.dot_general(qj, k, (((1,), (1,)), ((), ())), preferred_element_type=F32)
                s = jnp.where(valid, s - slopes[j] * dist, NEG)
                m = jnp.max(s, axis=-1, keepdims=True)
                p = jnp.exp(s - m)
                l = jnp.sum(p, axis=-1, keepdims=True)
                outs.append(jnp.dot(p.astype(BF16), v, preferred_element_type=F32) / l)
                lses.append(m + jnp.log(l))
            ob_refs[br][q_rows, :] = jnp.where(lane_lo, outs[0], outs[1])
            lse_refs[br][q_rows, :] = jnp.where(lane_lo, lses[0], lses[1])
            return carry

        lax.fori_loop(0, dil * n_blk, block, 0)

    rows = 512
    def merge(i, carry):
        sl = pl.ds(i * rows, rows)
        ls = [lse_refs[b][sl, :] for b in range(3)]
        m = jnp.maximum(jnp.maximum(ls[0], ls[1]), ls[2])
        es = [jnp.exp(x - m) for x in ls]
        num = es[0] * ob_refs[0][sl, :] + es[1] * ob_refs[1][sl, :] + es[2] * ob_refs[2][sl, :]
        o_ref[sl, :] = num / (es[0] + es[1] + es[2])
        return carry
    lax.fori_loop(0, seq // rows, merge, 0)


def _dil_attn(q, kv, n_heads):
    b, t, width = q.shape
    n_hp = width // LANES
    assert t % (16 * 2 * DIL_KEYS) == 0
    slopes = jnp.broadcast_to(_alibi_slopes(n_heads)[:, None], (n_heads, LANES))
    blk = lambda off: pl.BlockSpec((None, t, LANES), lambda i, j: (i, 0, off + j))
    return pl.pallas_call(
        functools.partial(_dil_attn_kernel, seq=t),
        grid=(b, n_hp),
        in_specs=[_resident((n_heads, LANES)), blk(0), blk(0), blk(n_hp)],
        out_specs=blk(0),
        out_shape=jax.ShapeDtypeStruct((b, t, width), F32),
        scratch_shapes=[pltpu.VMEM((t, LANES), F32)] * 6,
        compiler_params=_cparams(2),
        name="dil_attn",
    )(slopes, q, kv, kv)


def _rglru_kernel(xr_ref, gate_ref, cprev_ref, hprev_ref, cw_ref, cb_ref, wa_ref, ba_ref,
                  wx_ref, bx_ref, lam_ref, y_ref, cnew_ref, hnew_ref,
                  xin_ref, a_ref, u_ref, h_ref):
    ti = pl.program_id(1)
    bb, tc, width = xr_ref.shape
    pad = 8
    tail = CONV_W - 1

    @pl.when(ti == 0)
    def _():
        xin_ref[:, pad - tail:pad, :] = cprev_ref[...]
        h_ref[...] = hprev_ref[...]

    xin_ref[:, pad:pad + tc, :] = xr_ref[...]
    lam = lam_ref[...]
    neg_softplus_c = -LRU_C * (jnp.maximum(-lam, 0.0) + jnp.log1p(jnp.exp(-jnp.abs(lam))))
    for b in range(bb):
        xc = cb_ref[...] + sum(xin_ref[b, pad - tail + k:pad - tail + k + tc, :] * cw_ref[k:k + 1, :]
                               for k in range(CONV_W))
        xcb = xc.astype(BF16)
        ra = jnp.dot(xcb, wa_ref[...], preferred_element_type=F32) + ba_ref[...]
        rx = jnp.dot(xcb, wx_ref[...], preferred_element_type=F32) + bx_ref[...]
        rg = 1.0 / (1.0 + jnp.exp(-ra))
        ig = 1.0 / (1.0 + jnp.exp(-rx))
        log_a = neg_softplus_c * rg
        a = jnp.exp(log_a)
        a_ref[b] = a
        u_ref[b] = jnp.sqrt(jnp.tanh(-log_a) * (1.0 + a * a)) * ig * xc

    def step(t, hs):
        new = []
        for b in range(bb):
            hb = a_ref[b, pl.ds(t, 1), :] * hs[b] + u_ref[b, pl.ds(t, 1), :]
            u_ref[b, pl.ds(t, 1), :] = hb
            new.append(hb)
        return tuple(new)

    hs = lax.fori_loop(0, tc, step, tuple(h_ref[b:b + 1, :] for b in range(bb)), unroll=8)
    for b in range(bb):
        h_ref[b:b + 1, :] = hs[b]
    g = gate_ref[...]
    cdf = 0.5 * (1.0 + jnp.tanh(0.7978845608028654 * (g + 0.044715 * (g * g * g))))
    y_ref[...] = u_ref[...] * (g * cdf)
    xin_ref[:, pad - tail:pad, :] = xin_ref[:, pad + tc - tail:pad + tc, :]
    cnew_ref[...] = xin_ref[:, pad - tail:pad, :]
    hnew_ref[...] = h_ref[...]


def _rglru(xr, gate, conv_prev, h_prev, conv_w, conv_b, wa_bd, ba, wx_bd, bx, lam, bb, tc):
    b, t, width = xr.shape
    assert b % bb == 0 and t % tc == 0 and tc >= CONV_W - 1
    seq_blk = pl.BlockSpec((bb, tc, width), lambda i, j: (i, j, 0))
    vec = _resident((1, width))
    return pl.pallas_call(
        _rglru_kernel,
        grid=(b // bb, t // tc),
        in_specs=[seq_blk, seq_blk,
                  pl.BlockSpec((bb, CONV_W - 1, width), lambda i, j: (i, 0, 0)),
                  pl.BlockSpec((bb, width), lambda i, j: (i, 0)),
                  _resident((CONV_W, width)), vec, _resident(wa_bd.shape), vec,
                  _resident(wx_bd.shape), vec, vec],
        out_specs=[seq_blk,
                   pl.BlockSpec((bb, CONV_W - 1, width), lambda i, j: (i, 0, 0)),
                   pl.BlockSpec((bb, width), lambda i, j: (i, 0))],
        out_shape=[jax.ShapeDtypeStruct((b, t, width), F32),
                   jax.ShapeDtypeStruct((b, CONV_W - 1, width), F32),
                   jax.ShapeDtypeStruct((b, width), F32)],
        scratch_shapes=[pltpu.VMEM((bb, tc + 8, width), F32), pltpu.VMEM((bb, tc, width), F32),
                        pltpu.VMEM((bb, tc, width), F32), pltpu.VMEM((bb, width), F32)],
        compiler_params=_cparams(2),
        name="rglru",
    )(xr, gate, conv_prev, h_prev, conv_w, conv_b.reshape(1, width), wa_bd, ba.reshape(1, width),
      wx_bd, bx.reshape(1, width), lam.reshape(1, width))


def _block_diag(w):
    n, bi, bj = w.shape
    eye = jnp.eye(n, dtype=w.dtype)
    return (eye[:, None, :, None] * w[:, :, None, :]).reshape(n * bi, n * bj)


def _lane_lo():
    return lax.broadcasted_iota(jnp.int32, (1, LANES), 1) < HEAD_DIM


def _spread_kv(k2, v2, half):
    lo = _lane_lo()
    own = lo if half == 0 else ~lo
    k_own = jnp.where(own, k2, 0.0)
    v_own = jnp.where(own, v2, 0.0)
    k_other = pltpu.roll(k_own, HEAD_DIM, axis=1)
    v_both = v_own + pltpu.roll(v_own, HEAD_DIM, axis=1)
    k_lo, k_hi = (k_own, k_other) if half == 0 else (k_other, k_own)
    return k_lo.astype(BF16), k_hi.astype(BF16), v_both.astype(BF16)


def _group_heads(q_ref, gl):
    base = gl * 4 * HEAD_DIM
    qa = (q_ref[:, base:base + LANES] * (HEAD_DIM ** -0.5)).astype(BF16)
    qb = (q_ref[:, base + LANES:base + 2 * LANES] * (HEAD_DIM ** -0.5)).astype(BF16)
    return ((qa, 0), (qa, 1), (qb, 0), (qb, 1))


def _nt_dot(a, b):
    return lax.dot_general(a, b, (((1,), (1,)), ((), ())), preferred_element_type=F32)


def _compress_kernel(*refs, n_ch, merged):
    x_refs, (w_ref, pe_ref, o_ref) = refs[:-3], refs[-3:]
    per_half = len(x_refs) // 2
    half = per_half * LANES
    for c in range(2):
        acc = [None, None]
        pe_acc = None
        for s in range(CMP_STRIDE):
            xs = jnp.concatenate([x_ref[pl.ds(s, n_ch, stride=CMP_STRIDE), :]
                                  for x_ref in x_refs[c * per_half:(c + 1) * per_half]], axis=1).astype(BF16)
            for i in range(2):
                w = w_ref[(i * CMP_STRIDE + s) * 2 + c]
                d = jnp.dot(xs, w, preferred_element_type=F32)
                acc[i] = d if acc[i] is None else acc[i] + d
                pe_row = jnp.broadcast_to(pe_ref[i * CMP_STRIDE + s:i * CMP_STRIDE + s + 1, c * half:(c + 1) * half],
                                          (8, half)).astype(BF16)
                dp = jnp.dot(pe_row, w, preferred_element_type=F32)
                pe_acc = dp if pe_acc is None else pe_acc + dp
        if merged:
            o_ref[:, c * half:(c + 1) * half] = acc[0] + pltpu.roll(acc[1], n_ch - 1, axis=0) + pe_acc[0:1, :]
        else:
            o_ref[:, c * half:(c + 1) * half] = acc[0] + pe_acc[0:1, :]
            o_ref[:, (2 + c) * half:(3 + c) * half] = acc[1]


def _compress_weights(w_cmp, pe_cmp, n_groups):
    n_r = CMP_LEN // CMP_STRIDE
    w = w_cmp.reshape(n_r * CMP_STRIDE * 2, HEAD_DIM, HEAD_DIM)
    eye = jnp.eye(n_groups, dtype=w.dtype)
    wbd = (eye[None, :, None, :, None] * w[:, None, :, None, :]).reshape(-1, n_groups * HEAD_DIM, n_groups * HEAD_DIM)
    pe = jnp.broadcast_to(pe_cmp[:, :, None, :], (CMP_LEN, 2, n_groups, HEAD_DIM)).reshape(CMP_LEN, -1)
    return wbd.astype(BF16), pe


def _compress(ctx, wbd, pe):
    b, tc, width = ctx.shape
    n_ch = tc // CMP_STRIDE
    return pl.pallas_call(
        functools.partial(_compress_kernel, n_ch=n_ch, merged=True),
        grid=(b,),
        in_specs=[pl.BlockSpec((None, tc, LANES), functools.partial(lambda j, i: (i, 0, j), j))
                  for j in range(width // LANES)] + [_resident(wbd.shape), _resident(pe.shape)],
        out_specs=pl.BlockSpec((None, n_ch, width), lambda i: (i, 0, 0)),
        out_shape=jax.ShapeDtypeStruct((b, n_ch, width), F32),
        compiler_params=_cparams(1),
        name="nsa_compress",
    )(*([ctx] * (width // LANES)), wbd, pe)


def _cmp_select_kernel(slope_ref, c2s_ref, q_ref, kc_ref, vc_ref, o_ref, sel_ref, *, n_c, n_sel):
    gp, qi = pl.program_id(1), pl.program_id(2)
    tq = q_ref.shape[0]
    n_ch = kc_ref.shape[0]
    lo = _lane_lo()
    qp = qi * tq + lax.broadcasted_iota(jnp.int32, (tq, 1), 0)
    n_idx = lax.broadcasted_iota(jnp.int32, (1, n_ch), 1)
    cmp_end = n_idx * CMP_STRIDE + (CMP_LEN - 1)
    valid = (cmp_end <= qp) & (n_idx < n_c)
    dist = (qp - cmp_end).astype(F32)
    blk = lax.broadcasted_iota(jnp.int32, (1, LANES), 1)
    cb = qp // SEL_BLOCK
    blk_ok = (blk <= cb) & (blk < n_sel)
    forced = (blk == 0) | (blk == cb) | (blk == cb - 1)
    blk_t = lax.broadcasted_iota(jnp.int32, (n_sel, tq), 0)
    for gl in range(2):
        keys = _spread_kv(kc_ref[...], vc_ref[...], gl)
        imp = jnp.zeros((tq, LANES), F32)
        outs = []
        for r, (qh, variant) in enumerate(_group_heads(q_ref, gl)):
            slope = slope_ref[pl.ds((2 * gp + gl) * 4 + r, 1), 0:1]
            s = jnp.where(valid, _nt_dot(qh, keys[variant]) - slope * dist, NEG)
            m = jnp.max(s, axis=-1, keepdims=True)
            p = jnp.where(valid, jnp.exp(s - m), 0.0)
            p = (p / jnp.maximum(jnp.sum(p, axis=-1, keepdims=True), 1e-30)).astype(BF16)
            outs.append(jnp.dot(p, keys[2], preferred_element_type=F32))
            imp = imp + jnp.dot(p, c2s_ref[...], preferred_element_type=F32)
        base = gl * 4 * HEAD_DIM
        o_ref[:, base:base + LANES] = jnp.where(lo, outs[0], outs[1])
        o_ref[:, base + LANES:base + 2 * LANES] = jnp.where(lo, outs[2], outs[3])
        score = jnp.where(blk_ok, imp + FORCE_BONUS * forced.astype(F32), NEG)
        score_t = score.T[0:n_sel, :]
        rank = jnp.zeros((n_sel, tq), F32)
        for k in range(n_sel):
            row = score_t[k:k + 1, :]
            ahead = (row > score_t) | ((row == score_t) & (k < blk_t))
            rank = rank + ahead.astype(F32)
        chosen = ((rank < SEL_TOP) & (score_t > 0.5 * NEG)).astype(F32)
        chosen = jnp.concatenate([chosen, jnp.zeros((LANES - n_sel, tq), F32)], axis=0) if n_sel < LANES else chosen
        sel_ref[gl] = chosen.T.astype(BF16)


def _cmp_select(q, kvc, n_c, n_sel, n_groups, tq):
    b, t, width = q.shape
    n_ch = kvc.shape[1]
    n_gp = n_groups // 2
    n_heads = width // HEAD_DIM
    assert tq == LANES and n_sel <= LANES and n_sel % 8 == 0 and t % tq == 0
    slopes = jnp.broadcast_to(_alibi_slopes(n_heads)[:, None], (n_heads, LANES))
    cs = jnp.arange(n_ch)[:, None] * CMP_STRIDE
    ss = jnp.arange(LANES)[None, :] * SEL_BLOCK
    ov = jnp.maximum(jnp.minimum(cs + CMP_LEN, ss + SEL_BLOCK) - jnp.maximum(cs, ss), 0).astype(F32) / CMP_STRIDE
    c2s = jnp.where((jnp.arange(n_ch)[:, None] < n_c) & (jnp.arange(LANES)[None, :] < n_sel), ov, 0.0).astype(BF16)
    return pl.pallas_call(
        functools.partial(_cmp_select_kernel, n_c=n_c, n_sel=n_sel),
        grid=(b, n_gp, t // tq),
        in_specs=[_resident(slopes.shape), _resident(c2s.shape),
                  pl.BlockSpec((None, tq, 2 * 4 * HEAD_DIM), lambda i, j, k: (i, k, j)),
                  pl.BlockSpec((None, n_ch, LANES), lambda i, j, k: (i, 0, j)),
                  pl.BlockSpec((None, n_ch, LANES), lambda i, j, k: (i, 0, n_gp + j))],
        out_specs=[pl.BlockSpec((None, tq, 2 * 4 * HEAD_DIM), lambda i, j, k: (i, k, j)),
                   pl.BlockSpec((None, 2, tq, LANES), lambda i, j, k: (i, j, k, 0))],
        out_shape=[jax.ShapeDtypeStruct((b, t, width), F32),
                   jax.ShapeDtypeStruct((b, n_groups, t, LANES), BF16)],
        compiler_params=_cparams(3),
        name="nsa_cmp_select",
    )(slopes, c2s, q, kvc, kvc)


def _stage_kv(k_ref, v_ref, kv_scr):
    for gl in range(2):
        k_lo, k_hi, v_both = _spread_kv(k_ref[...], v_ref[...], gl)
        kv_scr[3 * gl + 0][...] = k_lo
        kv_scr[3 * gl + 1][...] = k_hi
        kv_scr[3 * gl + 2][...] = v_both


def _sel_attn_kernel(slope_ref, exp_ref, q_ref, sel_ref, k_ref, v_ref, o_ref, *kv_scr, tk):
    gp, qi = pl.program_id(1), pl.program_id(2)
    tq = q_ref.shape[0]
    lo = _lane_lo()

    @pl.when(qi == 0)
    def _():
        _stage_kv(k_ref, v_ref, kv_scr)

    qp = qi * tq + lax.broadcasted_iota(jnp.int32, (tq, 1), 0)
    n_kt = (qi * tq + tq - 1) // tk + 1
    for gl in range(2):
        heads = _group_heads(q_ref, gl)
        slopes = [slope_ref[pl.ds((2 * gp + gl) * 4 + r, 1), 0:1] for r in range(4)]
        sel = sel_ref[gl]
        k_scr = (kv_scr[3 * gl], kv_scr[3 * gl + 1])
        v_scr = kv_scr[3 * gl + 2]

        def tile(kt, carry, heads=heads, slopes=slopes, sel=sel, k_scr=k_scr, v_scr=v_scr):
            rows = pl.ds(pl.multiple_of(kt * tk, tk), tk)
            kpos = kt * tk + lax.broadcasted_iota(jnp.int32, (1, tk), 1)
            picked = _nt_dot(sel, exp_ref[rows, :]) > 0.5
            ok = picked & (kpos <= qp)
            dist = (qp - kpos).astype(F32)
            v = v_scr[rows, :]
            new = []
            for r, (qh, variant) in enumerate(heads):
                m_old, l_old, acc_old = carry[r]
                s = jnp.where(ok, _nt_dot(qh, k_scr[variant][rows, :]) - slopes[r] * dist, NEG)
                m_new = jnp.maximum(m_old, jnp.max(s, axis=-1, keepdims=True))
                alpha = jnp.exp(m_old - m_new)
                p = jnp.where(ok, jnp.exp(s - m_new), 0.0)
                l_new = alpha * l_old + jnp.sum(p, axis=-1, keepdims=True)
                acc_new = alpha * acc_old + jnp.dot(p.astype(BF16), v, preferred_element_type=F32)
                new.append((m_new, l_new, acc_new))
            return tuple(new)

        init = tuple((jnp.full((tq, 1), NEG, F32), jnp.zeros((tq, 1), F32), jnp.zeros((tq, LANES), F32))
                     for _ in range(4))
        res = lax.fori_loop(0, n_kt, tile, init)
        outs = [acc / jnp.maximum(l, 1e-30) for (_, l, acc) in res]
        base = gl * 4 * HEAD_DIM
        o_ref[:, base:base + LANES] = jnp.where(lo, outs[0], outs[1])
        o_ref[:, base + LANES:base + 2 * LANES] = jnp.where(lo, outs[2], outs[3])


def _sel_attn(q, kv, sel, n_groups, tq, tk):
    b, t, width = q.shape
    n_gp = n_groups // 2
    n_heads = width // HEAD_DIM
    assert t % tq == 0 and t % tk == 0 and tk % SEL_BLOCK == 0
    slopes = jnp.broadcast_to(_alibi_slopes(n_heads)[:, None], (n_heads, LANES))
    expand = (jnp.arange(t)[:, None] // SEL_BLOCK == jnp.arange(LANES)[None, :]).astype(BF16)
    return pl.pallas_call(
        functools.partial(_sel_attn_kernel, tk=tk),
        grid=(b, n_gp, t // tq),
        in_specs=[_resident(slopes.shape), _resident(expand.shape),
                  pl.BlockSpec((None, tq, 2 * 4 * HEAD_DIM), lambda i, j, k: (i, k, j)),
                  pl.BlockSpec((None, 2, tq, LANES), lambda i, j, k: (i, j, k, 0)),
                  pl.BlockSpec((None, t, LANES), lambda i, j, k: (i, 0, j)),
                  pl.BlockSpec((None, t, LANES), lambda i, j, k: (i, 0, n_gp + j))],
        out_specs=pl.BlockSpec((None, tq, 2 * 4 * HEAD_DIM), lambda i, j, k: (i, k, j)),
        out_shape=jax.ShapeDtypeStruct((b, t, width), F32),
        scratch_shapes=[pltpu.VMEM((t, LANES), BF16)] * 6,
        compiler_params=_cparams(3),
        name="nsa_sel_attn",
    )(slopes, expand, q, sel, kv, kv)


def _win_combine_kernel(slope_ref, q_ref, gate_ref, ocmp_ref, osel_ref, k_ref, v_ref, o_ref, *kv_scr, seq):
    gp, qi = pl.program_id(1), pl.program_id(2)
    tq = q_ref.shape[0]
    tk = WIN + tq
    lo = _lane_lo()

    @pl.when(qi == 0)
    def _():
        _stage_kv(k_ref, v_ref, kv_scr)

    k0 = jnp.clip(qi * tq - WIN, 0, seq - tk)
    rows = pl.ds(pl.multiple_of(k0, tq), tk)
    qp = qi * tq + lax.broadcasted_iota(jnp.int32, (tq, 1), 0)
    kpos = k0 + lax.broadcasted_iota(jnp.int32, (1, tk), 1)
    dist_i = qp - kpos
    ok = (dist_i >= 0) & (dist_i <= WIN)
    dist = dist_i.astype(F32)
    gates = 1.0 / (1.0 + jnp.exp(-gate_ref[...]))
    for gl in range(2):
        outs = []
        v = kv_scr[3 * gl + 2][rows, :]
        for r, (qh, variant) in enumerate(_group_heads(q_ref, gl)):
            slope = slope_ref[pl.ds((2 * gp + gl) * 4 + r, 1), 0:1]
            s = jnp.where(ok, _nt_dot(qh, kv_scr[3 * gl + variant][rows, :]) - slope * dist, NEG)
            m = jnp.max(s, axis=-1, keepdims=True)
            p = jnp.where(ok, jnp.exp(s - m), 0.0)
            l = jnp.sum(p, axis=-1, keepdims=True)
            outs.append(jnp.dot(p.astype(BF16), v, preferred_element_type=F32) / jnp.maximum(l, 1e-30))
        for half in range(2):
            cols = slice(gl * 4 * HEAD_DIM + half * LANES, gl * 4 * HEAD_DIM + (half + 1) * LANES)
            o_win = jnp.where(lo, outs[2 * half], outs[2 * half + 1])
            c0 = (gl * 4 + 2 * half) * 3
            g_cmp, g_sel, g_win = (jnp.where(lo, gates[:, c0 + i:c0 + i + 1], gates[:, c0 + 3 + i:c0 + 4 + i])
                                   for i in range(3))
            o_ref[:, cols] = g_cmp * ocmp_ref[:, cols] + g_sel * osel_ref[:, cols] + g_win * o_win


def _win_combine(q, kv, gates, o_cmp, o_sel, n_groups, tq):
    b, t, width = q.shape
    n_gp = n_groups // 2
    n_heads = width // HEAD_DIM
    assert t % tq == 0 and t >= WIN + tq
    slopes = jnp.broadcast_to(_alibi_slopes(n_heads)[:, None], (n_heads, LANES))
    qblk = pl.BlockSpec((None, tq, 2 * 4 * HEAD_DIM), lambda i, j, k: (i, k, j))
    return pl.pallas_call(
        functools.partial(_win_combine_kernel, seq=t),
        grid=(b, n_gp, t // tq),
        in_specs=[_resident(slopes.shape), qblk,
                  pl.BlockSpec((None, tq, LANES), lambda i, j, k: (i, k, j)),
                  qblk, qblk,
                  pl.BlockSpec((None, t, LANES), lambda i, j, k: (i, 0, j)),
                  pl.BlockSpec((None, t, LANES), lambda i, j, k: (i, 0, n_gp + j))],
        out_specs=qblk,
        out_shape=jax.ShapeDtypeStruct((b, t, width), F32),
        scratch_shapes=[pltpu.VMEM((t, LANES), BF16)] * 6,
        compiler_params=_cparams(3),
        name="nsa_win_combine",
    )(slopes, q, gates, o_cmp, o_sel, kv, kv)


def _nsa_prompt(q, kv_cmp, kv_sel, kv_win, gates, wbd, pe, n_groups):
    t = q.shape[1]
    kvc = _compress(kv_cmp, wbd, pe)
    n_c = t // CMP_STRIDE - CMP_LEN // CMP_STRIDE + 1
    n_sel = -(-t // SEL_BLOCK)
    o_cmp, sel = _cmp_select(q, kvc, n_c, n_sel, n_groups, LANES)
    o_sel = _sel_attn(q, kv_sel, sel, n_groups, LANES, 4 * LANES)
    return _win_combine(q, kv_win, gates, o_cmp, o_sel, n_groups, LANES)


def _head_block_diag(x, n_rep, lanes_per_head, rows_per_head):
    tiled = jnp.concatenate([x] * n_rep, axis=0)
    row_h = lax.broadcasted_iota(jnp.int32, tiled.shape, 0) // rows_per_head
    lane_h = lax.broadcasted_iota(jnp.int32, tiled.shape, 1) // lanes_per_head
    return jnp.where(row_h == lane_h, tiled, 0.0)


def _head_diag_rows(o, n_rep, lanes_per_head, rows_per_head):
    row_h = lax.broadcasted_iota(jnp.int32, o.shape, 0) // rows_per_head
    lane_h = lax.broadcasted_iota(jnp.int32, o.shape, 1) // lanes_per_head
    kept = jnp.where(row_h == lane_h, o, 0.0)
    return jnp.sum(kept.reshape(n_rep, rows_per_head, o.shape[1]), axis=0)


def _dil_sample_kernel(slope_ref, q_ref, kvn_ref, cache_ref, o_ref, ctx_ref, k_scr, v_scr):
    ts, width = q_ref.shape
    past = cache_ref.shape[0]
    n_heads = width // HEAD_DIM
    rows = n_heads * ts
    pad_rows = k_scr.shape[0] - past
    new_k = jnp.concatenate([kvn_ref[:, :width], jnp.zeros((pad_rows - ts, width), F32)], axis=0)
    new_v = jnp.concatenate([kvn_ref[:, width:], jnp.zeros((pad_rows - ts, width), F32)], axis=0)
    k_scr[0:past, :] = cache_ref[:, :width].astype(BF16)
    v_scr[0:past, :] = cache_ref[:, width:].astype(BF16)
    k_scr[past:, :] = new_k.astype(BF16)
    v_scr[past:, :] = new_v.astype(BF16)
    qbd = _head_block_diag(q_ref[...] * (HEAD_DIM ** -0.5), n_heads, HEAD_DIM, ts).astype(BF16)
    s = _nt_dot(qbd, k_scr[...])
    t_row = lax.broadcasted_iota(jnp.int32, (rows, 1), 0) % ts
    dist = (past + t_row) - lax.broadcasted_iota(jnp.int32, (1, k_scr.shape[0]), 1)
    count = jnp.zeros(s.shape, F32)
    for window, dil in DIL_PAIRS:
        assert dil & (dil - 1) == 0
        count = count + ((dist >= 0) & (dist <= window) & ((dist & (dil - 1)) == 0)).astype(F32)
    s = jnp.where(count > 0, s - slope_ref[:, 0:1] * dist.astype(F32), NEG)
    m = jnp.max(s, axis=-1, keepdims=True)
    p = count * jnp.exp(s - m)
    l = jnp.sum(p, axis=-1, keepdims=True)
    o = jnp.dot(p.astype(BF16), v_scr[...], preferred_element_type=F32) / l
    o_ref[...] = _head_diag_rows(o, n_heads, HEAD_DIM, ts)
    ctx_ref[0:past - ts, :] = cache_ref[ts:past, :]
    ctx_ref[past - ts:past, :] = kvn_ref[...]


def _dil_sample(q, kv_new, cache, n_heads):
    b, ts, width = q.shape
    past = cache.shape[1]
    assert ts % 8 == 0 and past % LANES == 0
    slopes = jnp.broadcast_to(jnp.repeat(_alibi_slopes(n_heads), ts)[:, None], (n_heads * ts, LANES))
    per_b = lambda rows, w: pl.BlockSpec((None, rows, w), lambda i: (i, 0, 0))
    return pl.pallas_call(
        _dil_sample_kernel,
        grid=(b,),
        in_specs=[_resident(slopes.shape), per_b(ts, width), per_b(ts, 2 * width), per_b(past, 2 * width)],
        out_specs=[per_b(ts, width), per_b(past, 2 * width)],
        out_shape=[jax.ShapeDtypeStruct((b, ts, width), F32), jax.ShapeDtypeStruct(cache.shape, F32)],
        scratch_shapes=[pltpu.VMEM((past + LANES, width), BF16)] * 2,
        compiler_params=_cparams(1),
        name="dil_sample",
    )(slopes, q, kv_new, cache)


def _compress_pool(pool, wbd, pe, rows_per_step):
    rows, width = pool.shape
    assert rows % rows_per_step == 0 and rows_per_step % (8 * CMP_STRIDE) == 0
    n_ch = rows_per_step // CMP_STRIDE
    return pl.pallas_call(
        functools.partial(_compress_kernel, n_ch=n_ch, merged=False),
        grid=(rows // rows_per_step,),
        in_specs=[pl.BlockSpec((rows_per_step, LANES), functools.partial(lambda j, i: (i, j), j))
                  for j in range(width // LANES)] + [_resident(wbd.shape), _resident(pe.shape)],
        out_specs=pl.BlockSpec((n_ch, 2 * width), lambda i: (i, 0)),
        out_shape=jax.ShapeDtypeStruct((rows // CMP_STRIDE, 2 * width), F32),
        compiler_params=_cparams(1),
        name="nsa_compress_pool",
    )(*([pool] * (width // LANES)), wbd, pe)


def _sample_query(q_ref, ts, n_groups):
    gw = n_groups * HEAD_DIM
    n_rep = q_ref.shape[1] // gw
    return jnp.concatenate(
        [_head_block_diag(q_ref[:, r * gw:(r + 1) * gw] * (HEAD_DIM ** -0.5), n_groups, HEAD_DIM, ts) for r in range(n_rep)],
        axis=0).astype(BF16)


def _sample_rows_out(o, ts, n_groups):
    per_r = n_groups * ts
    return [_head_diag_rows(o[r * per_r:(r + 1) * per_r, :], n_groups, HEAD_DIM, ts) for r in range(o.shape[0] // per_r)]


def _nsa_sample_cmp_kernel(pt_ref, slope_ref, c2s_ref, q_ref, kvn_ref, cwin_ref, *rest, pages, past, n_c, n_sel):
    del pt_ref
    ab_refs = rest[:pages]
    xcmp_ref, xwin_ref, sel_ref, winout_ref, ab_scr, kwin_scr, vwin_scr = rest[pages:]
    j = pl.program_id(1)
    ts = q_ref.shape[0]
    n_groups = kvn_ref.shape[1] // (2 * HEAD_DIM)
    gw = n_groups * HEAD_DIM
    for k in range(pages):
        ab_scr[pl.ds(pl.multiple_of((j * pages + k) * 8, 8), 8), :] = ab_refs[k][...]

    @pl.when(j == pl.num_programs(1) - 1)
    def _():
        n_ch = ab_scr.shape[0]
        n_rows = slope_ref.shape[0]
        slope = slope_ref[:, 0:1]
        qbd = _sample_query(q_ref, ts, n_groups)
        t_row = lax.broadcasted_iota(jnp.int32, (n_rows, 1), 0) % ts
        qp = past + t_row
        kvc = ab_scr[:, 0:2 * gw] + pltpu.roll(ab_scr[:, 2 * gw:4 * gw], n_ch - 1, axis=0)
        n_idx = lax.broadcasted_iota(jnp.int32, (1, n_ch), 1)
        cmp_end = n_idx * CMP_STRIDE + (CMP_LEN - 1)
        valid = (cmp_end <= qp) & (n_idx < n_c)
        s = jnp.where(valid, _nt_dot(qbd, kvc[:, 0:gw].astype(BF16)) - slope * (qp - cmp_end).astype(F32), NEG)
        m = jnp.max(s, axis=-1, keepdims=True)
        p = jnp.where(valid, jnp.exp(s - m), 0.0)
        p = (p / jnp.maximum(jnp.sum(p, axis=-1, keepdims=True), 1e-30)).astype(BF16)
        for r, x in enumerate(_sample_rows_out(jnp.dot(p, kvc[:, gw:2 * gw].astype(BF16), preferred_element_type=F32), ts, n_groups)):
            xcmp_ref[r * ts:(r + 1) * ts, :] = x
        imp_rows = jnp.dot(p, c2s_ref[...], preferred_element_type=F32)
        gt = n_groups * ts
        imp = jnp.sum(imp_rows.reshape(n_rows // gt, gt, imp_rows.shape[1]), axis=0)
        blk = lax.broadcasted_iota(jnp.int32, (1, imp.shape[1]), 1)
        cb = (past + lax.broadcasted_iota(jnp.int32, (gt, 1), 0) % ts) // SEL_BLOCK
        forced = (blk == 0) | (blk == cb) | (blk == cb - 1)
        score = jnp.where((blk <= cb) & (blk < n_sel), imp + FORCE_BONUS * forced.astype(F32), NEG)
        removed = -3e38
        cur = jnp.where(blk < n_sel, score, removed)
        chosen = jnp.zeros(score.shape, F32)
        for _ in range(SEL_TOP):
            top = jnp.max(cur, axis=-1, keepdims=True)
            first = jnp.min(jnp.where(cur == top, blk, imp.shape[1]), axis=-1, keepdims=True)
            pick = blk == first
            chosen = jnp.where(pick & (top > 0.5 * NEG), 1.0, chosen)
            cur = jnp.where(pick, removed, cur)
        n_steps = sel_ref.shape[0]
        per_step = (past // SEL_BLOCK) // n_steps
        lane = lax.broadcasted_iota(jnp.int32, (1, LANES), 1)
        for st in range(n_steps):
            shifted = chosen if st == 0 else pltpu.roll(chosen, chosen.shape[1] - st * per_step, axis=1)
            n_here = per_step + (n_sel - n_steps * per_step if st == n_steps - 1 else 0)
            piece = jnp.where(lane < n_here, shifted[:, 0:LANES], 0.0)
            sel_ref[st] = jnp.concatenate([piece] * (n_rows // gt), axis=0).astype(BF16)
        w_past = cwin_ref.shape[0]
        pad = kwin_scr.shape[0] - w_past
        kwin_scr[0:w_past, :] = cwin_ref[:, 0:gw].astype(BF16)
        vwin_scr[0:w_past, :] = cwin_ref[:, gw:2 * gw].astype(BF16)
        zeros = jnp.zeros((pad - ts, gw), F32)
        kwin_scr[w_past:, :] = jnp.concatenate([kvn_ref[:, 0:gw], zeros], axis=0).astype(BF16)
        vwin_scr[w_past:, :] = jnp.concatenate([kvn_ref[:, gw:2 * gw], zeros], axis=0).astype(BF16)
        dist = (w_past + t_row) - lax.broadcasted_iota(jnp.int32, (1, kwin_scr.shape[0]), 1)
        ok = (dist >= 0) & (dist <= WIN)
        s = jnp.where(ok, _nt_dot(qbd, kwin_scr[...]) - slope * dist.astype(F32), NEG)
        m = jnp.max(s, axis=-1, keepdims=True)
        p = jnp.where(ok, jnp.exp(s - m), 0.0)
        l = jnp.maximum(jnp.sum(p, axis=-1, keepdims=True), 1e-30)
        o = jnp.dot(p.astype(BF16), vwin_scr[...], preferred_element_type=F32) / l
        for r, x in enumerate(_sample_rows_out(o, ts, n_groups)):
            xwin_ref[r * ts:(r + 1) * ts, :] = x
        winout_ref[0:w_past - ts, :] = cwin_ref[ts:w_past, :]
        winout_ref[w_past - ts:w_past, :] = kvn_ref[...]


def _sample_slopes(n_groups, n_rep, ts):
    slopes = _alibi_slopes(n_groups * n_rep).reshape(n_groups, n_rep).T
    return jnp.broadcast_to(jnp.repeat(slopes.reshape(-1), ts)[:, None], (n_rep * n_groups * ts, LANES))


def _nsa_sample_cmp(q, kv_win_new, cache_win, ab_pool, page_table, n_groups, sel_steps, pages):
    b, ts, qw = q.shape
    n_rep = qw // HEAD_DIM // n_groups
    gw = n_groups * HEAD_DIM
    n_pages = page_table.shape[1]
    chunks_per_page = ab_pool.shape[1]
    past = n_pages * chunks_per_page * CMP_STRIDE
    n_ch = (past + ts) // CMP_STRIDE
    assert n_ch == n_pages * chunks_per_page and n_pages % pages == 0
    n_c = n_ch - CMP_LEN // CMP_STRIDE + 1
    n_sel = -(-(past + ts) // SEL_BLOCK)
    assert n_sel <= 2 * LANES and n_sel - past // SEL_BLOCK + past // SEL_BLOCK // sel_steps <= LANES
    n_rows = n_rep * n_groups * ts
    slopes = _sample_slopes(n_groups, n_rep, ts)
    cs = jnp.arange(n_ch)[:, None] * CMP_STRIDE
    ss = jnp.arange(2 * LANES)[None, :] * SEL_BLOCK
    ov = jnp.maximum(jnp.minimum(cs + CMP_LEN, ss + SEL_BLOCK) - jnp.maximum(cs, ss), 0).astype(F32) / CMP_STRIDE
    c2s = jnp.where((jnp.arange(n_ch)[:, None] < n_c) & (jnp.arange(2 * LANES)[None, :] < n_sel), ov, 0.0).astype(BF16)
    w_past = cache_win.shape[1]
    per_b = lambda rows, w: pl.BlockSpec((None, rows, w), lambda i, j, pt: (i, 0, 0))
    page_spec = lambda k: pl.BlockSpec((None, chunks_per_page, ab_pool.shape[2]),
                                       lambda i, j, pt: (pt[i, j * pages + k], 0, 0))
    const = lambda shape: pl.BlockSpec(shape, lambda i, j, pt: (0,) * len(shape), pipeline_mode=pl.Buffered(1))
    return pl.pallas_call(
        functools.partial(_nsa_sample_cmp_kernel, pages=pages, past=past, n_c=n_c, n_sel=n_sel),
        grid_spec=pltpu.PrefetchScalarGridSpec(
            num_scalar_prefetch=1,
            grid=(b, n_pages // pages),
            in_specs=[const(slopes.shape), const(c2s.shape), per_b(ts, qw), per_b(ts, 2 * gw), per_b(w_past, 2 * gw)]
                     + [page_spec(k) for k in range(pages)],
            out_specs=[per_b(n_rep * ts, gw), per_b(n_rep * ts, gw),
                       pl.BlockSpec((None, sel_steps, n_rows, LANES), lambda i, j, pt: (i, 0, 0, 0)),
                       per_b(w_past, 2 * gw)],
            scratch_shapes=[pltpu.VMEM((n_ch, ab_pool.shape[2]), F32),
                            pltpu.VMEM((w_past + LANES, gw), BF16), pltpu.VMEM((w_past + LANES, gw), BF16)]),
        out_shape=[jax.ShapeDtypeStruct((b, n_rep * ts, gw), F32), jax.ShapeDtypeStruct((b, n_rep * ts, gw), F32),
                   jax.ShapeDtypeStruct((b, sel_steps, n_rows, LANES), BF16),
                   jax.ShapeDtypeStruct(cache_win.shape, F32)],
        compiler_params=_cparams(2),
        name="nsa_sample_cmp",
    )(page_table, slopes, c2s, q, kv_win_new, cache_win, *([ab_pool] * pages))


def _nsa_sample_sel_kernel(pt_ref, slope_ref, exp_ref, q_ref, kvn_ref, sel_ref, xcmp_ref, xwin_ref, gate_ref, *rest,
                           pages, past):
    del pt_ref
    page_refs = rest[:pages]
    o_ref, m_scr, l_scr, acc_scr, new_scr = rest[pages:]
    j = pl.program_id(1)
    ts = q_ref.shape[0]
    n_groups = kvn_ref.shape[1] // (2 * HEAD_DIM)
    gw = n_groups * HEAD_DIM
    n_rows = slope_ref.shape[0]
    page_rows = page_refs[0].shape[0]
    tk = pages * page_rows
    slope = slope_ref[:, 0:1]
    qbd = _sample_query(q_ref, ts, n_groups)
    t_row = lax.broadcasted_iota(jnp.int32, (n_rows, 1), 0) % ts

    @pl.when(j == 0)
    def _():
        m_scr[...] = jnp.full(m_scr.shape, NEG, F32)
        l_scr[...] = jnp.zeros(l_scr.shape, F32)
        acc_scr[...] = jnp.zeros(acc_scr.shape, F32)

    def update(k, v, ok, dist):
        s = jnp.where(ok, _nt_dot(qbd, k) - slope * dist.astype(F32), NEG)
        m_old = m_scr[:, 0:1]
        m_new = jnp.maximum(m_old, jnp.max(s, axis=-1, keepdims=True))
        alpha = jnp.exp(m_old - m_new)
        p = jnp.where(ok, jnp.exp(s - m_new), 0.0)
        l_scr[...] = jnp.broadcast_to(alpha * l_scr[:, 0:1] + jnp.sum(p, axis=-1, keepdims=True), l_scr.shape)
        acc_scr[...] = alpha * acc_scr[...] + jnp.dot(p.astype(BF16), v, preferred_element_type=F32)
        m_scr[...] = jnp.broadcast_to(m_new, m_scr.shape)

    sel = sel_ref[j]
    k = jnp.concatenate([ref[:, 0:gw] for ref in page_refs], axis=0).astype(BF16)
    v = jnp.concatenate([ref[:, gw:2 * gw] for ref in page_refs], axis=0).astype(BF16)
    picked = _nt_dot(sel, exp_ref[0:tk, :]) > 0.5
    kpos = j * tk + lax.broadcasted_iota(jnp.int32, (1, tk), 1)
    update(k, v, picked, (past + t_row) - kpos)

    @pl.when(j == pl.num_programs(1) - 1)
    def _():
        pad = new_scr.shape[0]
        new_scr[...] = jnp.concatenate([kvn_ref[...], jnp.zeros((pad - ts, 2 * gw), F32)], axis=0).astype(BF16)
        i_new = lax.broadcasted_iota(jnp.int32, (1, pad), 1)
        ok = (_nt_dot(sel, exp_ref[tk:tk + pad, :]) > 0.5) & (i_new <= t_row) & (i_new < ts)
        update(new_scr[:, 0:gw], new_scr[:, gw:2 * gw], ok, t_row - i_new)
        o = acc_scr[...] / jnp.maximum(l_scr[:, 0:1], 1e-30)
        n_rep = n_rows // (n_groups * ts)
        for r, x_sel in enumerate(_sample_rows_out(o, ts, n_groups)):
            g_cmp, g_sel, g_win = (1.0 / (1.0 + jnp.exp(-gate_ref[:, (i * n_rep + r) * gw:(i * n_rep + r + 1) * gw]))
                                   for i in range(3))
            rows = slice(r * ts, (r + 1) * ts)
            o_ref[:, r * gw:(r + 1) * gw] = g_cmp * xcmp_ref[rows, :] + g_sel * x_sel + g_win * xwin_ref[rows, :]


def _nsa_sample_sel(q, kv_sel_new, sel, x_cmp, x_win, gates, pool, page_table, n_groups, pages):
    b, ts, qw = q.shape
    n_rep = qw // HEAD_DIM // n_groups
    gw = n_groups * HEAD_DIM
    n_pages = page_table.shape[1]
    page_rows = pool.shape[1]
    past = n_pages * page_rows
    n_steps = n_pages // pages
    tk = pages * page_rows
    n_rows = n_rep * n_groups * ts
    assert sel.shape[1] == n_steps and tk % SEL_BLOCK == 0 and tk // SEL_BLOCK < LANES
    slopes = _sample_slopes(n_groups, n_rep, ts)
    blk_of = jnp.concatenate([jnp.arange(tk) // SEL_BLOCK, jnp.full((LANES,), tk // SEL_BLOCK)])
    expand = (blk_of[:, None] == jnp.arange(LANES)[None, :]).astype(BF16)
    per_b = lambda rows, w: pl.BlockSpec((None, rows, w), lambda i, j, pt: (i, 0, 0))
    page_spec = lambda k: pl.BlockSpec((None, page_rows, 2 * gw), lambda i, j, pt: (pt[i, j * pages + k], 0, 0))
    const = lambda shape: pl.BlockSpec(shape, lambda i, j, pt: (0,) * len(shape), pipeline_mode=pl.Buffered(1))
    return pl.pallas_call(
        functools.partial(_nsa_sample_sel_kernel, pages=pages, past=past),
        grid_spec=pltpu.PrefetchScalarGridSpec(
            num_scalar_prefetch=1,
            grid=(b, n_steps),
            in_specs=[const(slopes.shape), const(expand.shape), per_b(ts, qw), per_b(ts, 2 * gw),
                      pl.BlockSpec((None, n_steps, n_rows, LANES), lambda i, j, pt: (i, 0, 0, 0)),
                      per_b(n_rep * ts, gw), per_b(n_rep * ts, gw), per_b(ts, 3 * qw)]
                     + [page_spec(k) for k in range(pages)],
            out_specs=per_b(ts, qw),
            scratch_shapes=[pltpu.VMEM((n_rows, LANES), F32), pltpu.VMEM((n_rows, LANES), F32),
                            pltpu.VMEM((n_rows, gw), F32), pltpu.VMEM((LANES, 2 * gw), BF16)]),
        out_shape=jax.ShapeDtypeStruct((b, ts, qw), F32),
        compiler_params=_cparams(2),
        name="nsa_sample_sel",
    )(page_table, slopes, expand, q, kv_sel_new, sel, x_cmp, x_win, gates, *([pool] * pages))


def _masked_softmax(s, mask):
    s = jnp.where(mask, s, NEG)
    m = jnp.max(s, axis=-1, keepdims=True)
    p = jnp.where(mask, jnp.exp(s - m), 0.0)
    return p / jnp.maximum(jnp.sum(p, axis=-1, keepdims=True), 1e-30)


def _dil_attn_sample(q, ctx, n_heads):
    t = q.shape[1]
    slopes = _alibi_slopes(n_heads)
    pos = ctx.shape[1] - t + jnp.arange(t)
    lses, outs = [], []
    for window, dil in DIL_PAIRS:
        dist = jnp.arange(window // dil + 1) * dil
        idx = pos[:, None] - dist[None, :]
        kvg = jnp.take(ctx, jnp.maximum(idx, 0), axis=1)
        s = jnp.einsum('bthd,btkhd->bhtk', q, kvg[:, :, :, 0], preferred_element_type=F32) * HEAD_DIM ** -0.5
        s = s - slopes[:, None, None] * dist.astype(F32)
        s = jnp.where(idx >= 0, s, NEG)
        m = jnp.max(s, axis=-1, keepdims=True)
        p = jnp.exp(s - m)
        l = jnp.sum(p, axis=-1, keepdims=True)
        outs.append(jnp.einsum('bhtk,btkhd->bthd', p / l, kvg[:, :, :, 1]))
        lses.append(jnp.swapaxes((m + jnp.log(l))[..., 0], 1, 2))
    wts = jax.nn.softmax(jnp.stack(lses), axis=0)[..., None]
    return jnp.sum(wts * jnp.stack(outs), axis=0)


def _nsa_sample(q, kv_cmp, kv_sel, kv_win, gate_logits, past_cmp, past_sel, win_buf, w_cmp, pe_cmp):
    b, t, g, r, _ = q.shape
    scale = HEAD_DIM ** -0.5
    slopes = _alibi_slopes(g * r).reshape(g, r)
    gates = jax.nn.sigmoid(gate_logits)
    ctx_cmp = jnp.concatenate([past_cmp, kv_cmp], axis=1)
    ctx_sel = jnp.concatenate([past_sel, kv_sel], axis=1)
    ctx_win = jnp.concatenate([win_buf, kv_win], axis=1)
    tc = ctx_cmp.shape[1]
    qp = (tc - t) + jnp.arange(t)
    n_r = CMP_LEN // CMP_STRIDE
    n_ch = tc // CMP_STRIDE
    n_c = n_ch - n_r + 1
    chunks = ctx_cmp[:, :n_ch * CMP_STRIDE].reshape((b, n_ch, CMP_STRIDE) + ctx_cmp.shape[2:])
    w = w_cmp.reshape(n_r, CMP_STRIDE, 2, HEAD_DIM, HEAD_DIM)
    kvc = jnp.einsum('lcd,lcde->ce', pe_cmp, w_cmp)[None, None, :, None, :]
    for i in range(n_r):
        kvc = kvc + jnp.einsum('bnscgd,scde->bncge', chunks[:, i:i + n_c], w[i])
    cmp_end = jnp.arange(n_c) * CMP_STRIDE + CMP_LEN - 1
    kc, vc = kvc[:, :, 0], kvc[:, :, 1]
    n_sel = -(-tc // SEL_BLOCK)
    n_top = min(SEL_TOP, n_sel)
    sel_blocks = jnp.pad(ctx_sel, ((0, 0), (0, n_sel * SEL_BLOCK - tc), (0, 0), (0, 0), (0, 0)))
    sel_blocks = sel_blocks.reshape(b, n_sel, SEL_BLOCK, 2, g, HEAD_DIM).transpose(0, 4, 1, 2, 3, 5)
    cs = jnp.arange(n_c)[:, None] * CMP_STRIDE
    ss = jnp.arange(n_sel)[None, :] * SEL_BLOCK
    cmp2sel = jnp.maximum(jnp.minimum(cs + CMP_LEN, ss + SEL_BLOCK) - jnp.maximum(cs, ss), 0).astype(F32) / CMP_STRIDE
    blk = jnp.arange(n_sel)
    s = jnp.einsum('btgrd,bngd->bgrtn', q, kc, preferred_element_type=F32) * scale
    s = s - slopes[:, :, None, None] * (qp[:, None] - cmp_end[None, :]).astype(F32)
    p = _masked_softmax(s, cmp_end[None, :] <= qp[:, None])
    o_cmp = jnp.einsum('bgrtn,bngd->btgrd', p, vc)
    imp = jnp.einsum('bgrtn,nj->bgtj', p, cmp2sel)
    cb = qp // SEL_BLOCK
    blk_ok = blk[None, :] <= cb[:, None]
    forced = (blk[None, :] == 0) | (blk[None, :] == cb[:, None]) | (blk[None, :] == cb[:, None] - 1)
    score = jnp.where(blk_ok, imp + FORCE_BONUS * forced.astype(F32), NEG)
    top_s, top_i = lax.top_k(score, n_top)
    picked = jnp.sum((top_i[..., None] == blk) & (top_s > 0.5 * NEG)[..., None], axis=-2) > 0
    kpos = jnp.arange(tc)
    ok = picked[..., kpos // SEL_BLOCK] & (kpos <= qp[:, None])
    s2 = jnp.einsum('btgrd,bkgd->bgrtk', q, ctx_sel[:, :, 0], preferred_element_type=F32) * scale
    s2 = s2 - slopes[None, :, :, None, None] * (qp[:, None] - kpos).astype(F32)
    p2 = _masked_softmax(s2, ok[:, :, None])
    o_sel = jnp.einsum('bgrtk,bkgd->btgrd', p2, ctx_sel[:, :, 1])
    tw = ctx_win.shape[1]
    dw = (tw - t + jnp.arange(t))[:, None] - jnp.arange(tw)[None, :]
    okw = (dw >= 0) & (dw <= WIN)
    s3 = jnp.einsum('btgrd,bsgd->bgrts', q, ctx_win[:, :, 0], preferred_element_type=F32) * scale
    s3 = s3 - slopes[:, :, None, None] * dw.astype(F32)
    p3 = _masked_softmax(s3, okw)
    o_win = jnp.einsum('bgrts,bsgd->btgrd', p3, ctx_win[:, :, 1])
    o = gates[..., 0:1] * o_cmp + gates[..., 1:2] * o_sel + gates[..., 2:3] * o_win
    return o, ctx_win[:, t:]


def _gate_columns(w_gate, n_groups):
    d = w_gate.shape[0]
    per_pair = w_gate.reshape(d, n_groups // 2, 2 * 4 * 3)
    return jnp.pad(per_pair, ((0, 0), (0, 0), (0, LANES - 2 * 4 * 3))).reshape(d, -1)


def kernel(x_prompt, x_sample, cache_dil_kv, state_conv, state_rnn, cache_win_kv, cache_cmp_kv, cache_sel_kv, page_table, norm_mix, norm_ffn, norm_out, w_in_ab, w_out_ab, conv_w, conv_b, gate_a_w, gate_a_b, gate_x_w, gate_x_b, lru_lambda, w_in_c, w_out_c, w_cmp, pe_cmp, ffn_w1, ffn_w3, ffn_w2):
    bp, t, d = x_prompt.shape
    bs, ts, _ = x_sample.shape
    depth = norm_mix.shape[0]
    bf = lambda z: z.astype(BF16)
    tm_p, tm_s = 512, bs * ts
    yp = x_prompt.reshape(bp * t, d)
    ys = x_sample.reshape(bs * ts, d)
    outs = {k: [] for k in ("dil_p", "dil_s", "conv_p", "conv_s", "rnn_p", "rnn_s",
                            "win_p", "win_s", "cmp_p", "cmp_s", "sel_p", "sel_s")}
    for layer in range(depth):
        li = layer // 2
        last = layer == depth - 1
        ffn = (norm_ffn[layer], bf(ffn_w1[layer]), bf(ffn_w3[layer]), bf(ffn_w2[layer]), norm_out if last else None)
        if layer % 2 == 0:
            rw = conv_w.shape[2]
            aw = (w_in_ab.shape[2] - 2 * rw) // 3
            n_heads = aw // HEAD_DIM
            w_in = bf(w_in_ab[li])
            splits = (aw, 2 * aw, rw, rw)
            wa, wx = bf(_block_diag(gate_a_w[li])), bf(_block_diag(gate_x_w[li]))
            lru = (conv_w[li], conv_b[li], wa, gate_a_b[li], wx, gate_x_b[li], lru_lambda[li])
            w_outs = bf(w_out_ab[li])
            q, kv, xr, gate = _norm_proj(yp, norm_mix[layer], w_in, splits, tm_p)
            o_att = _dil_attn(q.reshape(bp, t, aw), kv.reshape(bp, t, 2 * aw), n_heads)
            o_rnn, conv_new, h_new = _rglru(xr.reshape(bp, t, rw), gate.reshape(bp, t, rw),
                                            jnp.zeros((bp, CONV_W - 1, rw), F32), jnp.zeros((bp, rw), F32),
                                            *lru, bp, 256)
            yp = _out_ffn(yp, [o_att.reshape(bp * t, aw), o_rnn.reshape(bp * t, rw)], w_outs, *ffn, tm_p)
            keep = min(DIL_MAX, t)
            outs["dil_p"].append(kv.reshape(bp, t, 2, n_heads, HEAD_DIM)[:, t - keep:])
            outs["conv_p"].append(conv_new)
            outs["rnn_p"].append(h_new)
            q, kv, xr, gate = _norm_proj(ys, norm_mix[layer], w_in, splits, tm_s)
            cache = cache_dil_kv[li]
            o_att, ctx_new = _dil_sample(q.reshape(bs, ts, aw), kv.reshape(bs, ts, 2 * aw),
                                         cache.reshape(bs, cache.shape[1], 2 * aw), n_heads)
            o_rnn, conv_new, h_new = _rglru(xr.reshape(bs, ts, rw), gate.reshape(bs, ts, rw),
                                            state_conv[li], state_rnn[li], *lru, 8, ts)
            ys = _out_ffn(ys, [o_att.reshape(bs * ts, aw), o_rnn.reshape(bs * ts, rw)], w_outs, *ffn, tm_s)
            outs["dil_s"].append(ctx_new.reshape(cache.shape))
            outs["conv_s"].append(conv_new)
            outs["rnn_s"].append(h_new)
        else:
            n_groups = cache_win_kv.shape[4]
            kvw = 2 * n_groups * HEAD_DIM
            qw = w_out_c.shape[1]
            rep = qw // HEAD_DIM // n_groups
            w_in = bf(jnp.concatenate([w_in_c[li][:, :qw + 3 * kvw],
                                       _gate_columns(w_in_c[li][:, qw + 3 * kvw:], n_groups)], axis=1))
            gw = n_groups // 2 * LANES
            splits = (qw, kvw, kvw, kvw, gw)
            wbd, pe = _compress_weights(w_cmp[li], pe_cmp[li], n_groups)
            w_outs = bf(w_out_c[li])
            kv5 = lambda z, n: z.reshape(n, -1, 2, n_groups, HEAD_DIM)
            q, kv_cmp, kv_sel, kv_win, gates = _norm_proj(yp, norm_mix[layer], w_in, splits, tm_p)
            o = _nsa_prompt(q.reshape(bp, t, qw), kv_cmp.reshape(bp, t, kvw), kv_sel.reshape(bp, t, kvw),
                            kv_win.reshape(bp, t, kvw), gates.reshape(bp, t, gw), wbd, pe, n_groups)
            yp = _out_ffn(yp, [o.reshape(bp * t, qw)], w_outs, *ffn, tm_p)
            keep = min(WIN, t)
            outs["win_p"].append(kv5(kv_win, bp)[:, t - keep:])
            outs["cmp_p"].append(kv5(kv_cmp, bp))
            outs["sel_p"].append(kv5(kv_sel, bp))
            wq = w_in_c[li][:, :qw].reshape(d, n_groups, rep, HEAD_DIM).transpose(0, 2, 1, 3).reshape(d, qw)
            wg = w_in_c[li][:, qw + 3 * kvw:].reshape(d, n_groups, rep, 3).transpose(0, 3, 2, 1)
            wg = jnp.broadcast_to(wg[..., None], wg.shape + (HEAD_DIM,)).reshape(d, 3 * qw)
            w_in_s = bf(jnp.concatenate([wq, w_in_c[li][:, qw:qw + 3 * kvw], wg], axis=1))
            w_out_s = bf(w_out_c[li].reshape(n_groups, rep, HEAD_DIM, d).transpose(1, 0, 2, 3).reshape(qw, d))
            q, kv_cmp, kv_sel, kv_win, gates = _norm_proj(ys, norm_mix[layer], w_in_s, (qw, kvw, kvw, kvw, 3 * qw), tm_s)
            n_phys, page_rows = cache_cmp_kv.shape[1], cache_cmp_kv.shape[2]
            ab_pool = _compress_pool(cache_cmp_kv[li].reshape(n_phys * page_rows, kvw), wbd, pe, 64 * page_rows)
            ab_pool = ab_pool.reshape(n_phys, page_rows // CMP_STRIDE, 2 * kvw)
            cwin = cache_win_kv[li]
            x_cmp, x_win, sel, win_new = _nsa_sample_cmp(q.reshape(bs, ts, qw), kv_win.reshape(bs, ts, kvw),
                                                         cwin.reshape(bs, cwin.shape[1], kvw), ab_pool, page_table,
                                                         n_groups, 8, 16)
            o = _nsa_sample_sel(q.reshape(bs, ts, qw), kv_sel.reshape(bs, ts, kvw), sel, x_cmp, x_win,
                                gates.reshape(bs, ts, 3 * qw), cache_sel_kv[li].reshape(n_phys, page_rows, kvw),
                                page_table, n_groups, 8)
            ys = _out_ffn(ys, [o.reshape(bs * ts, qw)], w_out_s, *ffn, tm_s)
            outs["win_s"].append(win_new.reshape(cwin.shape))
            outs["cmp_s"].append(kv5(kv_cmp, bs))
            outs["sel_s"].append(kv5(kv_sel, bs))
    st = lambda k: jnp.stack(outs[k])
    return (yp.reshape(bp, t, d), ys.reshape(bs, ts, d), st("dil_p"), st("dil_s"), st("conv_p"), st("conv_s"),
            st("rnn_p"), st("rnn_s"), st("win_p"), st("win_s"), st("cmp_p"), st("cmp_s"), st("sel_p"), st("sel_s"))
```

```python
import functools

import jax
import jax.numpy as jnp
from jax import lax
from jax.experimental import pallas as pl
from jax.experimental.pallas import tpu as pltpu

HEAD_DIM = 64
LANES = 128
DIL_PAIRS = ((128, 1), (512, 4), (2048, 16))
DIL_MAX = 2048
DIL_KEYS = 128
CONV_W = 4
LRU_C = 8.0
CMP_LEN = 32
CMP_STRIDE = 16
SEL_BLOCK = 64
SEL_TOP = 16
WIN = 512
NEG = -1e30
FORCE_BONUS = 1e3
EPS = 1e-6
VMEM_LIMIT = 56 * 1024 * 1024

F32 = jnp.float32
BF16 = jnp.bfloat16


def _cparams(n_grid):
    return pltpu.CompilerParams(dimension_semantics=("arbitrary",) * n_grid,
                                vmem_limit_bytes=VMEM_LIMIT)


def _resident(shape):
    return pl.BlockSpec(shape, lambda *_: (0,) * len(shape), pipeline_mode=pl.Buffered(1))


def _rms(x, g):
    return x * lax.rsqrt(jnp.mean(x * x, axis=-1, keepdims=True) + EPS) * g


def _alibi_slopes(n):
    return 2.0 ** (-8.0 * jnp.arange(1, n + 1, dtype=F32) / n)


def _norm_proj_kernel(x_ref, g_ref, w_ref, *out_refs):
    h = _rms(x_ref[...], g_ref[...]).astype(BF16)
    off = 0
    for o_ref in out_refs:
        n = o_ref.shape[-1]
        o_ref[...] = jnp.dot(h, w_ref[:, off:off + n], preferred_element_type=F32)
        off += n


def _norm_proj(x, g, w, splits, tm):
    n, d = x.shape
    assert n % tm == 0 and sum(splits) == w.shape[1]
    return pl.pallas_call(
        _norm_proj_kernel,
        grid=(n // tm,),
        in_specs=[pl.BlockSpec((tm, d), lambda i: (i, 0)),
                  _resident((1, d)),
                  _resident(w.shape)],
        out_specs=[pl.BlockSpec((tm, s), lambda i: (i, 0)) for s in splits],
        out_shape=[jax.ShapeDtypeStruct((n, s), F32) for s in splits],
        compiler_params=_cparams(1),
        name="norm_proj",
    )(x, g.reshape(1, d), w)


def _out_ffn_kernel(*refs, n_mix, n_chunks, final_norm):
    res_ref = refs[0]
    a_refs = refs[1:1 + n_mix]
    wo_ref, gf_ref, w1_ref, w3_ref, w2_ref = refs[1 + n_mix:6 + n_mix]
    rest = refs[6 + n_mix:]
    go_ref = rest[0] if final_norm else None
    o_ref = rest[-1]
    mix = jnp.concatenate([a_ref[...].astype(BF16) for a_ref in a_refs], axis=1)
    y = res_ref[...] + jnp.dot(mix, wo_ref[...], preferred_element_type=F32)
    h = _rms(y, gf_ref[...]).astype(BF16)
    ch = w1_ref.shape[1] // n_chunks
    for c in range(n_chunks):
        a = jnp.dot(h, w1_ref[:, c * ch:(c + 1) * ch], preferred_element_type=F32)
        b = jnp.dot(h, w3_ref[:, c * ch:(c + 1) * ch], preferred_element_type=F32)
        act = (a * (1.0 / (1.0 + jnp.exp(-a))) * b).astype(BF16)
        y = y + jnp.dot(act, w2_ref[c * ch:(c + 1) * ch, :], preferred_element_type=F32)
    if final_norm:
        y = _rms(y, go_ref[...])
    o_ref[...] = y


def _out_ffn(res, mixes, w_out, g_ffn, w1, w3, w2, g_out, tm):
    n, d = res.shape
    hidden = w1.shape[1]
    n_chunks = hidden // 256
    assert n % tm == 0 and hidden % 256 == 0
    final_norm = g_out is not None
    row = lambda width: pl.BlockSpec((tm, width), lambda i: (i, 0))
    in_specs = [row(d)] + [row(m.shape[1]) for m in mixes] + [_resident(w_out.shape)]
    in_specs += [_resident((1, d)), _resident(w1.shape), _resident(w3.shape), _resident(w2.shape)]
    args = [res, *mixes, w_out, g_ffn.reshape(1, d), w1, w3, w2]
    if final_norm:
        in_specs.append(_resident((1, d)))
        args.append(g_out.reshape(1, d))
    return pl.pallas_call(
        functools.partial(_out_ffn_kernel, n_mix=len(mixes), n_chunks=n_chunks, final_norm=final_norm),
        grid=(n // tm,),
        in_specs=in_specs,
        out_specs=row(d),
        out_shape=jax.ShapeDtypeStruct((n, d), F32),
        compiler_params=_cparams(1),
        name="out_ffn",
    )(*args)


def _dil_attn_kernel(slope_ref, q_ref, k_ref, v_ref, o_ref, *scratch, seq):
    ob_refs, lse_refs = scratch[:3], scratch[3:]
    hp = pl.program_id(1)
    lane_lo = lax.broadcasted_iota(jnp.int32, (1, LANES), 1) < HEAD_DIM
    slopes = [slope_ref[pl.ds(2 * hp + j, 1), 0:1] for j in range(2)]
    tq, tk = DIL_KEYS, 2 * DIL_KEYS
    iq = lax.broadcasted_iota(jnp.int32, (tq, tk), 0)
    ik = lax.broadcasted_iota(jnp.int32, (tq, tk), 1)

    for br, (_, dil) in enumerate(DIL_PAIRS):
        n_blk = seq // dil // tq

        def block(idx, carry, br=br, dil=dil, n_blk=n_blk):
            r = idx // n_blk
            blk = idx % n_blk
            a0 = blk * tq
            ka0 = jnp.maximum(blk - 1, 0) * tq
            q_rows = pl.ds(r + dil * a0, tq, stride=dil) if dil > 1 else pl.ds(a0, tq)
            k_rows = pl.ds(r + dil * ka0, tk, stride=dil) if dil > 1 else pl.ds(ka0, tk)
            q = q_ref[q_rows, :] * (HEAD_DIM ** -0.5)
            k = k_ref[k_rows, :].astype(BF16)
            v = v_ref[k_rows, :].astype(BF16)
            da = (a0 - ka0) + iq - ik
            valid = (da >= 0) & (da <= DIL_KEYS)
            dist = (dil * da).astype(F32)
            outs, lses = [], []
            for j in range(2):
                qj = jnp.where(lane_lo if j == 0 else ~lane_lo, q, 0.0).astype(BF16)
                s = lax.dot_general(qj, k, (((1,), (1,)), ((), ())), preferred_element_type=F32)
                s = jnp.where(valid, s - slopes[j] * dist, NEG)
                m = jnp.max(s, axis=-1, keepdims=True)
                p = jnp.exp(s - m)
                l = jnp.sum(p, axis=-1, keepdims=True)
                outs.append(jnp.dot(p.astype(BF16), v, preferred_element_type=F32) / l)
                lses.append(m + jnp.log(l))
            ob_refs[br][q_rows, :] = jnp.where(lane_lo, outs[0], outs[1])
            lse_refs[br][q_rows, :] = jnp.where(lane_lo, lses[0], lses[1])
            return carry

        lax.fori_loop(0, dil * n_blk, block, 0)

    rows = 512
    def merge(i, carry):
        sl = pl.ds(i * rows, rows)
        ls = [lse_refs[b][sl, :] for b in range(3)]
        m = jnp.maximum(jnp.maximum(ls[0], ls[1]), ls[2])
        es = [jnp.exp(x - m) for x in ls]
        num = es[0] * ob_refs[0][sl, :] + es[1] * ob_refs[1][sl, :] + es[2] * ob_refs[2][sl, :]
        o_ref[sl, :] = num / (es[0] + es[1] + es[2])
        return carry
    lax.fori_loop(0, seq // rows, merge, 0)


def _dil_attn(q, kv, n_heads):
    b, t, width = q.shape
    n_hp = width // LANES
    assert t % (16 * 2 * DIL_KEYS) == 0
    slopes = jnp.broadcast_to(_alibi_slopes(n_heads)[:, None], (n_heads, LANES))
    blk = lambda off: pl.BlockSpec((None, t, LANES), lambda i, j: (i, 0, off + j))
    return pl.pallas_call(
        functools.partial(_dil_attn_kernel, seq=t),
        grid=(b, n_hp),
        in_specs=[_resident((n_heads, LANES)), blk(0), blk(0), blk(n_hp)],
        out_specs=blk(0),
        out_shape=jax.ShapeDtypeStruct((b, t, width), F32),
        scratch_shapes=[pltpu.VMEM((t, LANES), F32)] * 6,
        compiler_params=_cparams(2),
        name="dil_attn",
    )(slopes, q, kv, kv)


def _rglru_kernel(xr_ref, gate_ref, cprev_ref, hprev_ref, cw_ref, cb_ref, wa_ref, ba_ref,
                  wx_ref, bx_ref, lam_ref, y_ref, cnew_ref, hnew_ref,
                  xin_ref, a_ref, u_ref, h_ref):
    ti = pl.program_id(1)
    bb, tc, width = xr_ref.shape
    pad = 8
    tail = CONV_W - 1

    @pl.when(ti == 0)
    def _():
        xin_ref[:, pad - tail:pad, :] = cprev_ref[...]
        h_ref[...] = hprev_ref[...]

    xin_ref[:, pad:pad + tc, :] = xr_ref[...]
    lam = lam_ref[...]
    neg_softplus_c = -LRU_C * (jnp.maximum(-lam, 0.0) + jnp.log1p(jnp.exp(-jnp.abs(lam))))
    for b in range(bb):
        xc = cb_ref[...] + sum(xin_ref[b, pad - tail + k:pad - tail + k + tc, :] * cw_ref[k:k + 1, :]
                               for k in range(CONV_W))
        xcb = xc.astype(BF16)
        ra = jnp.dot(xcb, wa_ref[...], preferred_element_type=F32) + ba_ref[...]
        rx = jnp.dot(xcb, wx_ref[...], preferred_element_type=F32) + bx_ref[...]
        rg = 1.0 / (1.0 + jnp.exp(-ra))
        ig = 1.0 / (1.0 + jnp.exp(-rx))
        log_a = neg_softplus_c * rg
        a = jnp.exp(log_a)
        a_ref[b] = a
        u_ref[b] = jnp.sqrt(jnp.tanh(-log_a) * (1.0 + a * a)) * ig * xc

    def step(t, hs):
        new = []
        for b in range(bb):
            hb = a_ref[b, pl.ds(t, 1), :] * hs[b] + u_ref[b, pl.ds(t, 1), :]
            u_ref[b, pl.ds(t, 1), :] = hb
            new.append(hb)
        return tuple(new)

    hs = lax.fori_loop(0, tc, step, tuple(h_ref[b:b + 1, :] for b in range(bb)), unroll=8)
    for b in range(bb):
        h_ref[b:b + 1, :] = hs[b]
    g = gate_ref[...]
    cdf = 0.5 * (1.0 + jnp.tanh(0.7978845608028654 * (g + 0.044715 * (g * g * g))))
    y_ref[...] = u_ref[...] * (g * cdf)
    xin_ref[:, pad - tail:pad, :] = xin_ref[:, pad + tc - tail:pad + tc, :]
    cnew_ref[...] = xin_ref[:, pad - tail:pad, :]
    hnew_ref[...] = h_ref[...]


def _rglru(xr, gate, conv_prev, h_prev, conv_w, conv_b, wa_bd, ba, wx_bd, bx, lam, bb, tc):
    b, t, width = xr.shape
    assert b % bb == 0 and t % tc == 0 and tc >= CONV_W - 1
    seq_blk = pl.BlockSpec((bb, tc, width), lambda i, j: (i, j, 0))
    vec = _resident((1, width))
    return pl.pallas_call(
        _rglru_kernel,
        grid=(b // bb, t // tc),
        in_specs=[seq_blk, seq_blk,
                  pl.BlockSpec((bb, CONV_W - 1, width), lambda i, j: (i, 0, 0)),
                  pl.BlockSpec((bb, width), lambda i, j: (i, 0)),
                  _resident((CONV_W, width)), vec, _resident(wa_bd.shape), vec,
                  _resident(wx_bd.shape), vec, vec],
        out_specs=[seq_blk,
                   pl.BlockSpec((bb, CONV_W - 1, width), lambda i, j: (i, 0, 0)),
                   pl.BlockSpec((bb, width), lambda i, j: (i, 0))],
        out_shape=[jax.ShapeDtypeStruct((b, t, width), F32),
                   jax.ShapeDtypeStruct((b, CONV_W - 1, width), F32),
                   jax.ShapeDtypeStruct((b, width), F32)],
        scratch_shapes=[pltpu.VMEM((bb, tc + 8, width), F32), pltpu.VMEM((bb, tc, width), F32),
                        pltpu.VMEM((bb, tc, width), F32), pltpu.VMEM((bb, width), F32)],
        compiler_params=_cparams(2),
        name="rglru",
    )(xr, gate, conv_prev, h_prev, conv_w, conv_b.reshape(1, width), wa_bd, ba.reshape(1, width),
      wx_bd, bx.reshape(1, width), lam.reshape(1, width))


def _block_diag(w):
    n, bi, bj = w.shape
    eye = jnp.eye(n, dtype=w.dtype)
    return (eye[:, None, :, None] * w[:, :, None, :]).reshape(n * bi, n * bj)


def _lane_lo():
    return lax.broadcasted_iota(jnp.int32, (1, LANES), 1) < HEAD_DIM


def _spread_kv(k2, v2, half):
    lo = _lane_lo()
    own = lo if half == 0 else ~lo
    k_own = jnp.where(own, k2, 0.0)
    v_own = jnp.where(own, v2, 0.0)
    k_other = pltpu.roll(k_own, HEAD_DIM, axis=1)
    v_both = v_own + pltpu.roll(v_own, HEAD_DIM, axis=1)
    k_lo, k_hi = (k_own, k_other) if half == 0 else (k_other, k_own)
    return k_lo.astype(BF16), k_hi.astype(BF16), v_both.astype(BF16)


def _group_heads(q_ref, gl):
    base = gl * 4 * HEAD_DIM
    qa = (q_ref[:, base:base + LANES] * (HEAD_DIM ** -0.5)).astype(BF16)
    qb = (q_ref[:, base + LANES:base + 2 * LANES] * (HEAD_DIM ** -0.5)).astype(BF16)
    return ((qa, 0), (qa, 1), (qb, 0), (qb, 1))


def _nt_dot(a, b):
    return lax.dot_general(a, b, (((1,), (1,)), ((), ())), preferred_element_type=F32)


def _compress_kernel(*refs, n_ch, merged):
    x_refs, (w_ref, pe_ref, o_ref) = refs[:-3], refs[-3:]
    per_half = len(x_refs) // 2
    half = per_half * LANES
    for c in range(2):
        acc = [None, None]
        pe_acc = None
        for s in range(CMP_STRIDE):
            xs = jnp.concatenate([x_ref[pl.ds(s, n_ch, stride=CMP_STRIDE), :]
                                  for x_ref in x_refs[c * per_half:(c + 1) * per_half]], axis=1).astype(BF16)
            for i in range(2):
                w = w_ref[(i * CMP_STRIDE + s) * 2 + c]
                d = jnp.dot(xs, w, preferred_element_type=F32)
                acc[i] = d if acc[i] is None else acc[i] + d
                pe_row = jnp.broadcast_to(pe_ref[i * CMP_STRIDE + s:i * CMP_STRIDE + s + 1, c * half:(c + 1) * half],
                                          (8, half)).astype(BF16)
                dp = jnp.dot(pe_row, w, preferred_element_type=F32)
                pe_acc = dp if pe_acc is None else pe_acc + dp
        if merged:
            o_ref[:, c * half:(c + 1) * half] = acc[0] + pltpu.roll(acc[1], n_ch - 1, axis=0) + pe_acc[0:1, :]
        else:
            o_ref[:, c * half:(c + 1) * half] = acc[0] + pe_acc[0:1, :]
            o_ref[:, (2 + c) * half:(3 + c) * half] = acc[1]


def _compress_weights(w_cmp, pe_cmp, n_groups):
    n_r = CMP_LEN // CMP_STRIDE
    w = w_cmp.reshape(n_r * CMP_STRIDE * 2, HEAD_DIM, HEAD_DIM)
    eye = jnp.eye(n_groups, dtype=w.dtype)
    wbd = (eye[None, :, None, :, None] * w[:, None, :, None, :]).reshape(-1, n_groups * HEAD_DIM, n_groups * HEAD_DIM)
    pe = jnp.broadcast_to(pe_cmp[:, :, None, :], (CMP_LEN, 2, n_groups, HEAD_DIM)).reshape(CMP_LEN, -1)
    return wbd.astype(BF16), pe


def _compress(ctx, wbd, pe):
    b, tc, width = ctx.shape
    n_ch = tc // CMP_STRIDE
    return pl.pallas_call(
        functools.partial(_compress_kernel, n_ch=n_ch, merged=True),
        grid=(b,),
        in_specs=[pl.BlockSpec((None, tc, LANES), functools.partial(lambda j, i: (i, 0, j), j))
                  for j in range(width // LANES)] + [_resident(wbd.shape), _resident(pe.shape)],
        out_specs=pl.BlockSpec((None, n_ch, width), lambda i: (i, 0, 0)),
        out_shape=jax.ShapeDtypeStruct((b, n_ch, width), F32),
        compiler_params=_cparams(1),
        name="nsa_compress",
    )(*([ctx] * (width // LANES)), wbd, pe)


def _cmp_select_kernel(slope_ref, c2s_ref, q_ref, kc_ref, vc_ref, o_ref, sel_ref, *, n_c, n_sel):
    gp, qi = pl.program_id(1), pl.program_id(2)
    tq = q_ref.shape[0]
    n_ch = kc_ref.shape[0]
    lo = _lane_lo()
    qp = qi * tq + lax.broadcasted_iota(jnp.int32, (tq, 1), 0)
    n_idx = lax.broadcasted_iota(jnp.int32, (1, n_ch), 1)
    cmp_end = n_idx * CMP_STRIDE + (CMP_LEN - 1)
    valid = (cmp_end <= qp) & (n_idx < n_c)
    dist = (qp - cmp_end).astype(F32)
    blk = lax.broadcasted_iota(jnp.int32, (1, LANES), 1)
    cb = qp // SEL_BLOCK
    blk_ok = (blk <= cb) & (blk < n_sel)
    forced = (blk == 0) | (blk == cb) | (blk == cb - 1)
    blk_t = lax.broadcasted_iota(jnp.int32, (n_sel, tq), 0)
    for gl in range(2):
        keys = _spread_kv(kc_ref[...], vc_ref[...], gl)
        imp = jnp.zeros((tq, LANES), F32)
        outs = []
        for r, (qh, variant) in enumerate(_group_heads(q_ref, gl)):
            slope = slope_ref[pl.ds((2 * gp + gl) * 4 + r, 1), 0:1]
            s = jnp.where(valid, _nt_dot(qh, keys[variant]) - slope * dist, NEG)
            m = jnp.max(s, axis=-1, keepdims=True)
            p = jnp.where(valid, jnp.exp(s - m), 0.0)
            p = (p / jnp.maximum(jnp.sum(p, axis=-1, keepdims=True), 1e-30)).astype(BF16)
            outs.append(jnp.dot(p, keys[2], preferred_element_type=F32))
            imp = imp + jnp.dot(p, c2s_ref[...], preferred_element_type=F32)
        base = gl * 4 * HEAD_DIM
        o_ref[:, base:base + LANES] = jnp.where(lo, outs[0], outs[1])
        o_ref[:, base + LANES:base + 2 * LANES] = jnp.where(lo, outs[2], outs[3])
        score = jnp.where(blk_ok, imp + FORCE_BONUS * forced.astype(F32), NEG)
        score_t = score.T[0:n_sel, :]
        rank = jnp.zeros((n_sel, tq), F32)
        for k in range(n_sel):
            row = score_t[k:k + 1, :]
            ahead = (row > score_t) | ((row == score_t) & (k < blk_t))
            rank = rank + ahead.astype(F32)
        chosen = ((rank < SEL_TOP) & (score_t > 0.5 * NEG)).astype(F32)
        chosen = jnp.concatenate([chosen, jnp.zeros((LANES - n_sel, tq), F32)], axis=0) if n_sel < LANES else chosen
        sel_ref[gl] = chosen.T.astype(BF16)


def _cmp_select(q, kvc, n_c, n_sel, n_groups, tq):
    b, t, width = q.shape
    n_ch = kvc.shape[1]
    n_gp = n_groups // 2
    n_heads = width // HEAD_DIM
    assert tq == LANES and n_sel <= LANES and n_sel % 8 == 0 and t % tq == 0
    slopes = jnp.broadcast_to(_alibi_slopes(n_heads)[:, None], (n_heads, LANES))
    cs = jnp.arange(n_ch)[:, None] * CMP_STRIDE
    ss = jnp.arange(LANES)[None, :] * SEL_BLOCK
    ov = jnp.maximum(jnp.minimum(cs + CMP_LEN, ss + SEL_BLOCK) - jnp.maximum(cs, ss), 0).astype(F32) / CMP_STRIDE
    c2s = jnp.where((jnp.arange(n_ch)[:, None] < n_c) & (jnp.arange(LANES)[None, :] < n_sel), ov, 0.0).astype(BF16)
    return pl.pallas_call(
        functools.partial(_cmp_select_kernel, n_c=n_c, n_sel=n_sel),
        grid=(b, n_gp, t // tq),
        in_specs=[_resident(slopes.shape), _resident(c2s.shape),
                  pl.BlockSpec((None, tq, 2 * 4 * HEAD_DIM), lambda i, j, k: (i, k, j)),
                  pl.BlockSpec((None, n_ch, LANES), lambda i, j, k: (i, 0, j)),
                  pl.BlockSpec((None, n_ch, LANES), lambda i, j, k: (i, 0, n_gp + j))],
        out_specs=[pl.BlockSpec((None, tq, 2 * 4 * HEAD_DIM), lambda i, j, k: (i, k, j)),
                   pl.BlockSpec((None, 2, tq, LANES), lambda i, j, k: (i, j, k, 0))],
        out_shape=[jax.ShapeDtypeStruct((b, t, width), F32),
                   jax.ShapeDtypeStruct((b, n_groups, t, LANES), BF16)],
        compiler_params=_cparams(3),
        name="nsa_cmp_select",
    )(slopes, c2s, q, kvc, kvc)


def _stage_kv(k_ref, v_ref, kv_scr):
    for gl in range(2):
        k_lo, k_hi, v_both = _spread_kv(k_ref[...], v_ref[...], gl)
        kv_scr[3 * gl + 0][...] = k_lo
        kv_scr[3 * gl + 1][...] = k_hi
        kv_scr[3 * gl + 2][...] = v_both


def _sel_attn_kernel(slope_ref, exp_ref, q_ref, sel_ref, k_ref, v_ref, o_ref, *kv_scr, tk):
    gp, qi = pl.program_id(1), pl.program_id(2)
    tq = q_ref.shape[0]
    lo = _lane_lo()

    @pl.when(qi == 0)
    def _():
        _stage_kv(k_ref, v_ref, kv_scr)

    qp = qi * tq + lax.broadcasted_iota(jnp.int32, (tq, 1), 0)
    n_kt = (qi * tq + tq - 1) // tk + 1
    for gl in range(2):
        heads = _group_heads(q_ref, gl)
        slopes = [slope_ref[pl.ds((2 * gp + gl) * 4 + r, 1), 0:1] for r in range(4)]
        sel = sel_ref[gl]
        k_scr = (kv_scr[3 * gl], kv_scr[3 * gl + 1])
        v_scr = kv_scr[3 * gl + 2]

        def tile(kt, carry, heads=heads, slopes=slopes, sel=sel, k_scr=k_scr, v_scr=v_scr):
            rows = pl.ds(pl.multiple_of(kt * tk, tk), tk)
            kpos = kt * tk + lax.broadcasted_iota(jnp.int32, (1, tk), 1)
            picked = _nt_dot(sel, exp_ref[rows, :]) > 0.5
            ok = picked & (kpos <= qp)
            dist = (qp - kpos).astype(F32)
            v = v_scr[rows, :]
            new = []
            for r, (qh, variant) in enumerate(heads):
                m_old, l_old, acc_old = carry[r]
                s = jnp.where(ok, _nt_dot(qh, k_scr[variant][rows, :]) - slopes[r] * dist, NEG)
                m_new = jnp.maximum(m_old, jnp.max(s, axis=-1, keepdims=True))
                alpha = jnp.exp(m_old - m_new)
                p = jnp.where(ok, jnp.exp(s - m_new), 0.0)
                l_new = alpha * l_old + jnp.sum(p, axis=-1, keepdims=True)
                acc_new = alpha * acc_old + jnp.dot(p.astype(BF16), v, preferred_element_type=F32)
                new.append((m_new, l_new, acc_new))
            return tuple(new)

        init = tuple((jnp.full((tq, 1), NEG, F32), jnp.zeros((tq, 1), F32), jnp.zeros((tq, LANES), F32))
                     for _ in range(4))
        res = lax.fori_loop(0, n_kt, tile, init)
        outs = [acc / jnp.maximum(l, 1e-30) for (_, l, acc) in res]
        base = gl * 4 * HEAD_DIM
        o_ref[:, base:base + LANES] = jnp.where(lo, outs[0], outs[1])
        o_ref[:, base + LANES:base + 2 * LANES] = jnp.where(lo, outs[2], outs[3])


def _sel_attn(q, kv, sel, n_groups, tq, tk):
    b, t, width = q.shape
    n_gp = n_groups // 2
    n_heads = width // HEAD_DIM
    assert t % tq == 0 and t % tk == 0 and tk % SEL_BLOCK == 0
    slopes = jnp.broadcast_to(_alibi_slopes(n_heads)[:, None], (n_heads, LANES))
    expand = (jnp.arange(t)[:, None] // SEL_BLOCK == jnp.arange(LANES)[None, :]).astype(BF16)
    return pl.pallas_call(
        functools.partial(_sel_attn_kernel, tk=tk),
        grid=(b, n_gp, t // tq),
        in_specs=[_resident(slopes.shape), _resident(expand.shape),
                  pl.BlockSpec((None, tq, 2 * 4 * HEAD_DIM), lambda i, j, k: (i, k, j)),
                  pl.BlockSpec((None, 2, tq, LANES), lambda i, j, k: (i, j, k, 0)),
                  pl.BlockSpec((None, t, LANES), lambda i, j, k: (i, 0, j)),
                  pl.BlockSpec((None, t, LANES), lambda i, j, k: (i, 0, n_gp + j))],
        out_specs=pl.BlockSpec((None, tq, 2 * 4 * HEAD_DIM), lambda i, j, k: (i, k, j)),
        out_shape=jax.ShapeDtypeStruct((b, t, width), F32),
        scratch_shapes=[pltpu.VMEM((t, LANES), BF16)] * 6,
        compiler_params=_cparams(3),
        name="nsa_sel_attn",
    )(slopes, expand, q, sel, kv, kv)


ALIBI_LANES = 6
MASK_BIG = 2.0 ** 100


def _alibi_tables(n_heads, t):
    slopes = _alibi_slopes(n_heads)
    s1 = slopes.astype(BF16).astype(F32)
    s2 = (slopes - s1).astype(BF16).astype(F32)
    s3 = (slopes - s1 - s2).astype(BF16).astype(F32)
    qtab = jnp.zeros((n_heads, LANES), F32).at[:, HEAD_DIM:HEAD_DIM + ALIBI_LANES].set(
        jnp.stack([s1, s2, s3, s1, s2, s3], axis=-1))
    pos = jnp.arange(t)
    hi = (pos // SEL_BLOCK * SEL_BLOCK).astype(F32)
    lo = (pos % SEL_BLOCK).astype(F32)
    ktab = jnp.zeros((t, LANES), F32).at[:, HEAD_DIM:HEAD_DIM + ALIBI_LANES].set(
        jnp.stack([hi, hi, hi, lo, lo, lo], axis=-1))
    return qtab, ktab


def _stack_heads(q_ref, gl, qtab_ref, first_head, extra):
    lo = _lane_lo()
    parts = []
    for r in range(4):
        c0 = gl * 4 * HEAD_DIM + (r // 2) * LANES
        slab = q_ref[:, c0:c0 + LANES] * (HEAD_DIM ** -0.5)
        if r % 2:
            slab = pltpu.roll(slab, HEAD_DIM, axis=1)
        qa = jnp.where(lo, slab, qtab_ref[pl.ds(first_head + r, 1), :]).astype(BF16)
        parts.append(qa if extra is None else jnp.concatenate([qa, extra], axis=1))
    return jnp.concatenate(parts, axis=0)


def _stage_kv_aug(k_ref, v_ref, ktab_ref, gl):
    lo = _lane_lo()
    own = lo if gl == 0 else ~lo
    k_own = jnp.where(own, k_ref[...], 0.0)
    v_own = jnp.where(own, v_ref[...], 0.0)
    k_low = k_own if gl == 0 else pltpu.roll(k_own, HEAD_DIM, axis=1)
    return (k_low + ktab_ref[...]).astype(BF16), (v_own + pltpu.roll(v_own, HEAD_DIM, axis=1)).astype(BF16)


def _unstack_heads(o_ref, gl, out, tq):
    lo = _lane_lo()
    base = gl * 4 * HEAD_DIM
    o_ref[:, base:base + LANES] = jnp.where(lo, out[0:tq], out[tq:2 * tq])
    o_ref[:, base + LANES:base + 2 * LANES] = jnp.where(lo, out[2 * tq:3 * tq], out[3 * tq:4 * tq])


def _sel_attn2_kernel(qtab_ref, ktab_ref, q_ref, sel_ref, k_ref, v_ref, o_ref, ka0, ka1, v0, v1, qv_ref, *, tk):
    gp, qi = pl.program_id(1), pl.program_id(2)
    tq = q_ref.shape[0]
    staged = ((ka0, v0), (ka1, v1))
    n_var = qv_ref.shape[0]
    blocks_per_tile = tk // SEL_BLOCK
    mask_lane0 = HEAD_DIM + ALIBI_LANES

    @pl.when(qi == 0)
    def _():
        for gl, (ka, vs) in enumerate(staged):
            ka[...], vs[...] = _stage_kv_aug(k_ref, v_ref, ktab_ref, gl)

    q0 = qi * tq
    n_full = q0 // tk
    qp = q0 + lax.broadcasted_iota(jnp.int32, (4 * tq, 1), 0) % tq
    lane = lax.broadcasted_iota(jnp.int32, (1, LANES), 1)
    mask_lanes = (lane >= mask_lane0) & (lane < mask_lane0 + blocks_per_tile)
    for gl, (ka, vs) in enumerate(staged):
        q_base = _stack_heads(q_ref, gl, qtab_ref, (2 * gp + gl) * 4, None)
        not_sel = sel_ref[gl].astype(F32) - 1.0
        for var in range(n_var):
            shift = (mask_lane0 - var * blocks_per_tile) % LANES
            flags = jnp.where(mask_lanes, pltpu.roll(not_sel, shift, axis=1) if shift else not_sel, 0.0).astype(BF16)
            qv_ref[var] = q_base + jnp.concatenate([flags] * 4, axis=0)

        def tile(kt, carry, masked, ka=ka, vs=vs):
            m_old, l_old, acc_old = carry
            rows = pl.ds(pl.multiple_of(kt * tk, tk), tk)
            s = _nt_dot(qv_ref[kt], ka[rows, :])
            if masked:
                kpos = kt * tk + lax.broadcasted_iota(jnp.int32, (1, tk), 1)
                s = jnp.where(kpos <= qp, s, NEG)
            m_new = jnp.maximum(m_old, jnp.max(s, axis=-1, keepdims=True))
            alpha = jnp.exp(m_old - m_new)
            p = jnp.exp(s - m_new)
            l_new = alpha * l_old + jnp.sum(p, axis=-1, keepdims=True)
            acc_new = alpha * acc_old + jnp.dot(p.astype(BF16), vs[rows, :], preferred_element_type=F32)
            return m_new, l_new, acc_new

        init = (jnp.full((4 * tq, 1), NEG, F32), jnp.zeros((4 * tq, 1), F32), jnp.zeros((4 * tq, LANES), F32))
        carry = lax.fori_loop(0, n_full, functools.partial(tile, masked=False), init)
        _, l, acc = tile(n_full, carry, True)
        _unstack_heads(o_ref, gl, acc / jnp.maximum(l, 1e-30), tq)


def _sel_attn2(q, kv, sel, n_groups, tq, tk):
    b, t, width = q.shape
    n_gp = n_groups // 2
    n_heads = width // HEAD_DIM
    blocks_per_tile = tk // SEL_BLOCK
    assert t % tk == 0 and tk % tq == 0 and HEAD_DIM + ALIBI_LANES + blocks_per_tile <= LANES
    qtab, ktab = _alibi_tables(n_heads, t)
    blk_lane = HEAD_DIM + ALIBI_LANES + (jnp.arange(t) // SEL_BLOCK) % blocks_per_tile
    ktab = ktab + jnp.where(blk_lane[:, None] == jnp.arange(LANES)[None, :], MASK_BIG, 0.0)
    return pl.pallas_call(
        functools.partial(_sel_attn2_kernel, tk=tk),
        grid=(b, n_gp, t // tq),
        in_specs=[_resident(qtab.shape), _resident(ktab.shape),
                  pl.BlockSpec((None, tq, 2 * 4 * HEAD_DIM), lambda i, j, k: (i, k, j)),
                  pl.BlockSpec((None, 2, tq, LANES), lambda i, j, k: (i, j, k, 0)),
                  pl.BlockSpec((None, t, LANES), lambda i, j, k: (i, 0, j)),
                  pl.BlockSpec((None, t, LANES), lambda i, j, k: (i, 0, n_gp + j))],
        out_specs=pl.BlockSpec((None, tq, 2 * 4 * HEAD_DIM), lambda i, j, k: (i, k, j)),
        out_shape=jax.ShapeDtypeStruct((b, t, width), F32),
        scratch_shapes=[pltpu.VMEM((t, LANES), BF16)] * 4 + [pltpu.VMEM((t // tk, 4 * tq, LANES), BF16)],
        compiler_params=_cparams(3),
        name="nsa_sel_attn",
    )(qtab, ktab, q, sel, kv, kv)


def _win_combine_kernel(slope_ref, q_ref, gate_ref, ocmp_ref, osel_ref, k_ref, v_ref, o_ref, *kv_scr, seq):
    gp, qi = pl.program_id(1), pl.program_id(2)
    tq = q_ref.shape[0]
    tk = WIN + tq
    lo = _lane_lo()

    @pl.when(qi == 0)
    def _():
        _stage_kv(k_ref, v_ref, kv_scr)

    k0 = jnp.clip(qi * tq - WIN, 0, seq - tk)
    rows = pl.ds(pl.multiple_of(k0, tq), tk)
    qp = qi * tq + lax.broadcasted_iota(jnp.int32, (tq, 1), 0)
    kpos = k0 + lax.broadcasted_iota(jnp.int32, (1, tk), 1)
    dist_i = qp - kpos
    ok = (dist_i >= 0) & (dist_i <= WIN)
    dist = dist_i.astype(F32)
    gates = 1.0 / (1.0 + jnp.exp(-gate_ref[...]))
    for gl in range(2):
        outs = []
        v = kv_scr[3 * gl + 2][rows, :]
        for r, (qh, variant) in enumerate(_group_heads(q_ref, gl)):
            slope = slope_ref[pl.ds((2 * gp + gl) * 4 + r, 1), 0:1]
            s = jnp.where(ok, _nt_dot(qh, kv_scr[3 * gl + variant][rows, :]) - slope * dist, NEG)
            m = jnp.max(s, axis=-1, keepdims=True)
            p = jnp.where(ok, jnp.exp(s - m), 0.0)
            l = jnp.sum(p, axis=-1, keepdims=True)
            outs.append(jnp.dot(p.astype(BF16), v, preferred_element_type=F32) / jnp.maximum(l, 1e-30))
        for half in range(2):
            cols = slice(gl * 4 * HEAD_DIM + half * LANES, gl * 4 * HEAD_DIM + (half + 1) * LANES)
            o_win = jnp.where(lo, outs[2 * half], outs[2 * half + 1])
            c0 = (gl * 4 + 2 * half) * 3
            g_cmp, g_sel, g_win = (jnp.where(lo, gates[:, c0 + i:c0 + i + 1], gates[:, c0 + 3 + i:c0 + 4 + i])
                                   for i in range(3))
            o_ref[:, cols] = g_cmp * ocmp_ref[:, cols] + g_sel * osel_ref[:, cols] + g_win * o_win


def _win_combine(q, kv, gates, o_cmp, o_sel, n_groups, tq):
    b, t, width = q.shape
    n_gp = n_groups // 2
    n_heads = width // HEAD_DIM
    assert t % tq == 0 and t >= WIN + tq
    slopes = jnp.broadcast_to(_alibi_slopes(n_heads)[:, None], (n_heads, LANES))
    qblk = pl.BlockSpec((None, tq, 2 * 4 * HEAD_DIM), lambda i, j, k: (i, k, j))
    return pl.pallas_call(
        functools.partial(_win_combine_kernel, seq=t),
        grid=(b, n_gp, t // tq),
        in_specs=[_resident(slopes.shape), qblk,
                  pl.BlockSpec((None, tq, LANES), lambda i, j, k: (i, k, j)),
                  qblk, qblk,
                  pl.BlockSpec((None, t, LANES), lambda i, j, k: (i, 0, j)),
                  pl.BlockSpec((None, t, LANES), lambda i, j, k: (i, 0, n_gp + j))],
        out_specs=qblk,
        out_shape=jax.ShapeDtypeStruct((b, t, width), F32),
        scratch_shapes=[pltpu.VMEM((t, LANES), BF16)] * 6,
        compiler_params=_cparams(3),
        name="nsa_win_combine",
    )(slopes, q, gates, o_cmp, o_sel, kv, kv)


def _nsa_prompt(q, kv_cmp, kv_sel, kv_win, gates, wbd, pe, n_groups):
    t = q.shape[1]
    kvc = _compress(kv_cmp, wbd, pe)
    n_c = t // CMP_STRIDE - CMP_LEN // CMP_STRIDE + 1
    n_sel = -(-t // SEL_BLOCK)
    o_cmp, sel = _cmp_select(q, kvc, n_c, n_sel, n_groups, LANES)
    o_sel = _sel_attn2(q, kv_sel, sel, n_groups, LANES, 8 * LANES)
    return _win_combine(q, kv_win, gates, o_cmp, o_sel, n_groups, LANES)


def _head_block_diag(x, n_rep, lanes_per_head, rows_per_head):
    tiled = jnp.concatenate([x] * n_rep, axis=0)
    row_h = lax.broadcasted_iota(jnp.int32, tiled.shape, 0) // rows_per_head
    lane_h = lax.broadcasted_iota(jnp.int32, tiled.shape, 1) // lanes_per_head
    return jnp.where(row_h == lane_h, tiled, 0.0)


def _head_diag_rows(o, n_rep, lanes_per_head, rows_per_head):
    row_h = lax.broadcasted_iota(jnp.int32, o.shape, 0) // rows_per_head
    lane_h = lax.broadcasted_iota(jnp.int32, o.shape, 1) // lanes_per_head
    kept = jnp.where(row_h == lane_h, o, 0.0)
    return jnp.sum(kept.reshape(n_rep, rows_per_head, o.shape[1]), axis=0)


def _dil_sample_kernel(slope_ref, q_ref, kvn_ref, cache_ref, o_ref, ctx_ref, k_scr, v_scr):
    ts, width = q_ref.shape
    past = cache_ref.shape[0]
    n_heads = width // HEAD_DIM
    rows = n_heads * ts
    pad_rows = k_scr.shape[0] - past
    new_k = jnp.concatenate([kvn_ref[:, :width], jnp.zeros((pad_rows - ts, width), F32)], axis=0)
    new_v = jnp.concatenate([kvn_ref[:, width:], jnp.zeros((pad_rows - ts, width), F32)], axis=0)
    k_scr[0:past, :] = cache_ref[:, :width].astype(BF16)
    v_scr[0:past, :] = cache_ref[:, width:].astype(BF16)
    k_scr[past:, :] = new_k.astype(BF16)
    v_scr[past:, :] = new_v.astype(BF16)
    qbd = _head_block_diag(q_ref[...] * (HEAD_DIM ** -0.5), n_heads, HEAD_DIM, ts).astype(BF16)
    s = _nt_dot(qbd, k_scr[...])
    t_row = lax.broadcasted_iota(jnp.int32, (rows, 1), 0) % ts
    dist = (past + t_row) - lax.broadcasted_iota(jnp.int32, (1, k_scr.shape[0]), 1)
    count = jnp.zeros(s.shape, F32)
    for window, dil in DIL_PAIRS:
        assert dil & (dil - 1) == 0
        count = count + ((dist >= 0) & (dist <= window) & ((dist & (dil - 1)) == 0)).astype(F32)
    s = jnp.where(count > 0, s - slope_ref[:, 0:1] * dist.astype(F32), NEG)
    m = jnp.max(s, axis=-1, keepdims=True)
    p = count * jnp.exp(s - m)
    l = jnp.sum(p, axis=-1, keepdims=True)
    o = jnp.dot(p.astype(BF16), v_scr[...], preferred_element_type=F32) / l
    o_ref[...] = _head_diag_rows(o, n_heads, HEAD_DIM, ts)
    ctx_ref[0:past - ts, :] = cache_ref[ts:past, :]
    ctx_ref[past - ts:past, :] = kvn_ref[...]


def _dil_sample(q, kv_new, cache, n_heads):
    b, ts, width = q.shape
    past = cache.shape[1]
    assert ts % 8 == 0 and past % LANES == 0
    slopes = jnp.broadcast_to(jnp.repeat(_alibi_slopes(n_heads), ts)[:, None], (n_heads * ts, LANES))
    per_b = lambda rows, w: pl.BlockSpec((None, rows, w), lambda i: (i, 0, 0))
    return pl.pallas_call(
        _dil_sample_kernel,
        grid=(b,),
        in_specs=[_resident(slopes.shape), per_b(ts, width), per_b(ts, 2 * width), per_b(past, 2 * width)],
        out_specs=[per_b(ts, width), per_b(past, 2 * width)],
        out_shape=[jax.ShapeDtypeStruct((b, ts, width), F32), jax.ShapeDtypeStruct(cache.shape, F32)],
        scratch_shapes=[pltpu.VMEM((past + LANES, width), BF16)] * 2,
        compiler_params=_cparams(1),
        name="dil_sample",
    )(slopes, q, kv_new, cache)


def _compress_pool_kernel(pool_ref, w_ref, pe_ref, o_ref, *slabs):
    pages, _, n_groups, hd, page_rows = pool_ref.shape
    pairs = n_groups * hd // LANES
    eye = (lax.broadcasted_iota(jnp.int32, (page_rows, page_rows), 0)
           == lax.broadcasted_iota(jnp.int32, (page_rows, page_rows), 1)).astype(BF16)

    def to_rows(p, carry):
        for c in range(2):
            for gp in range(pairs):
                tile_t = pool_ref[p, c, pl.ds(gp * (LANES // hd), LANES // hd)].reshape(LANES, page_rows).astype(BF16)
                slabs[c * pairs + gp][pl.ds(pl.multiple_of(p * page_rows, page_rows), page_rows), :] = _nt_dot(eye, tile_t)
        return carry

    lax.fori_loop(0, pages, to_rows, 0)
    _compress_kernel(*slabs, w_ref, pe_ref, o_ref, n_ch=pages * page_rows // CMP_STRIDE, merged=False)


def _compress_pool(pool, wbd, pe, pages_per_step):
    n_pages, _, n_groups, hd, page_rows = pool.shape
    width = 2 * n_groups * hd
    assert n_pages % pages_per_step == 0 and page_rows % (8 * CMP_STRIDE) == 0 and page_rows == LANES
    n_ch = pages_per_step * page_rows // CMP_STRIDE
    return pl.pallas_call(
        _compress_pool_kernel,
        grid=(n_pages // pages_per_step,),
        in_specs=[pl.BlockSpec((pages_per_step,) + pool.shape[1:], lambda i: (i, 0, 0, 0, 0)),
                  _resident(wbd.shape), _resident(pe.shape)],
        out_specs=pl.BlockSpec((n_ch, 2 * width), lambda i: (i, 0)),
        out_shape=jax.ShapeDtypeStruct((n_pages * page_rows // CMP_STRIDE, 2 * width), F32),
        scratch_shapes=[pltpu.VMEM((pages_per_step * page_rows, LANES), F32)] * (width // LANES),
        compiler_params=_cparams(1),
        name="nsa_compress_pool",
    )(pool, wbd, pe)


def _sample_query(q_ref, ts, n_groups):
    gw = n_groups * HEAD_DIM
    n_rep = q_ref.shape[1] // gw
    return jnp.concatenate(
        [_head_block_diag(q_ref[:, r * gw:(r + 1) * gw] * (HEAD_DIM ** -0.5), n_groups, HEAD_DIM, ts) for r in range(n_rep)],
        axis=0).astype(BF16)


def _sample_rows_out(o, ts, n_groups):
    per_r = n_groups * ts
    return [_head_diag_rows(o[r * per_r:(r + 1) * per_r, :], n_groups, HEAD_DIM, ts) for r in range(o.shape[0] // per_r)]


def _nsa_sample_cmp_kernel(pt_ref, slope_ref, c2s_ref, q_ref, kvn_ref, cwin_ref, *rest, pages, past, n_c, n_sel):
    del pt_ref
    ab_refs = rest[:pages]
    xcmp_ref, xwin_ref, sel_ref, winout_ref, ab_scr, kwin_scr, vwin_scr = rest[pages:]
    j = pl.program_id(1)
    ts = q_ref.shape[0]
    n_groups = kvn_ref.shape[1] // (2 * HEAD_DIM)
    gw = n_groups * HEAD_DIM
    for k in range(pages):
        ab_scr[pl.ds(pl.multiple_of((j * pages + k) * 8, 8), 8), :] = ab_refs[k][...]

    @pl.when(j == pl.num_programs(1) - 1)
    def _():
        n_ch = ab_scr.shape[0]
        n_rows = slope_ref.shape[0]
        slope = slope_ref[:, 0:1]
        qbd = _sample_query(q_ref, ts, n_groups)
        t_row = lax.broadcasted_iota(jnp.int32, (n_rows, 1), 0) % ts
        qp = past + t_row
        kvc = ab_scr[:, 0:2 * gw] + pltpu.roll(ab_scr[:, 2 * gw:4 * gw], n_ch - 1, axis=0)
        n_idx = lax.broadcasted_iota(jnp.int32, (1, n_ch), 1)
        cmp_end = n_idx * CMP_STRIDE + (CMP_LEN - 1)
        valid = (cmp_end <= qp) & (n_idx < n_c)
        s = jnp.where(valid, _nt_dot(qbd, kvc[:, 0:gw].astype(BF16)) - slope * (qp - cmp_end).astype(F32), NEG)
        m = jnp.max(s, axis=-1, keepdims=True)
        p = jnp.where(valid, jnp.exp(s - m), 0.0)
        p = (p / jnp.maximum(jnp.sum(p, axis=-1, keepdims=True), 1e-30)).astype(BF16)
        for r, x in enumerate(_sample_rows_out(jnp.dot(p, kvc[:, gw:2 * gw].astype(BF16), preferred_element_type=F32), ts, n_groups)):
            xcmp_ref[r * ts:(r + 1) * ts, :] = x
        imp_rows = jnp.dot(p, c2s_ref[...], preferred_element_type=F32)
        gt = n_groups * ts
        imp = jnp.sum(imp_rows.reshape(n_rows // gt, gt, imp_rows.shape[1]), axis=0)
        blk = lax.broadcasted_iota(jnp.int32, (1, imp.shape[1]), 1)
        cb = (past + lax.broadcasted_iota(jnp.int32, (gt, 1), 0) % ts) // SEL_BLOCK
        forced = (blk == 0) | (blk == cb) | (blk == cb - 1)
        score = jnp.where((blk <= cb) & (blk < n_sel), imp + FORCE_BONUS * forced.astype(F32), NEG)
        removed = -3e38
        cur = jnp.where(blk < n_sel, score, removed)
        chosen = jnp.zeros(score.shape, F32)
        for _ in range(SEL_TOP):
            top = jnp.max(cur, axis=-1, keepdims=True)
            first = jnp.min(jnp.where(cur == top, blk, imp.shape[1]), axis=-1, keepdims=True)
            pick = blk == first
            chosen = jnp.where(pick & (top > 0.5 * NEG), 1.0, chosen)
            cur = jnp.where(pick, removed, cur)
        n_steps = sel_ref.shape[0]
        per_step = (past // SEL_BLOCK) // n_steps
        lane = lax.broadcasted_iota(jnp.int32, (1, LANES), 1)
        for st in range(n_steps):
            shifted = chosen if st == 0 else pltpu.roll(chosen, chosen.shape[1] - st * per_step, axis=1)
            n_here = per_step + (n_sel - n_steps * per_step if st == n_steps - 1 else 0)
            piece = jnp.where(lane < n_here, shifted[:, 0:LANES], 0.0)
            sel_ref[st] = jnp.concatenate([piece] * (n_rows // gt), axis=0).astype(BF16)
        w_past = cwin_ref.shape[0]
        pad = kwin_scr.shape[0] - w_past
        kwin_scr[0:w_past, :] = cwin_ref[:, 0:gw].astype(BF16)
        vwin_scr[0:w_past, :] = cwin_ref[:, gw:2 * gw].astype(BF16)
        zeros = jnp.zeros((pad - ts, gw), F32)
        kwin_scr[w_past:, :] = jnp.concatenate([kvn_ref[:, 0:gw], zeros], axis=0).astype(BF16)
        vwin_scr[w_past:, :] = jnp.concatenate([kvn_ref[:, gw:2 * gw], zeros], axis=0).astype(BF16)
        dist = (w_past + t_row) - lax.broadcasted_iota(jnp.int32, (1, kwin_scr.shape[0]), 1)
        ok = (dist >= 0) & (dist <= WIN)
        s = jnp.where(ok, _nt_dot(qbd, kwin_scr[...]) - slope * dist.astype(F32), NEG)
        m = jnp.max(s, axis=-1, keepdims=True)
        p = jnp.where(ok, jnp.exp(s - m), 0.0)
        l = jnp.maximum(jnp.sum(p, axis=-1, keepdims=True), 1e-30)
        o = jnp.dot(p.astype(BF16), vwin_scr[...], preferred_element_type=F32) / l
        for r, x in enumerate(_sample_rows_out(o, ts, n_groups)):
            xwin_ref[r * ts:(r + 1) * ts, :] = x
        winout_ref[0:w_past - ts, :] = cwin_ref[ts:w_past, :]
        winout_ref[w_past - ts:w_past, :] = kvn_ref[...]


def _sample_slopes(n_groups, n_rep, ts):
    slopes = _alibi_slopes(n_groups * n_rep).reshape(n_groups, n_rep).T
    return jnp.broadcast_to(jnp.repeat(slopes.reshape(-1), ts)[:, None], (n_rep * n_groups * ts, LANES))


def _nsa_sample_cmp(q, kv_win_new, cache_win, ab_pool, page_table, n_groups, sel_steps, pages):
    b, ts, qw = q.shape
    n_rep = qw // HEAD_DIM // n_groups
    gw = n_groups * HEAD_DIM
    n_pages = page_table.shape[1]
    chunks_per_page = ab_pool.shape[1]
    past = n_pages * chunks_per_page * CMP_STRIDE
    n_ch = (past + ts) // CMP_STRIDE
    assert n_ch == n_pages * chunks_per_page and n_pages % pages == 0
    n_c = n_ch - CMP_LEN // CMP_STRIDE + 1
    n_sel = -(-(past + ts) // SEL_BLOCK)
    assert n_sel <= 2 * LANES and n_sel - past // SEL_BLOCK + past // SEL_BLOCK // sel_steps <= LANES
    n_rows = n_rep * n_groups * ts
    slopes = _sample_slopes(n_groups, n_rep, ts)
    cs = jnp.arange(n_ch)[:, None] * CMP_STRIDE
    ss = jnp.arange(2 * LANES)[None, :] * SEL_BLOCK
    ov = jnp.maximum(jnp.minimum(cs + CMP_LEN, ss + SEL_BLOCK) - jnp.maximum(cs, ss), 0).astype(F32) / CMP_STRIDE
    c2s = jnp.where((jnp.arange(n_ch)[:, None] < n_c) & (jnp.arange(2 * LANES)[None, :] < n_sel), ov, 0.0).astype(BF16)
    w_past = cache_win.shape[1]
    per_b = lambda rows, w: pl.BlockSpec((None, rows, w), lambda i, j, pt: (i, 0, 0))
    page_spec = lambda k: pl.BlockSpec((None, chunks_per_page, ab_pool.shape[2]),
                                       lambda i, j, pt: (pt[i, j * pages + k], 0, 0))
    const = lambda shape: pl.BlockSpec(shape, lambda i, j, pt: (0,) * len(shape), pipeline_mode=pl.Buffered(1))
    return pl.pallas_call(
        functools.partial(_nsa_sample_cmp_kernel, pages=pages, past=past, n_c=n_c, n_sel=n_sel),
        grid_spec=pltpu.PrefetchScalarGridSpec(
            num_scalar_prefetch=1,
            grid=(b, n_pages // pages),
            in_specs=[const(slopes.shape), const(c2s.shape), per_b(ts, qw), per_b(ts, 2 * gw), per_b(w_past, 2 * gw)]
                     + [page_spec(k) for k in range(pages)],
            out_specs=[per_b(n_rep * ts, gw), per_b(n_rep * ts, gw),
                       pl.BlockSpec((None, sel_steps, n_rows, LANES), lambda i, j, pt: (i, 0, 0, 0)),
                       per_b(w_past, 2 * gw)],
            scratch_shapes=[pltpu.VMEM((n_ch, ab_pool.shape[2]), F32),
                            pltpu.VMEM((w_past + LANES, gw), BF16), pltpu.VMEM((w_past + LANES, gw), BF16)]),
        out_shape=[jax.ShapeDtypeStruct((b, n_rep * ts, gw), F32), jax.ShapeDtypeStruct((b, n_rep * ts, gw), F32),
                   jax.ShapeDtypeStruct((b, sel_steps, n_rows, LANES), BF16),
                   jax.ShapeDtypeStruct(cache_win.shape, F32)],
        compiler_params=_cparams(2),
        name="nsa_sample_cmp",
    )(page_table, slopes, c2s, q, kv_win_new, cache_win, *([ab_pool] * pages))


def _nsa_sample_sel_kernel(pt_ref, slope_ref, exp_ref, q_ref, kvn_ref, sel_ref, xcmp_ref, xwin_ref, gate_ref, *rest,
                           pages, past):
    del pt_ref
    page_refs = rest[:pages]
    o_ref, m_scr, l_scr, acc_scr, new_scr = rest[pages:]
    j = pl.program_id(1)
    ts = q_ref.shape[0]
    n_groups = kvn_ref.shape[1] // (2 * HEAD_DIM)
    gw = n_groups * HEAD_DIM
    n_rows = slope_ref.shape[0]
    page_rows = page_refs[0].shape[-1]
    tk = pages * page_rows
    slope = slope_ref[:, 0:1]
    qbd = _sample_query(q_ref, ts, n_groups)
    t_row = lax.broadcasted_iota(jnp.int32, (n_rows, 1), 0) % ts

    @pl.when(j == 0)
    def _():
        m_scr[...] = jnp.full(m_scr.shape, NEG, F32)
        l_scr[...] = jnp.zeros(l_scr.shape, F32)
        acc_scr[...] = jnp.zeros(acc_scr.shape, F32)

    def update(k, v, ok, dist, transposed):
        qk = jnp.dot(qbd, k, preferred_element_type=F32) if transposed else _nt_dot(qbd, k)
        s = jnp.where(ok, qk - slope * dist.astype(F32), NEG)
        m_old = m_scr[:, 0:1]
        m_new = jnp.maximum(m_old, jnp.max(s, axis=-1, keepdims=True))
        alpha = jnp.exp(m_old - m_new)
        p = jnp.where(ok, jnp.exp(s - m_new), 0.0)
        pv = _nt_dot(p.astype(BF16), v) if transposed else jnp.dot(p.astype(BF16), v, preferred_element_type=F32)
        l_scr[...] = jnp.broadcast_to(alpha * l_scr[:, 0:1] + jnp.sum(p, axis=-1, keepdims=True), l_scr.shape)
        acc_scr[...] = alpha * acc_scr[...] + pv
        m_scr[...] = jnp.broadcast_to(m_new, m_scr.shape)

    sel = sel_ref[j]
    k = jnp.concatenate([ref[0].reshape(gw, page_rows) for ref in page_refs], axis=1).astype(BF16)
    v = jnp.concatenate([ref[1].reshape(gw, page_rows) for ref in page_refs], axis=1).astype(BF16)
    picked = _nt_dot(sel, exp_ref[0:tk, :]) > 0.5
    kpos = j * tk + lax.broadcasted_iota(jnp.int32, (1, tk), 1)
    update(k, v, picked, (past + t_row) - kpos, True)

    @pl.when(j == pl.num_programs(1) - 1)
    def _():
        pad = new_scr.shape[0]
        new_scr[...] = jnp.concatenate([kvn_ref[...], jnp.zeros((pad - ts, 2 * gw), F32)], axis=0).astype(BF16)
        i_new = lax.broadcasted_iota(jnp.int32, (1, pad), 1)
        ok = (_nt_dot(sel, exp_ref[tk:tk + pad, :]) > 0.5) & (i_new <= t_row) & (i_new < ts)
        update(new_scr[:, 0:gw], new_scr[:, gw:2 * gw], ok, t_row - i_new, False)
        o = acc_scr[...] / jnp.maximum(l_scr[:, 0:1], 1e-30)
        n_rep = n_rows // (n_groups * ts)
        for r, x_sel in enumerate(_sample_rows_out(o, ts, n_groups)):
            g_cmp, g_sel, g_win = (1.0 / (1.0 + jnp.exp(-gate_ref[:, (i * n_rep + r) * gw:(i * n_rep + r + 1) * gw]))
                                   for i in range(3))
            rows = slice(r * ts, (r + 1) * ts)
            o_ref[:, r * gw:(r + 1) * gw] = g_cmp * xcmp_ref[rows, :] + g_sel * x_sel + g_win * xwin_ref[rows, :]


def _nsa_sample_sel(q, kv_sel_new, sel, x_cmp, x_win, gates, pool, page_table, n_groups, pages):
    b, ts, qw = q.shape
    n_rep = qw // HEAD_DIM // n_groups
    gw = n_groups * HEAD_DIM
    n_pages = page_table.shape[1]
    page_rows = pool.shape[-1]
    past = n_pages * page_rows
    n_steps = n_pages // pages
    tk = pages * page_rows
    n_rows = n_rep * n_groups * ts
    assert sel.shape[1] == n_steps and tk % SEL_BLOCK == 0 and tk // SEL_BLOCK < LANES
    slopes = _sample_slopes(n_groups, n_rep, ts)
    blk_of = jnp.concatenate([jnp.arange(tk) // SEL_BLOCK, jnp.full((LANES,), tk // SEL_BLOCK)])
    expand = (blk_of[:, None] == jnp.arange(LANES)[None, :]).astype(BF16)
    per_b = lambda rows, w: pl.BlockSpec((None, rows, w), lambda i, j, pt: (i, 0, 0))
    page_spec = lambda k: pl.BlockSpec((None,) + pool.shape[1:], lambda i, j, pt: (pt[i, j * pages + k], 0, 0, 0, 0))
    const = lambda shape: pl.BlockSpec(shape, lambda i, j, pt: (0,) * len(shape), pipeline_mode=pl.Buffered(1))
    return pl.pallas_call(
        functools.partial(_nsa_sample_sel_kernel, pages=pages, past=past),
        grid_spec=pltpu.PrefetchScalarGridSpec(
            num_scalar_prefetch=1,
            grid=(b, n_steps),
            in_specs=[const(slopes.shape), const(expand.shape), per_b(ts, qw), per_b(ts, 2 * gw),
                      pl.BlockSpec((None, n_steps, n_rows, LANES), lambda i, j, pt: (i, 0, 0, 0)),
                      per_b(n_rep * ts, gw), per_b(n_rep * ts, gw), per_b(ts, 3 * qw)]
                     + [page_spec(k) for k in range(pages)],
            out_specs=per_b(ts, qw),
            scratch_shapes=[pltpu.VMEM((n_rows, LANES), F32), pltpu.VMEM((n_rows, LANES), F32),
                            pltpu.VMEM((n_rows, gw), F32), pltpu.VMEM((LANES, 2 * gw), BF16)]),
        out_shape=jax.ShapeDtypeStruct((b, ts, qw), F32),
        compiler_params=_cparams(2),
        name="nsa_sample_sel",
    )(page_table, slopes, expand, q, kv_sel_new, sel, x_cmp, x_win, gates, *([pool] * pages))


def _masked_softmax(s, mask):
    s = jnp.where(mask, s, NEG)
    m = jnp.max(s, axis=-1, keepdims=True)
    p = jnp.where(mask, jnp.exp(s - m), 0.0)
    return p / jnp.maximum(jnp.sum(p, axis=-1, keepdims=True), 1e-30)


def _dil_attn_sample(q, ctx, n_heads):
    t = q.shape[1]
    slopes = _alibi_slopes(n_heads)
    pos = ctx.shape[1] - t + jnp.arange(t)
    lses, outs = [], []
    for window, dil in DIL_PAIRS:
        dist = jnp.arange(window // dil + 1) * dil
        idx = pos[:, None] - dist[None, :]
        kvg = jnp.take(ctx, jnp.maximum(idx, 0), axis=1)
        s = jnp.einsum('bthd,btkhd->bhtk', q, kvg[:, :, :, 0], preferred_element_type=F32) * HEAD_DIM ** -0.5
        s = s - slopes[:, None, None] * dist.astype(F32)
        s = jnp.where(idx >= 0, s, NEG)
        m = jnp.max(s, axis=-1, keepdims=True)
        p = jnp.exp(s - m)
        l = jnp.sum(p, axis=-1, keepdims=True)
        outs.append(jnp.einsum('bhtk,btkhd->bthd', p / l, kvg[:, :, :, 1]))
        lses.append(jnp.swapaxes((m + jnp.log(l))[..., 0], 1, 2))
    wts = jax.nn.softmax(jnp.stack(lses), axis=0)[..., None]
    return jnp.sum(wts * jnp.stack(outs), axis=0)


def _nsa_sample(q, kv_cmp, kv_sel, kv_win, gate_logits, past_cmp, past_sel, win_buf, w_cmp, pe_cmp):
    b, t, g, r, _ = q.shape
    scale = HEAD_DIM ** -0.5
    slopes = _alibi_slopes(g * r).reshape(g, r)
    gates = jax.nn.sigmoid(gate_logits)
    ctx_cmp = jnp.concatenate([past_cmp, kv_cmp], axis=1)
    ctx_sel = jnp.concatenate([past_sel, kv_sel], axis=1)
    ctx_win = jnp.concatenate([win_buf, kv_win], axis=1)
    tc = ctx_cmp.shape[1]
    qp = (tc - t) + jnp.arange(t)
    n_r = CMP_LEN // CMP_STRIDE
    n_ch = tc // CMP_STRIDE
    n_c = n_ch - n_r + 1
    chunks = ctx_cmp[:, :n_ch * CMP_STRIDE].reshape((b, n_ch, CMP_STRIDE) + ctx_cmp.shape[2:])
    w = w_cmp.reshape(n_r, CMP_STRIDE, 2, HEAD_DIM, HEAD_DIM)
    kvc = jnp.einsum('lcd,lcde->ce', pe_cmp, w_cmp)[None, None, :, None, :]
    for i in range(n_r):
        kvc = kvc + jnp.einsum('bnscgd,scde->bncge', chunks[:, i:i + n_c], w[i])
    cmp_end = jnp.arange(n_c) * CMP_STRIDE + CMP_LEN - 1
    kc, vc = kvc[:, :, 0], kvc[:, :, 1]
    n_sel = -(-tc // SEL_BLOCK)
    n_top = min(SEL_TOP, n_sel)
    sel_blocks = jnp.pad(ctx_sel, ((0, 0), (0, n_sel * SEL_BLOCK - tc), (0, 0), (0, 0), (0, 0)))
    sel_blocks = sel_blocks.reshape(b, n_sel, SEL_BLOCK, 2, g, HEAD_DIM).transpose(0, 4, 1, 2, 3, 5)
    cs = jnp.arange(n_c)[:, None] * CMP_STRIDE
    ss = jnp.arange(n_sel)[None, :] * SEL_BLOCK
    cmp2sel = jnp.maximum(jnp.minimum(cs + CMP_LEN, ss + SEL_BLOCK) - jnp.maximum(cs, ss), 0).astype(F32) / CMP_STRIDE
    blk = jnp.arange(n_sel)
    s = jnp.einsum('btgrd,bngd->bgrtn', q, kc, preferred_element_type=F32) * scale
    s = s - slopes[:, :, None, None] * (qp[:, None] - cmp_end[None, :]).astype(F32)
    p = _masked_softmax(s, cmp_end[None, :] <= qp[:, None])
    o_cmp = jnp.einsum('bgrtn,bngd->btgrd', p, vc)
    imp = jnp.einsum('bgrtn,nj->bgtj', p, cmp2sel)
    cb = qp // SEL_BLOCK
    blk_ok = blk[None, :] <= cb[:, None]
    forced = (blk[None, :] == 0) | (blk[None, :] == cb[:, None]) | (blk[None, :] == cb[:, None] - 1)
    score = jnp.where(blk_ok, imp + FORCE_BONUS * forced.astype(F32), NEG)
    top_s, top_i = lax.top_k(score, n_top)
    picked = jnp.sum((top_i[..., None] == blk) & (top_s > 0.5 * NEG)[..., None], axis=-2) > 0
    kpos = jnp.arange(tc)
    ok = picked[..., kpos // SEL_BLOCK] & (kpos <= qp[:, None])
    s2 = jnp.einsum('btgrd,bkgd->bgrtk', q, ctx_sel[:, :, 0], preferred_element_type=F32) * scale
    s2 = s2 - slopes[None, :, :, None, None] * (qp[:, None] - kpos).astype(F32)
    p2 = _masked_softmax(s2, ok[:, :, None])
    o_sel = jnp.einsum('bgrtk,bkgd->btgrd', p2, ctx_sel[:, :, 1])
    tw = ctx_win.shape[1]
    dw = (tw - t + jnp.arange(t))[:, None] - jnp.arange(tw)[None, :]
    okw = (dw >= 0) & (dw <= WIN)
    s3 = jnp.einsum('btgrd,bsgd->bgrts', q, ctx_win[:, :, 0], preferred_element_type=F32) * scale
    s3 = s3 - slopes[:, :, None, None] * dw.astype(F32)
    p3 = _masked_softmax(s3, okw)
    o_win = jnp.einsum('bgrts,bsgd->btgrd', p3, ctx_win[:, :, 1])
    o = gates[..., 0:1] * o_cmp + gates[..., 1:2] * o_sel + gates[..., 2:3] * o_win
    return o, ctx_win[:, t:]


def _gate_columns(w_gate, n_groups):
    d = w_gate.shape[0]
    per_pair = w_gate.reshape(d, n_groups // 2, 2 * 4 * 3)
    return jnp.pad(per_pair, ((0, 0), (0, 0), (0, LANES - 2 * 4 * 3))).reshape(d, -1)


def kernel(x_prompt, x_sample, cache_dil_kv, state_conv, state_rnn, cache_win_kv, cache_cmp_kv, cache_sel_kv, page_table, norm_mix, norm_ffn, norm_out, w_in_ab, w_out_ab, conv_w, conv_b, gate_a_w, gate_a_b, gate_x_w, gate_x_b, lru_lambda, w_in_c, w_out_c, w_cmp, pe_cmp, ffn_w1, ffn_w3, ffn_w2):
    bp, t, d = x_prompt.shape
    bs, ts, _ = x_sample.shape
    depth = norm_mix.shape[0]
    bf = lambda z: z.astype(BF16)
    tm_p, tm_s = 512, bs * ts
    yp = x_prompt.reshape(bp * t, d)
    ys = x_sample.reshape(bs * ts, d)
    outs = {k: [] for k in ("dil_p", "dil_s", "conv_p", "conv_s", "rnn_p", "rnn_s",
                            "win_p", "win_s", "cmp_p", "cmp_s", "sel_p", "sel_s")}
    for layer in range(depth):
        li = layer // 2
        last = layer == depth - 1
        ffn = (norm_ffn[layer], bf(ffn_w1[layer]), bf(ffn_w3[layer]), bf(ffn_w2[layer]), norm_out if last else None)
        if layer % 2 == 0:
            rw = conv_w.shape[2]
            aw = (w_in_ab.shape[2] - 2 * rw) // 3
            n_heads = aw // HEAD_DIM
            w_in = bf(w_in_ab[li])
            splits = (aw, 2 * aw, rw, rw)
            wa, wx = bf(_block_diag(gate_a_w[li])), bf(_block_diag(gate_x_w[li]))
            lru = (conv_w[li], conv_b[li], wa, gate_a_b[li], wx, gate_x_b[li], lru_lambda[li])
            w_outs = bf(w_out_ab[li])
            q, kv, xr, gate = _norm_proj(yp, norm_mix[layer], w_in, splits, tm_p)
            o_att = _dil_attn(q.reshape(bp, t, aw), kv.reshape(bp, t, 2 * aw), n_heads)
            o_rnn, conv_new, h_new = _rglru(xr.reshape(bp, t, rw), gate.reshape(bp, t, rw),
                                            jnp.zeros((bp, CONV_W - 1, rw), F32), jnp.zeros((bp, rw), F32),
                                            *lru, bp, 256)
            yp = _out_ffn(yp, [o_att.reshape(bp * t, aw), o_rnn.reshape(bp * t, rw)], w_outs, *ffn, tm_p)
            keep = min(DIL_MAX, t)
            outs["dil_p"].append(kv.reshape(bp, t, 2, n_heads, HEAD_DIM)[:, t - keep:])
            outs["conv_p"].append(conv_new)
            outs["rnn_p"].append(h_new)
            q, kv, xr, gate = _norm_proj(ys, norm_mix[layer], w_in, splits, tm_s)
            cache = cache_dil_kv[li]
            o_att, ctx_new = _dil_sample(q.reshape(bs, ts, aw), kv.reshape(bs, ts, 2 * aw),
                                         cache.reshape(bs, cache.shape[1], 2 * aw), n_heads)
            o_rnn, conv_new, h_new = _rglru(xr.reshape(bs, ts, rw), gate.reshape(bs, ts, rw),
                                            state_conv[li], state_rnn[li], *lru, 8, ts)
            ys = _out_ffn(ys, [o_att.reshape(bs * ts, aw), o_rnn.reshape(bs * ts, rw)], w_outs, *ffn, tm_s)
            outs["dil_s"].append(ctx_new.reshape(cache.shape))
            outs["conv_s"].append(conv_new)
            outs["rnn_s"].append(h_new)
        else:
            n_groups = cache_win_kv.shape[4]
            kvw = 2 * n_groups * HEAD_DIM
            qw = w_out_c.shape[1]
            rep = qw // HEAD_DIM // n_groups
            w_in = bf(jnp.concatenate([w_in_c[li][:, :qw + 3 * kvw],
                                       _gate_columns(w_in_c[li][:, qw + 3 * kvw:], n_groups)], axis=1))
            gw = n_groups // 2 * LANES
            splits = (qw, kvw, kvw, kvw, gw)
            wbd, pe = _compress_weights(w_cmp[li], pe_cmp[li], n_groups)
            w_outs = bf(w_out_c[li])
            kv5 = lambda z, n: z.reshape(n, -1, 2, n_groups, HEAD_DIM)
            q, kv_cmp, kv_sel, kv_win, gates = _norm_proj(yp, norm_mix[layer], w_in, splits, tm_p)
            o = _nsa_prompt(q.reshape(bp, t, qw), kv_cmp.reshape(bp, t, kvw), kv_sel.reshape(bp, t, kvw),
                            kv_win.reshape(bp, t, kvw), gates.reshape(bp, t, gw), wbd, pe, n_groups)
            yp = _out_ffn(yp, [o.reshape(bp * t, qw)], w_outs, *ffn, tm_p)
            keep = min(WIN, t)
            outs["win_p"].append(kv5(kv_win, bp)[:, t - keep:])
            outs["cmp_p"].append(kv5(kv_cmp, bp))
            outs["sel_p"].append(kv5(kv_sel, bp))
            wq = w_in_c[li][:, :qw].reshape(d, n_groups, rep, HEAD_DIM).transpose(0, 2, 1, 3).reshape(d, qw)
            wg = w_in_c[li][:, qw + 3 * kvw:].reshape(d, n_groups, rep, 3).transpose(0, 3, 2, 1)
            wg = jnp.broadcast_to(wg[..., None], wg.shape + (HEAD_DIM,)).reshape(d, 3 * qw)
            w_in_s = bf(jnp.concatenate([wq, w_in_c[li][:, qw:qw + 3 * kvw], wg], axis=1))
            w_out_s = bf(w_out_c[li].reshape(n_groups, rep, HEAD_DIM, d).transpose(1, 0, 2, 3).reshape(qw, d))
            q, kv_cmp, kv_sel, kv_win, gates = _norm_proj(ys, norm_mix[layer], w_in_s, (qw, kvw, kvw, kvw, 3 * qw), tm_s)
            n_phys, page_rows = cache_cmp_kv.shape[1], cache_cmp_kv.shape[2]
            rows_minor = lambda pool: jnp.transpose(pool, (0, 2, 3, 4, 1))
            ab_pool = _compress_pool(rows_minor(cache_cmp_kv[li]), wbd, pe, 32)
            ab_pool = ab_pool.reshape(n_phys, page_rows // CMP_STRIDE, 2 * kvw)
            cwin = cache_win_kv[li]
            x_cmp, x_win, sel, win_new = _nsa_sample_cmp(q.reshape(bs, ts, qw), kv_win.reshape(bs, ts, kvw),
                                                         cwin.reshape(bs, cwin.shape[1], kvw), ab_pool, page_table,
                                                         n_groups, 8, 16)
            o = _nsa_sample_sel(q.reshape(bs, ts, qw), kv_sel.reshape(bs, ts, kvw), sel, x_cmp, x_win,
                                gates.reshape(bs, ts, 3 * qw), rows_minor(cache_sel_kv[li]), page_table, n_groups, 8)
            ys = _out_ffn(ys, [o.reshape(bs * ts, qw)], w_out_s, *ffn, tm_s)
            outs["win_s"].append(win_new.reshape(cwin.shape))
            outs["cmp_s"].append(kv5(kv_cmp, bs))
            outs["sel_s"].append(kv5(kv_sel, bs))
    st = lambda k: jnp.stack(outs[k])
    return (yp.reshape(bp, t, d), ys.reshape(bs, ts, d), st("dil_p"), st("dil_s"), st("conv_p"), st("conv_s"),
            st("rnn_p"), st("rnn_s"), st("win_p"), st("win_s"), st("cmp_p"), st("cmp_s"), st("sel_p"), st("sel_s"))
```

```python
import functools

import jax
import jax.numpy as jnp
from jax import lax
from jax.experimental import pallas as pl
from jax.experimental.pallas import tpu as pltpu

HEAD_DIM = 64
LANES = 128
DIL_PAIRS = ((128, 1), (512, 4), (2048, 16))
DIL_MAX = 2048
DIL_KEYS = 128
CONV_W = 4
LRU_C = 8.0
CMP_LEN = 32
CMP_STRIDE = 16
SEL_BLOCK = 64
SEL_TOP = 16
WIN = 512
NEG = -1e30
FORCE_BONUS = 1e3
EPS = 1e-6
VMEM_LIMIT = 56 * 1024 * 1024

F32 = jnp.float32
BF16 = jnp.bfloat16


def _cparams(n_grid):
    return pltpu.CompilerParams(dimension_semantics=("arbitrary",) * n_grid,
                                vmem_limit_bytes=VMEM_LIMIT)


def _resident(shape):
    return pl.BlockSpec(shape, lambda *_: (0,) * len(shape), pipeline_mode=pl.Buffered(1))


def _rms(x, g):
    return x * lax.rsqrt(jnp.mean(x * x, axis=-1, keepdims=True) + EPS) * g


def _alibi_slopes(n):
    return 2.0 ** (-8.0 * jnp.arange(1, n + 1, dtype=F32) / n)


def _norm_proj_kernel(x_ref, g_ref, w_ref, *out_refs):
    h = _rms(x_ref[...], g_ref[...]).astype(BF16)
    off = 0
    for o_ref in out_refs:
        n = o_ref.shape[-1]
        o_ref[...] = jnp.dot(h, w_ref[:, off:off + n], preferred_element_type=F32)
        off += n


def _norm_proj(x, g, w, splits, tm):
    n, d = x.shape
    assert n % tm == 0 and sum(splits) == w.shape[1]
    return pl.pallas_call(
        _norm_proj_kernel,
        grid=(n // tm,),
        in_specs=[pl.BlockSpec((tm, d), lambda i: (i, 0)),
                  _resident((1, d)),
                  _resident(w.shape)],
        out_specs=[pl.BlockSpec((tm, s), lambda i: (i, 0)) for s in splits],
        out_shape=[jax.ShapeDtypeStruct((n, s), F32) for s in splits],
        compiler_params=_cparams(1),
        name="norm_proj",
    )(x, g.reshape(1, d), w)


def _out_ffn_kernel(*refs, n_mix, n_chunks, final_norm):
    res_ref = refs[0]
    a_refs = refs[1:1 + n_mix]
    wo_ref, gf_ref, w1_ref, w3_ref, w2_ref = refs[1 + n_mix:6 + n_mix]
    rest = refs[6 + n_mix:]
    go_ref = rest[0] if final_norm else None
    o_ref = rest[-1]
    mix = jnp.concatenate([a_ref[...].astype(BF16) for a_ref in a_refs], axis=1)
    y = res_ref[...] + jnp.dot(mix, wo_ref[...], preferred_element_type=F32)
    h = _rms(y, gf_ref[...]).astype(BF16)
    ch = w1_ref.shape[1] // n_chunks
    for c in range(n_chunks):
        a = jnp.dot(h, w1_ref[:, c * ch:(c + 1) * ch], preferred_element_type=F32)
        b = jnp.dot(h, w3_ref[:, c * ch:(c + 1) * ch], preferred_element_type=F32)
        act = (a * (1.0 / (1.0 + jnp.exp(-a))) * b).astype(BF16)
        y = y + jnp.dot(act, w2_ref[c * ch:(c + 1) * ch, :], preferred_element_type=F32)
    if final_norm:
        y = _rms(y, go_ref[...])
    o_ref[...] = y


def _out_ffn(res, mixes, w_out, g_ffn, w1, w3, w2, g_out, tm):
    n, d = res.shape
    hidden = w1.shape[1]
    n_chunks = hidden // 256
    assert n % tm == 0 and hidden % 256 == 0
    final_norm = g_out is not None
    row = lambda width: pl.BlockSpec((tm, width), lambda i: (i, 0))
    in_specs = [row(d)] + [row(m.shape[1]) for m in mixes] + [_resident(w_out.shape)]
    in_specs += [_resident((1, d)), _resident(w1.shape), _resident(w3.shape), _resident(w2.shape)]
    args = [res, *mixes, w_out, g_ffn.reshape(1, d), w1, w3, w2]
    if final_norm:
        in_specs.append(_resident((1, d)))
        args.append(g_out.reshape(1, d))
    return pl.pallas_call(
        functools.partial(_out_ffn_kernel, n_mix=len(mixes), n_chunks=n_chunks, final_norm=final_norm),
        grid=(n // tm,),
        in_specs=in_specs,
        out_specs=row(d),
        out_shape=jax.ShapeDtypeStruct((n, d), F32),
        compiler_params=_cparams(1),
        name="out_ffn",
    )(*args)


def _dil_attn_kernel(slope_ref, q_ref, k_ref, v_ref, o_ref, *scratch, seq):
    ob_refs, lse_refs = scratch[:3], scratch[3:]
    hp = pl.program_id(1)
    lane_lo = lax.broadcasted_iota(jnp.int32, (1, LANES), 1) < HEAD_DIM
    tq, tk = DIL_KEYS, 2 * DIL_KEYS
    head1 = lax.broadcasted_iota(jnp.int32, (2 * tq, 1), 0) >= tq
    slope_rows = jnp.where(head1, slope_ref[pl.ds(2 * hp + 1, 1), 0:1], slope_ref[pl.ds(2 * hp, 1), 0:1])
    iq = lax.broadcasted_iota(jnp.int32, (2 * tq, tk), 0) % tq
    ik = lax.broadcasted_iota(jnp.int32, (2 * tq, tk), 1)
    unroll = 4

    for br, (_, dil) in enumerate(DIL_PAIRS):
        n_blk = seq // dil // tq
        assert (dil * n_blk) % unroll == 0

        def blocks(it, carry, br=br, dil=dil, n_blk=n_blk):
            for u in range(unroll):
                idx = it * unroll + u
                r = idx // n_blk
                blk = idx % n_blk
                a0 = blk * tq
                ka0 = jnp.maximum(blk - 1, 0) * tq
                q_rows = pl.ds(r + dil * a0, tq, stride=dil) if dil > 1 else pl.ds(a0, tq)
                k_rows = pl.ds(r + dil * ka0, tk, stride=dil) if dil > 1 else pl.ds(ka0, tk)
                q = q_ref[q_rows, :] * (HEAD_DIM ** -0.5)
                q2 = jnp.concatenate([jnp.where(lane_lo, q, 0.0), jnp.where(lane_lo, 0.0, q)], axis=0).astype(BF16)
                k = k_ref[k_rows, :].astype(BF16)
                v = v_ref[k_rows, :].astype(BF16)
                da = (a0 - ka0) + iq - ik
                valid = (da >= 0) & (da <= DIL_KEYS)
                s = jnp.where(valid, _nt_dot(q2, k) - slope_rows * (dil * da).astype(F32), NEG)
                m = jnp.max(s, axis=-1, keepdims=True)
                p = jnp.exp(s - m)
                l = jnp.sum(p, axis=-1, keepdims=True)
                out = jnp.dot(p.astype(BF16), v, preferred_element_type=F32) / l
                lse = m + jnp.log(l)
                ob_refs[br][q_rows, :] = jnp.where(lane_lo, out[0:tq], out[tq:2 * tq])
                lse_refs[br][q_rows, :] = jnp.where(lane_lo, lse[0:tq], lse[tq:2 * tq])
            return carry

        lax.fori_loop(0, dil * n_blk // unroll, blocks, 0)

    rows = 512
    def merge(i, carry):
        sl = pl.ds(i * rows, rows)
        ls = [lse_refs[b][sl, :] for b in range(3)]
        m = jnp.maximum(jnp.maximum(ls[0], ls[1]), ls[2])
        es = [jnp.exp(x - m) for x in ls]
        num = es[0] * ob_refs[0][sl, :] + es[1] * ob_refs[1][sl, :] + es[2] * ob_refs[2][sl, :]
        o_ref[sl, :] = num / (es[0] + es[1] + es[2])
        return carry
    lax.fori_loop(0, seq // rows, merge, 0)


def _dil_attn(q, kv, n_heads):
    b, t, width = q.shape
    n_hp = width // LANES
    assert t % (16 * 2 * DIL_KEYS) == 0
    slopes = jnp.broadcast_to(_alibi_slopes(n_heads)[:, None], (n_heads, LANES))
    blk = lambda off: pl.BlockSpec((None, t, LANES), lambda i, j: (i, 0, off + j))
    return pl.pallas_call(
        functools.partial(_dil_attn_kernel, seq=t),
        grid=(b, n_hp),
        in_specs=[_resident((n_heads, LANES)), blk(0), blk(0), blk(n_hp)],
        out_specs=blk(0),
        out_shape=jax.ShapeDtypeStruct((b, t, width), F32),
        scratch_shapes=[pltpu.VMEM((t, LANES), F32)] * 6,
        compiler_params=_cparams(2),
        name="dil_attn",
    )(slopes, q, kv, kv)


def _rglru_kernel(xr_ref, gate_ref, cprev_ref, hprev_ref, cw_ref, cb_ref, wa_ref, ba_ref,
                  wx_ref, bx_ref, lam_ref, y_ref, cnew_ref, hnew_ref,
                  xin_ref, a_ref, u_ref, h_ref):
    ti = pl.program_id(1)
    bb, tc, width = xr_ref.shape
    pad = 8
    tail = CONV_W - 1

    @pl.when(ti == 0)
    def _():
        xin_ref[:, pad - tail:pad, :] = cprev_ref[...]
        h_ref[...] = hprev_ref[...]

    xin_ref[:, pad:pad + tc, :] = xr_ref[...]
    lam = lam_ref[...]
    neg_softplus_c = -LRU_C * (jnp.maximum(-lam, 0.0) + jnp.log1p(jnp.exp(-jnp.abs(lam))))
    for b in range(bb):
        xc = cb_ref[...] + sum(xin_ref[b, pad - tail + k:pad - tail + k + tc, :] * cw_ref[k:k + 1, :]
                               for k in range(CONV_W))
        xcb = xc.astype(BF16)
        ra = jnp.dot(xcb, wa_ref[...], preferred_element_type=F32) + ba_ref[...]
        rx = jnp.dot(xcb, wx_ref[...], preferred_element_type=F32) + bx_ref[...]
        rg = 1.0 / (1.0 + jnp.exp(-ra))
        ig = 1.0 / (1.0 + jnp.exp(-rx))
        log_a = neg_softplus_c * rg
        a = jnp.exp(log_a)
        a_ref[b] = a
        u_ref[b] = jnp.sqrt(jnp.tanh(-log_a) * (1.0 + a * a)) * ig * xc

    def step(t, hs):
        new = []
        for b in range(bb):
            hb = a_ref[b, pl.ds(t, 1), :] * hs[b] + u_ref[b, pl.ds(t, 1), :]
            u_ref[b, pl.ds(t, 1), :] = hb
            new.append(hb)
        return tuple(new)

    hs = lax.fori_loop(0, tc, step, tuple(h_ref[b:b + 1, :] for b in range(bb)), unroll=8)
    for b in range(bb):
        h_ref[b:b + 1, :] = hs[b]
    g = gate_ref[...]
    cdf = 0.5 * (1.0 + jnp.tanh(0.7978845608028654 * (g + 0.044715 * (g * g * g))))
    y_ref[...] = u_ref[...] * (g * cdf)
    xin_ref[:, pad - tail:pad, :] = xin_ref[:, pad + tc - tail:pad + tc, :]
    cnew_ref[...] = xin_ref[:, pad - tail:pad, :]
    hnew_ref[...] = h_ref[...]


def _rglru(xr, gate, conv_prev, h_prev, conv_w, conv_b, wa_bd, ba, wx_bd, bx, lam, bb, tc):
    b, t, width = xr.shape
    assert b % bb == 0 and t % tc == 0 and tc >= CONV_W - 1
    seq_blk = pl.BlockSpec((bb, tc, width), lambda i, j: (i, j, 0))
    vec = _resident((1, width))
    return pl.pallas_call(
        _rglru_kernel,
        grid=(b // bb, t // tc),
        in_specs=[seq_blk, seq_blk,
                  pl.BlockSpec((bb, CONV_W - 1, width), lambda i, j: (i, 0, 0)),
                  pl.BlockSpec((bb, width), lambda i, j: (i, 0)),
                  _resident((CONV_W, width)), vec, _resident(wa_bd.shape), vec,
                  _resident(wx_bd.shape), vec, vec],
        out_specs=[seq_blk,
                   pl.BlockSpec((bb, CONV_W - 1, width), lambda i, j: (i, 0, 0)),
                   pl.BlockSpec((bb, width), lambda i, j: (i, 0))],
        out_shape=[jax.ShapeDtypeStruct((b, t, width), F32),
                   jax.ShapeDtypeStruct((b, CONV_W - 1, width), F32),
                   jax.ShapeDtypeStruct((b, width), F32)],
        scratch_shapes=[pltpu.VMEM((bb, tc + 8, width), F32), pltpu.VMEM((bb, tc, width), F32),
                        pltpu.VMEM((bb, tc, width), F32), pltpu.VMEM((bb, width), F32)],
        compiler_params=_cparams(2),
        name="rglru",
    )(xr, gate, conv_prev, h_prev, conv_w, conv_b.reshape(1, width), wa_bd, ba.reshape(1, width),
      wx_bd, bx.reshape(1, width), lam.reshape(1, width))


def _block_diag(w):
    n, bi, bj = w.shape
    eye = jnp.eye(n, dtype=w.dtype)
    return (eye[:, None, :, None] * w[:, :, None, :]).reshape(n * bi, n * bj)


def _lane_lo():
    return lax.broadcasted_iota(jnp.int32, (1, LANES), 1) < HEAD_DIM


def _spread_kv(k2, v2, half):
    lo = _lane_lo()
    own = lo if half == 0 else ~lo
    k_own = jnp.where(own, k2, 0.0)
    v_own = jnp.where(own, v2, 0.0)
    k_other = pltpu.roll(k_own, HEAD_DIM, axis=1)
    v_both = v_own + pltpu.roll(v_own, HEAD_DIM, axis=1)
    k_lo, k_hi = (k_own, k_other) if half == 0 else (k_other, k_own)
    return k_lo.astype(BF16), k_hi.astype(BF16), v_both.astype(BF16)


def _group_heads(q_ref, gl):
    base = gl * 4 * HEAD_DIM
    qa = (q_ref[:, base:base + LANES] * (HEAD_DIM ** -0.5)).astype(BF16)
    qb = (q_ref[:, base + LANES:base + 2 * LANES] * (HEAD_DIM ** -0.5)).astype(BF16)
    return ((qa, 0), (qa, 1), (qb, 0), (qb, 1))


def _nt_dot(a, b):
    return lax.dot_general(a, b, (((1,), (1,)), ((), ())), preferred_element_type=F32)


def _compress_kernel(*refs, n_ch, merged):
    x_refs, (w_ref, pe_ref, o_ref) = refs[:-3], refs[-3:]
    per_half = len(x_refs) // 2
    half = per_half * LANES
    for c in range(2):
        acc = [None, None]
        pe_acc = None
        for s in range(CMP_STRIDE):
            xs = jnp.concatenate([x_ref[pl.ds(s, n_ch, stride=CMP_STRIDE), :]
                                  for x_ref in x_refs[c * per_half:(c + 1) * per_half]], axis=1).astype(BF16)
            for i in range(2):
                w = w_ref[(i * CMP_STRIDE + s) * 2 + c]
                d = jnp.dot(xs, w, preferred_element_type=F32)
                acc[i] = d if acc[i] is None else acc[i] + d
                pe_row = jnp.broadcast_to(pe_ref[i * CMP_STRIDE + s:i * CMP_STRIDE + s + 1, c * half:(c + 1) * half],
                                          (8, half)).astype(BF16)
                dp = jnp.dot(pe_row, w, preferred_element_type=F32)
                pe_acc = dp if pe_acc is None else pe_acc + dp
        if merged:
            o_ref[:, c * half:(c + 1) * half] = acc[0] + pltpu.roll(acc[1], n_ch - 1, axis=0) + pe_acc[0:1, :]
        else:
            o_ref[:, c * half:(c + 1) * half] = acc[0] + pe_acc[0:1, :]
            o_ref[:, (2 + c) * half:(3 + c) * half] = acc[1]


def _compress_weights(w_cmp, pe_cmp, n_groups):
    n_r = CMP_LEN // CMP_STRIDE
    w = w_cmp.reshape(n_r * CMP_STRIDE * 2, HEAD_DIM, HEAD_DIM)
    eye = jnp.eye(n_groups, dtype=w.dtype)
    wbd = (eye[None, :, None, :, None] * w[:, None, :, None, :]).reshape(-1, n_groups * HEAD_DIM, n_groups * HEAD_DIM)
    pe = jnp.broadcast_to(pe_cmp[:, :, None, :], (CMP_LEN, 2, n_groups, HEAD_DIM)).reshape(CMP_LEN, -1)
    return wbd.astype(BF16), pe


def _compress(ctx, wbd, pe):
    b, tc, width = ctx.shape
    n_ch = tc // CMP_STRIDE
    return pl.pallas_call(
        functools.partial(_compress_kernel, n_ch=n_ch, merged=True),
        grid=(b,),
        in_specs=[pl.BlockSpec((None, tc, LANES), functools.partial(lambda j, i: (i, 0, j), j))
                  for j in range(width // LANES)] + [_resident(wbd.shape), _resident(pe.shape)],
        out_specs=pl.BlockSpec((None, n_ch, width), lambda i: (i, 0, 0)),
        out_shape=jax.ShapeDtypeStruct((b, n_ch, width), F32),
        compiler_params=_cparams(1),
        name="nsa_compress",
    )(*([ctx] * (width // LANES)), wbd, pe)


def _cmp_select_kernel(qtab_ref, ktab_ref, c2s_ref, q_ref, kc_ref, vc_ref, o_ref, sel_ref, *, n_c, n_sel):
    gp, qi = pl.program_id(1), pl.program_id(2)
    tq = q_ref.shape[0]
    n_ch = kc_ref.shape[0]
    qp = qi * tq + lax.broadcasted_iota(jnp.int32, (tq, 1), 0)
    n_idx = lax.broadcasted_iota(jnp.int32, (1, n_ch), 1)
    cmp_end = n_idx * CMP_STRIDE + (CMP_LEN - 1)
    valid = (cmp_end <= qp) & (n_idx < n_c)
    blk = lax.broadcasted_iota(jnp.int32, (1, LANES), 1)
    cb = qp // SEL_BLOCK
    blk_ok = (blk <= cb) & (blk < n_sel)
    forced = (blk == 0) | (blk == cb) | (blk == cb - 1)
    blk_t = lax.broadcasted_iota(jnp.int32, (n_sel, tq), 0)
    valid4 = jnp.concatenate([valid] * 4, axis=0)
    for gl in range(2):
        k_aug, v_both = _stage_kv_aug(kc_ref, vc_ref, ktab_ref, gl)
        q_all = _stack_heads(q_ref, gl, qtab_ref, (2 * gp + gl) * 4, None)
        s = jnp.where(valid4, _nt_dot(q_all, k_aug), NEG)
        m = jnp.max(s, axis=-1, keepdims=True)
        p = jnp.where(valid4, jnp.exp(s - m), 0.0)
        p = (p / jnp.maximum(jnp.sum(p, axis=-1, keepdims=True), 1e-30)).astype(BF16)
        _unstack_heads(o_ref, gl, jnp.dot(p, v_both, preferred_element_type=F32), tq)
        imp_rows = jnp.dot(p, c2s_ref[...], preferred_element_type=F32)
        imp = imp_rows[0:tq] + imp_rows[tq:2 * tq] + imp_rows[2 * tq:3 * tq] + imp_rows[3 * tq:4 * tq]
        score = jnp.where(blk_ok, imp + FORCE_BONUS * forced.astype(F32), NEG)
        score_t = score.T[0:n_sel, :]
        rank = jnp.zeros((n_sel, tq), F32)
        for k in range(n_sel):
            row = score_t[k:k + 1, :]
            ahead = (row > score_t) | ((row == score_t) & (k < blk_t))
            rank = rank + ahead.astype(F32)
        chosen = ((rank < SEL_TOP) & (score_t > 0.5 * NEG)).astype(F32)
        chosen = jnp.concatenate([chosen, jnp.zeros((LANES - n_sel, tq), F32)], axis=0) if n_sel < LANES else chosen
        sel_ref[gl] = chosen.T.astype(BF16)


def _cmp_select(q, kvc, n_c, n_sel, n_groups, tq):
    b, t, width = q.shape
    n_ch = kvc.shape[1]
    n_gp = n_groups // 2
    n_heads = width // HEAD_DIM
    assert tq == LANES and n_sel <= LANES and n_sel % 8 == 0 and t % tq == 0
    qtab, ktab = _alibi_tables(n_heads, jnp.arange(n_ch) * CMP_STRIDE + (CMP_LEN - 1))
    cs = jnp.arange(n_ch)[:, None] * CMP_STRIDE
    ss = jnp.arange(LANES)[None, :] * SEL_BLOCK
    ov = jnp.maximum(jnp.minimum(cs + CMP_LEN, ss + SEL_BLOCK) - jnp.maximum(cs, ss), 0).astype(F32) / CMP_STRIDE
    c2s = jnp.where((jnp.arange(n_ch)[:, None] < n_c) & (jnp.arange(LANES)[None, :] < n_sel), ov, 0.0).astype(BF16)
    return pl.pallas_call(
        functools.partial(_cmp_select_kernel, n_c=n_c, n_sel=n_sel),
        grid=(b, n_gp, t // tq),
        in_specs=[_resident(qtab.shape), _resident(ktab.shape), _resident(c2s.shape),
                  pl.BlockSpec((None, tq, 2 * 4 * HEAD_DIM), lambda i, j, k: (i, k, j)),
                  pl.BlockSpec((None, n_ch, LANES), lambda i, j, k: (i, 0, j)),
                  pl.BlockSpec((None, n_ch, LANES), lambda i, j, k: (i, 0, n_gp + j))],
        out_specs=[pl.BlockSpec((None, tq, 2 * 4 * HEAD_DIM), lambda i, j, k: (i, k, j)),
                   pl.BlockSpec((None, 2, tq, LANES), lambda i, j, k: (i, j, k, 0))],
        out_shape=[jax.ShapeDtypeStruct((b, t, width), F32),
                   jax.ShapeDtypeStruct((b, n_groups, t, LANES), BF16)],
        compiler_params=_cparams(3),
        name="nsa_cmp_select",
    )(qtab, ktab, c2s, q, kvc, kvc)


def _stage_kv(k_ref, v_ref, kv_scr):
    for gl in range(2):
        k_lo, k_hi, v_both = _spread_kv(k_ref[...], v_ref[...], gl)
        kv_scr[3 * gl + 0][...] = k_lo
        kv_scr[3 * gl + 1][...] = k_hi
        kv_scr[3 * gl + 2][...] = v_both


def _sel_attn_kernel(slope_ref, exp_ref, q_ref, sel_ref, k_ref, v_ref, o_ref, *kv_scr, tk):
    gp, qi = pl.program_id(1), pl.program_id(2)
    tq = q_ref.shape[0]
    lo = _lane_lo()

    @pl.when(qi == 0)
    def _():
        _stage_kv(k_ref, v_ref, kv_scr)

    qp = qi * tq + lax.broadcasted_iota(jnp.int32, (tq, 1), 0)
    n_kt = (qi * tq + tq - 1) // tk + 1
    for gl in range(2):
        heads = _group_heads(q_ref, gl)
        slopes = [slope_ref[pl.ds((2 * gp + gl) * 4 + r, 1), 0:1] for r in range(4)]
        sel = sel_ref[gl]
        k_scr = (kv_scr[3 * gl], kv_scr[3 * gl + 1])
        v_scr = kv_scr[3 * gl + 2]

        def tile(kt, carry, heads=heads, slopes=slopes, sel=sel, k_scr=k_scr, v_scr=v_scr):
            rows = pl.ds(pl.multiple_of(kt * tk, tk), tk)
            kpos = kt * tk + lax.broadcasted_iota(jnp.int32, (1, tk), 1)
            picked = _nt_dot(sel, exp_ref[rows, :]) > 0.5
            ok = picked & (kpos <= qp)
            dist = (qp - kpos).astype(F32)
            v = v_scr[rows, :]
            new = []
            for r, (qh, variant) in enumerate(heads):
                m_old, l_old, acc_old = carry[r]
                s = jnp.where(ok, _nt_dot(qh, k_scr[variant][rows, :]) - slopes[r] * dist, NEG)
                m_new = jnp.maximum(m_old, jnp.max(s, axis=-1, keepdims=True))
                alpha = jnp.exp(m_old - m_new)
                p = jnp.where(ok, jnp.exp(s - m_new), 0.0)
                l_new = alpha * l_old + jnp.sum(p, axis=-1, keepdims=True)
                acc_new = alpha * acc_old + jnp.dot(p.astype(BF16), v, preferred_element_type=F32)
                new.append((m_new, l_new, acc_new))
            return tuple(new)

        init = tuple((jnp.full((tq, 1), NEG, F32), jnp.zeros((tq, 1), F32), jnp.zeros((tq, LANES), F32))
                     for _ in range(4))
        res = lax.fori_loop(0, n_kt, tile, init)
        outs = [acc / jnp.maximum(l, 1e-30) for (_, l, acc) in res]
        base = gl * 4 * HEAD_DIM
        o_ref[:, base:base + LANES] = jnp.where(lo, outs[0], outs[1])
        o_ref[:, base + LANES:base + 2 * LANES] = jnp.where(lo, outs[2], outs[3])


def _sel_attn(q, kv, sel, n_groups, tq, tk):
    b, t, width = q.shape
    n_gp = n_groups // 2
    n_heads = width // HEAD_DIM
    assert t % tq == 0 and t % tk == 0 and tk % SEL_BLOCK == 0
    slopes = jnp.broadcast_to(_alibi_slopes(n_heads)[:, None], (n_heads, LANES))
    expand = (jnp.arange(t)[:, None] // SEL_BLOCK == jnp.arange(LANES)[None, :]).astype(BF16)
    return pl.pallas_call(
        functools.partial(_sel_attn_kernel, tk=tk),
        grid=(b, n_gp, t // tq),
        in_specs=[_resident(slopes.shape), _resident(expand.shape),
                  pl.BlockSpec((None, tq, 2 * 4 * HEAD_DIM), lambda i, j, k: (i, k, j)),
                  pl.BlockSpec((None, 2, tq, LANES), lambda i, j, k: (i, j, k, 0)),
                  pl.BlockSpec((None, t, LANES), lambda i, j, k: (i, 0, j)),
                  pl.BlockSpec((None, t, LANES), lambda i, j, k: (i, 0, n_gp + j))],
        out_specs=pl.BlockSpec((None, tq, 2 * 4 * HEAD_DIM), lambda i, j, k: (i, k, j)),
        out_shape=jax.ShapeDtypeStruct((b, t, width), F32),
        scratch_shapes=[pltpu.VMEM((t, LANES), BF16)] * 6,
        compiler_params=_cparams(3),
        name="nsa_sel_attn",
    )(slopes, expand, q, sel, kv, kv)


ALIBI_LANES = 6
MASK_BIG = 2.0 ** 100


def _alibi_tables(n_heads, pos):
    t = pos.shape[0]
    slopes = _alibi_slopes(n_heads)
    s1 = slopes.astype(BF16).astype(F32)
    s2 = (slopes - s1).astype(BF16).astype(F32)
    s3 = (slopes - s1 - s2).astype(BF16).astype(F32)
    qtab = jnp.zeros((n_heads, LANES), F32).at[:, HEAD_DIM:HEAD_DIM + ALIBI_LANES].set(
        jnp.stack([s1, s2, s3, s1, s2, s3], axis=-1))
    hi = (pos // SEL_BLOCK * SEL_BLOCK).astype(F32)
    lo = (pos % SEL_BLOCK).astype(F32)
    ktab = jnp.zeros((t, LANES), F32).at[:, HEAD_DIM:HEAD_DIM + ALIBI_LANES].set(
        jnp.stack([hi, hi, hi, lo, lo, lo], axis=-1))
    return qtab, ktab


def _stack_heads(q_ref, gl, qtab_ref, first_head, extra):
    lo = _lane_lo()
    parts = []
    for r in range(4):
        c0 = gl * 4 * HEAD_DIM + (r // 2) * LANES
        slab = q_ref[:, c0:c0 + LANES] * (HEAD_DIM ** -0.5)
        if r % 2:
            slab = pltpu.roll(slab, HEAD_DIM, axis=1)
        qa = jnp.where(lo, slab, qtab_ref[pl.ds(first_head + r, 1), :]).astype(BF16)
        parts.append(qa if extra is None else jnp.concatenate([qa, extra], axis=1))
    return jnp.concatenate(parts, axis=0)


def _stage_kv_aug(k_ref, v_ref, ktab_ref, gl):
    lo = _lane_lo()
    own = lo if gl == 0 else ~lo
    k_own = jnp.where(own, k_ref[...], 0.0)
    v_own = jnp.where(own, v_ref[...], 0.0)
    k_low = k_own if gl == 0 else pltpu.roll(k_own, HEAD_DIM, axis=1)
    return (k_low + ktab_ref[...]).astype(BF16), (v_own + pltpu.roll(v_own, HEAD_DIM, axis=1)).astype(BF16)


def _unstack_heads(o_ref, gl, out, tq):
    lo = _lane_lo()
    base = gl * 4 * HEAD_DIM
    o_ref[:, base:base + LANES] = jnp.where(lo, out[0:tq], out[tq:2 * tq])
    o_ref[:, base + LANES:base + 2 * LANES] = jnp.where(lo, out[2 * tq:3 * tq], out[3 * tq:4 * tq])


def _sel_attn2_kernel(qtab_ref, ktab_ref, q_ref, sel_ref, k_ref, v_ref, o_ref, ka0, ka1, v0, v1, qv_ref, *, tk):
    gp, qi = pl.program_id(1), pl.program_id(2)
    tq = q_ref.shape[0]
    staged = ((ka0, v0), (ka1, v1))
    n_var = qv_ref.shape[0]
    blocks_per_tile = tk // SEL_BLOCK
    mask_lane0 = HEAD_DIM + ALIBI_LANES

    @pl.when(qi == 0)
    def _():
        for gl, (ka, vs) in enumerate(staged):
            ka[...], vs[...] = _stage_kv_aug(k_ref, v_ref, ktab_ref, gl)

    q0 = qi * tq
    n_full = q0 // tk
    qp = q0 + lax.broadcasted_iota(jnp.int32, (4 * tq, 1), 0) % tq
    lane = lax.broadcasted_iota(jnp.int32, (1, LANES), 1)
    mask_lanes = (lane >= mask_lane0) & (lane < mask_lane0 + blocks_per_tile)
    for gl, (ka, vs) in enumerate(staged):
        q_base = _stack_heads(q_ref, gl, qtab_ref, (2 * gp + gl) * 4, None)
        not_sel = sel_ref[gl].astype(F32) - 1.0
        for var in range(n_var):
            shift = (mask_lane0 - var * blocks_per_tile) % LANES
            flags = jnp.where(mask_lanes, pltpu.roll(not_sel, shift, axis=1) if shift else not_sel, 0.0).astype(BF16)
            qv_ref[var] = q_base + jnp.concatenate([flags] * 4, axis=0)

        def tile(kt, carry, masked, ka=ka, vs=vs):
            m_old, l_old, acc_old = carry
            rows = pl.ds(pl.multiple_of(kt * tk, tk), tk)
            s = _nt_dot(qv_ref[kt], ka[rows, :])
            if masked:
                kpos = kt * tk + lax.broadcasted_iota(jnp.int32, (1, tk), 1)
                s = jnp.where(kpos <= qp, s, NEG)
            m_new = jnp.maximum(m_old, jnp.max(s, axis=-1, keepdims=True))
            alpha = jnp.exp(m_old - m_new)
            p = jnp.exp(s - m_new)
            l_new = alpha * l_old + jnp.sum(p, axis=-1, keepdims=True)
            acc_new = alpha * acc_old + jnp.dot(p.astype(BF16), vs[rows, :], preferred_element_type=F32)
            return m_new, l_new, acc_new

        init = (jnp.full((4 * tq, 1), NEG, F32), jnp.zeros((4 * tq, 1), F32), jnp.zeros((4 * tq, LANES), F32))
        carry = lax.fori_loop(0, n_full, functools.partial(tile, masked=False), init)
        _, l, acc = tile(n_full, carry, True)
        _unstack_heads(o_ref, gl, acc / jnp.maximum(l, 1e-30), tq)


def _sel_attn2(q, kv, sel, n_groups, tq, tk):
    b, t, width = q.shape
    n_gp = n_groups // 2
    n_heads = width // HEAD_DIM
    blocks_per_tile = tk // SEL_BLOCK
    assert t % tk == 0 and tk % tq == 0 and HEAD_DIM + ALIBI_LANES + blocks_per_tile <= LANES
    qtab, ktab = _alibi_tables(n_heads, jnp.arange(t))
    blk_lane = HEAD_DIM + ALIBI_LANES + (jnp.arange(t) // SEL_BLOCK) % blocks_per_tile
    ktab = ktab + jnp.where(blk_lane[:, None] == jnp.arange(LANES)[None, :], MASK_BIG, 0.0)
    return pl.pallas_call(
        functools.partial(_sel_attn2_kernel, tk=tk),
        grid=(b, n_gp, t // tq),
        in_specs=[_resident(qtab.shape), _resident(ktab.shape),
                  pl.BlockSpec((None, tq, 2 * 4 * HEAD_DIM), lambda i, j, k: (i, k, j)),
                  pl.BlockSpec((None, 2, tq, LANES), lambda i, j, k: (i, j, k, 0)),
                  pl.BlockSpec((None, t, LANES), lambda i, j, k: (i, 0, j)),
                  pl.BlockSpec((None, t, LANES), lambda i, j, k: (i, 0, n_gp + j))],
        out_specs=pl.BlockSpec((None, tq, 2 * 4 * HEAD_DIM), lambda i, j, k: (i, k, j)),
        out_shape=jax.ShapeDtypeStruct((b, t, width), F32),
        scratch_shapes=[pltpu.VMEM((t, LANES), BF16)] * 4 + [pltpu.VMEM((t // tk, 4 * tq, LANES), BF16)],
        compiler_params=_cparams(3),
        name="nsa_sel_attn",
    )(qtab, ktab, q, sel, kv, kv)


def _win_combine_kernel(slope_ref, q_ref, gate_ref, ocmp_ref, osel_ref, k_ref, v_ref, o_ref, *kv_scr, seq):
    gp, qi = pl.program_id(1), pl.program_id(2)
    tq = q_ref.shape[0]
    tk = WIN + tq
    lo = _lane_lo()

    @pl.when(qi == 0)
    def _():
        _stage_kv(k_ref, v_ref, kv_scr)

    k0 = jnp.clip(qi * tq - WIN, 0, seq - tk)
    rows = pl.ds(pl.multiple_of(k0, tq), tk)
    qp = qi * tq + lax.broadcasted_iota(jnp.int32, (tq, 1), 0)
    kpos = k0 + lax.broadcasted_iota(jnp.int32, (1, tk), 1)
    dist_i = qp - kpos
    ok = (dist_i >= 0) & (dist_i <= WIN)
    dist = dist_i.astype(F32)
    gates = 1.0 / (1.0 + jnp.exp(-gate_ref[...]))
    for gl in range(2):
        outs = []
        v = kv_scr[3 * gl + 2][rows, :]
        for r, (qh, variant) in enumerate(_group_heads(q_ref, gl)):
            slope = slope_ref[pl.ds((2 * gp + gl) * 4 + r, 1), 0:1]
            s = jnp.where(ok, _nt_dot(qh, kv_scr[3 * gl + variant][rows, :]) - slope * dist, NEG)
            m = jnp.max(s, axis=-1, keepdims=True)
            p = jnp.where(ok, jnp.exp(s - m), 0.0)
            l = jnp.sum(p, axis=-1, keepdims=True)
            outs.append(jnp.dot(p.astype(BF16), v, preferred_element_type=F32) / jnp.maximum(l, 1e-30))
        for half in range(2):
            cols = slice(gl * 4 * HEAD_DIM + half * LANES, gl * 4 * HEAD_DIM + (half + 1) * LANES)
            o_win = jnp.where(lo, outs[2 * half], outs[2 * half + 1])
            c0 = (gl * 4 + 2 * half) * 3
            g_cmp, g_sel, g_win = (jnp.where(lo, gates[:, c0 + i:c0 + i + 1], gates[:, c0 + 3 + i:c0 + 4 + i])
                                   for i in range(3))
            o_ref[:, cols] = g_cmp * ocmp_ref[:, cols] + g_sel * osel_ref[:, cols] + g_win * o_win


def _win_combine(q, kv, gates, o_cmp, o_sel, n_groups, tq):
    b, t, width = q.shape
    n_gp = n_groups // 2
    n_heads = width // HEAD_DIM
    assert t % tq == 0 and t >= WIN + tq
    slopes = jnp.broadcast_to(_alibi_slopes(n_heads)[:, None], (n_heads, LANES))
    qblk = pl.BlockSpec((None, tq, 2 * 4 * HEAD_DIM), lambda i, j, k: (i, k, j))
    return pl.pallas_call(
        functools.partial(_win_combine_kernel, seq=t),
        grid=(b, n_gp, t // tq),
        in_specs=[_resident(slopes.shape), qblk,
                  pl.BlockSpec((None, tq, LANES), lambda i, j, k: (i, k, j)),
                  qblk, qblk,
                  pl.BlockSpec((None, t, LANES), lambda i, j, k: (i, 0, j)),
                  pl.BlockSpec((None, t, LANES), lambda i, j, k: (i, 0, n_gp + j))],
        out_specs=qblk,
        out_shape=jax.ShapeDtypeStruct((b, t, width), F32),
        scratch_shapes=[pltpu.VMEM((t, LANES), BF16)] * 6,
        compiler_params=_cparams(3),
        name="nsa_win_combine",
    )(slopes, q, gates, o_cmp, o_sel, kv, kv)


def _win_combine2_kernel(qtab_ref, ktab_ref, q_ref, gate_ref, ocmp_ref, osel_ref, k_ref, v_ref, o_ref,
                         ka0, ka1, v0, v1, *, seq):
    gp, qi = pl.program_id(1), pl.program_id(2)
    tq = q_ref.shape[0]
    tk = WIN + tq
    lo = _lane_lo()
    staged = ((ka0, v0), (ka1, v1))

    @pl.when(qi == 0)
    def _():
        for gl, (ka, vs) in enumerate(staged):
            ka[...], vs[...] = _stage_kv_aug(k_ref, v_ref, ktab_ref, gl)

    k0 = jnp.clip(qi * tq - WIN, 0, seq - tk)
    rows = pl.ds(pl.multiple_of(k0, tq), tk)
    qp = qi * tq + lax.broadcasted_iota(jnp.int32, (4 * tq, 1), 0) % tq
    dist = qp - (k0 + lax.broadcasted_iota(jnp.int32, (1, tk), 1))
    ok = (dist >= 0) & (dist <= WIN)
    gates = 1.0 / (1.0 + jnp.exp(-gate_ref[...]))
    for gl, (ka, vs) in enumerate(staged):
        q_all = _stack_heads(q_ref, gl, qtab_ref, (2 * gp + gl) * 4, None)
        s = jnp.where(ok, _nt_dot(q_all, ka[rows, :]), NEG)
        m = jnp.max(s, axis=-1, keepdims=True)
        p = jnp.where(ok, jnp.exp(s - m), 0.0)
        l = jnp.maximum(jnp.sum(p, axis=-1, keepdims=True), 1e-30)
        out = jnp.dot(p.astype(BF16), vs[rows, :], preferred_element_type=F32) / l
        for half in range(2):
            cols = slice(gl * 4 * HEAD_DIM + half * LANES, gl * 4 * HEAD_DIM + (half + 1) * LANES)
            o_win = jnp.where(lo, out[2 * half * tq:(2 * half + 1) * tq], out[(2 * half + 1) * tq:(2 * half + 2) * tq])
            c0 = (gl * 4 + 2 * half) * 3
            g_cmp, g_sel, g_win = (jnp.where(lo, gates[:, c0 + i:c0 + i + 1], gates[:, c0 + 3 + i:c0 + 4 + i])
                                   for i in range(3))
            o_ref[:, cols] = g_cmp * ocmp_ref[:, cols] + g_sel * osel_ref[:, cols] + g_win * o_win


def _win_combine2(q, kv, gates, o_cmp, o_sel, n_groups, tq):
    b, t, width = q.shape
    n_gp = n_groups // 2
    n_heads = width // HEAD_DIM
    assert t % tq == 0 and t >= WIN + tq
    qtab, ktab = _alibi_tables(n_heads, jnp.arange(t))
    qblk = pl.BlockSpec((None, tq, 2 * 4 * HEAD_DIM), lambda i, j, k: (i, k, j))
    return pl.pallas_call(
        functools.partial(_win_combine2_kernel, seq=t),
        grid=(b, n_gp, t // tq),
        in_specs=[_resident(qtab.shape), _resident(ktab.shape), qblk,
                  pl.BlockSpec((None, tq, LANES), lambda i, j, k: (i, k, j)),
                  qblk, qblk,
                  pl.BlockSpec((None, t, LANES), lambda i, j, k: (i, 0, j)),
                  pl.BlockSpec((None, t, LANES), lambda i, j, k: (i, 0, n_gp + j))],
        out_specs=qblk,
        out_shape=jax.ShapeDtypeStruct((b, t, width), F32),
        scratch_shapes=[pltpu.VMEM((t, LANES), BF16)] * 4,
        compiler_params=_cparams(3),
        name="nsa_win_combine",
    )(qtab, ktab, q, gates, o_cmp, o_sel, kv, kv)


def _nsa_prompt(q, kv_cmp, kv_sel, kv_win, gates, wbd, pe, n_groups):
    t = q.shape[1]
    kvc = _compress(kv_cmp, wbd, pe)
    n_c = t // CMP_STRIDE - CMP_LEN // CMP_STRIDE + 1
    n_sel = -(-t // SEL_BLOCK)
    o_cmp, sel = _cmp_select(q, kvc, n_c, n_sel, n_groups, LANES)
    o_sel = _sel_attn2(q, kv_sel, sel, n_groups, LANES, 8 * LANES)
    return _win_combine2(q, kv_win, gates, o_cmp, o_sel, n_groups, LANES)


def _head_block_diag(x, n_rep, lanes_per_head, rows_per_head):
    tiled = jnp.concatenate([x] * n_rep, axis=0)
    row_h = lax.broadcasted_iota(jnp.int32, tiled.shape, 0) // rows_per_head
    lane_h = lax.broadcasted_iota(jnp.int32, tiled.shape, 1) // lanes_per_head
    return jnp.where(row_h == lane_h, tiled, 0.0)


def _head_diag_rows(o, n_rep, lanes_per_head, rows_per_head):
    row_h = lax.broadcasted_iota(jnp.int32, o.shape, 0) // rows_per_head
    lane_h = lax.broadcasted_iota(jnp.int32, o.shape, 1) // lanes_per_head
    kept = jnp.where(row_h == lane_h, o, 0.0)
    return jnp.sum(kept.reshape(n_rep, rows_per_head, o.shape[1]), axis=0)


def _dil_sample_kernel(slope_ref, q_ref, kvn_ref, cache_ref, o_ref, ctx_ref, k_scr, v_scr):
    ts, width = q_ref.shape
    past = cache_ref.shape[0]
    n_heads = width // HEAD_DIM
    rows = n_heads * ts
    pad_rows = k_scr.shape[0] - past
    new_k = jnp.concatenate([kvn_ref[:, :width], jnp.zeros((pad_rows - ts, width), F32)], axis=0)
    new_v = jnp.concatenate([kvn_ref[:, width:], jnp.zeros((pad_rows - ts, width), F32)], axis=0)
    k_scr[0:past, :] = cache_ref[:, :width].astype(BF16)
    v_scr[0:past, :] = cache_ref[:, width:].astype(BF16)
    k_scr[past:, :] = new_k.astype(BF16)
    v_scr[past:, :] = new_v.astype(BF16)
    qbd = _head_block_diag(q_ref[...] * (HEAD_DIM ** -0.5), n_heads, HEAD_DIM, ts).astype(BF16)
    s = _nt_dot(qbd, k_scr[...])
    t_row = lax.broadcasted_iota(jnp.int32, (rows, 1), 0) % ts
    dist = (past + t_row) - lax.broadcasted_iota(jnp.int32, (1, k_scr.shape[0]), 1)
    count = jnp.zeros(s.shape, F32)
    for window, dil in DIL_PAIRS:
        assert dil & (dil - 1) == 0
        count = count + ((dist >= 0) & (dist <= window) & ((dist & (dil - 1)) == 0)).astype(F32)
    s = jnp.where(count > 0, s - slope_ref[:, 0:1] * dist.astype(F32), NEG)
    m = jnp.max(s, axis=-1, keepdims=True)
    p = count * jnp.exp(s - m)
    l = jnp.sum(p, axis=-1, keepdims=True)
    o = jnp.dot(p.astype(BF16), v_scr[...], preferred_element_type=F32) / l
    o_ref[...] = _head_diag_rows(o, n_heads, HEAD_DIM, ts)
    ctx_ref[0:past - ts, :] = cache_ref[ts:past, :]
    ctx_ref[past - ts:past, :] = kvn_ref[...]


def _dil_count(dist):
    count = jnp.zeros(dist.shape, F32)
    for window, dil in DIL_PAIRS:
        assert dil & (dil - 1) == 0
        count = count + ((dist >= 0) & (dist <= window) & ((dist & (dil - 1)) == 0)).astype(F32)
    return count


def _dil_sample2_kernel(slope_ref, q_ref, kvn_ref, cache_ref, o_ref, ctx_ref):
    ts, width = q_ref.shape
    past = cache_ref.shape[-1]
    n_heads = width // HEAD_DIM
    rows = n_heads * ts
    slope = slope_ref[:, 0:1]
    k_t = cache_ref[0].reshape(width, past)
    v_t = cache_ref[1].reshape(width, past)
    new = jnp.concatenate([kvn_ref[...], jnp.zeros((LANES - ts, 2 * width), F32)], axis=0)
    qbd = _head_block_diag(q_ref[...] * (HEAD_DIM ** -0.5), n_heads, HEAD_DIM, ts).astype(BF16)
    t_row = lax.broadcasted_iota(jnp.int32, (rows, 1), 0) % ts
    d_old = (past + t_row) - lax.broadcasted_iota(jnp.int32, (1, past), 1)
    j_new = lax.broadcasted_iota(jnp.int32, (1, LANES), 1)
    d_new = jnp.where(j_new < ts, t_row - j_new, -1)
    c_old, c_new = _dil_count(d_old), _dil_count(d_new)
    s_old = jnp.dot(qbd, k_t.astype(BF16), preferred_element_type=F32)
    s_new = _nt_dot(qbd, new[:, 0:width].astype(BF16))
    s_old = jnp.where(c_old > 0, s_old - slope * d_old.astype(F32), NEG)
    s_new = jnp.where(c_new > 0, s_new - slope * d_new.astype(F32), NEG)
    m = jnp.maximum(jnp.max(s_old, axis=-1, keepdims=True), jnp.max(s_new, axis=-1, keepdims=True))
    p_old = c_old * jnp.exp(s_old - m)
    p_new = c_new * jnp.exp(s_new - m)
    l = jnp.sum(p_old, axis=-1, keepdims=True) + jnp.sum(p_new, axis=-1, keepdims=True)
    o = _nt_dot(p_old.astype(BF16), v_t.astype(BF16)) + jnp.dot(p_new.astype(BF16), new[:, width:].astype(BF16),
                                                               preferred_element_type=F32)
    o_ref[...] = _head_diag_rows(o / l, n_heads, HEAD_DIM, ts)
    both = cache_ref[...].reshape(2 * width, past)
    rolled = pltpu.roll(both, past - ts, axis=1)
    new_t = pltpu.roll(new.T, LANES - ts, axis=1)
    ctx_ref[:, 0:past - LANES] = rolled[:, 0:past - LANES]
    ctx_ref[:, past - LANES:past] = jnp.where(j_new >= LANES - ts, new_t, rolled[:, past - LANES:past])


def _dil_sample2(q, kv_new, cache_t, n_heads):
    b, ts, width = q.shape
    past = cache_t.shape[-1]
    assert ts % 8 == 0 and past % LANES == 0
    slopes = jnp.broadcast_to(jnp.repeat(_alibi_slopes(n_heads), ts)[:, None], (n_heads * ts, LANES))
    per_b = lambda rows, w: pl.BlockSpec((None, rows, w), lambda i: (i, 0, 0))
    return pl.pallas_call(
        _dil_sample2_kernel,
        grid=(b,),
        in_specs=[_resident(slopes.shape), per_b(ts, width), per_b(ts, 2 * width),
                  pl.BlockSpec((None,) + cache_t.shape[1:], lambda i: (i, 0, 0, 0, 0))],
        out_specs=[per_b(ts, width), per_b(2 * width, past)],
        out_shape=[jax.ShapeDtypeStruct((b, ts, width), F32), jax.ShapeDtypeStruct((b, 2 * width, past), F32)],
        compiler_params=_cparams(1),
        name="dil_sample",
    )(slopes, q, kv_new, cache_t)


def _dil_sample(q, kv_new, cache, n_heads):
    b, ts, width = q.shape
    past = cache.shape[1]
    assert ts % 8 == 0 and past % LANES == 0
    slopes = jnp.broadcast_to(jnp.repeat(_alibi_slopes(n_heads), ts)[:, None], (n_heads * ts, LANES))
    per_b = lambda rows, w: pl.BlockSpec((None, rows, w), lambda i: (i, 0, 0))
    return pl.pallas_call(
        _dil_sample_kernel,
        grid=(b,),
        in_specs=[_resident(slopes.shape), per_b(ts, width), per_b(ts, 2 * width), per_b(past, 2 * width)],
        out_specs=[per_b(ts, width), per_b(past, 2 * width)],
        out_shape=[jax.ShapeDtypeStruct((b, ts, width), F32), jax.ShapeDtypeStruct(cache.shape, F32)],
        scratch_shapes=[pltpu.VMEM((past + LANES, width), BF16)] * 2,
        compiler_params=_cparams(1),
        name="dil_sample",
    )(slopes, q, kv_new, cache)


def _compress_pool_kernel(pool_ref, w_ref, pe_ref, o_ref, *slabs):
    pages, _, n_groups, hd, page_rows = pool_ref.shape
    pairs = n_groups * hd // LANES
    eye = (lax.broadcasted_iota(jnp.int32, (page_rows, page_rows), 0)
           == lax.broadcasted_iota(jnp.int32, (page_rows, page_rows), 1)).astype(BF16)

    batch = 8

    def to_rows(pb, carry):
        for i in range(batch):
            p = pb * batch + i
            for c in range(2):
                for gp in range(pairs):
                    tile_t = pool_ref[p, c, pl.ds(gp * (LANES // hd), LANES // hd)].reshape(LANES, page_rows)
                    slabs[c * pairs + gp][pl.ds(pl.multiple_of(p * page_rows, page_rows), page_rows), :] = _nt_dot(
                        eye, tile_t.astype(BF16))
        return carry

    assert pages % batch == 0
    lax.fori_loop(0, pages // batch, to_rows, 0)
    _compress_kernel(*slabs, w_ref, pe_ref, o_ref, n_ch=pages * page_rows // CMP_STRIDE, merged=False)


def _compress_pool(pool, wbd, pe, pages_per_step):
    n_pages, _, n_groups, hd, page_rows = pool.shape
    width = 2 * n_groups * hd
    assert n_pages % pages_per_step == 0 and page_rows % (8 * CMP_STRIDE) == 0 and page_rows == LANES
    n_ch = pages_per_step * page_rows // CMP_STRIDE
    return pl.pallas_call(
        _compress_pool_kernel,
        grid=(n_pages // pages_per_step,),
        in_specs=[pl.BlockSpec((pages_per_step,) + pool.shape[1:], lambda i: (i, 0, 0, 0, 0)),
                  _resident(wbd.shape), _resident(pe.shape)],
        out_specs=pl.BlockSpec((n_ch, 2 * width), lambda i: (i, 0)),
        out_shape=jax.ShapeDtypeStruct((n_pages * page_rows // CMP_STRIDE, 2 * width), F32),
        scratch_shapes=[pltpu.VMEM((pages_per_step * page_rows, LANES), F32)] * (width // LANES),
        compiler_params=_cparams(1),
        name="nsa_compress_pool",
    )(pool, wbd, pe)


def _sample_query(q_ref, ts, n_groups):
    gw = n_groups * HEAD_DIM
    n_rep = q_ref.shape[1] // gw
    return jnp.concatenate(
        [_head_block_diag(q_ref[:, r * gw:(r + 1) * gw] * (HEAD_DIM ** -0.5), n_groups, HEAD_DIM, ts) for r in range(n_rep)],
        axis=0).astype(BF16)


def _sample_rows_out(o, ts, n_groups):
    per_r = n_groups * ts
    return [_head_diag_rows(o[r * per_r:(r + 1) * per_r, :], n_groups, HEAD_DIM, ts) for r in range(o.shape[0] // per_r)]


def _nsa_sample_cmp_kernel(pt_ref, slope_ref, c2s_ref, q_ref, kvn_ref, cwin_ref, *rest, pages, past, n_c, n_sel):
    del pt_ref
    ab_refs = rest[:pages]
    xcmp_ref, xwin_ref, sel_ref, winout_ref, ab_scr, kwin_scr, vwin_scr = rest[pages:]
    j = pl.program_id(1)
    ts = q_ref.shape[0]
    n_groups = kvn_ref.shape[1] // (2 * HEAD_DIM)
    gw = n_groups * HEAD_DIM
    for k in range(pages):
        ab_scr[pl.ds(pl.multiple_of((j * pages + k) * 8, 8), 8), :] = ab_refs[k][...]

    @pl.when(j == pl.num_programs(1) - 1)
    def _():
        n_ch = ab_scr.shape[0]
        n_rows = slope_ref.shape[0]
        slope = slope_ref[:, 0:1]
        qbd = _sample_query(q_ref, ts, n_groups)
        t_row = lax.broadcasted_iota(jnp.int32, (n_rows, 1), 0) % ts
        qp = past + t_row
        kvc = ab_scr[:, 0:2 * gw] + pltpu.roll(ab_scr[:, 2 * gw:4 * gw], n_ch - 1, axis=0)
        n_idx = lax.broadcasted_iota(jnp.int32, (1, n_ch), 1)
        cmp_end = n_idx * CMP_STRIDE + (CMP_LEN - 1)
        valid = (cmp_end <= qp) & (n_idx < n_c)
        s = jnp.where(valid, _nt_dot(qbd, kvc[:, 0:gw].astype(BF16)) - slope * (qp - cmp_end).astype(F32), NEG)
        m = jnp.max(s, axis=-1, keepdims=True)
        p = jnp.where(valid, jnp.exp(s - m), 0.0)
        p = (p / jnp.maximum(jnp.sum(p, axis=-1, keepdims=True), 1e-30)).astype(BF16)
        for r, x in enumerate(_sample_rows_out(jnp.dot(p, kvc[:, gw:2 * gw].astype(BF16), preferred_element_type=F32), ts, n_groups)):
            xcmp_ref[r * ts:(r + 1) * ts, :] = x
        imp_rows = jnp.dot(p, c2s_ref[...], preferred_element_type=F32)
        gt = n_groups * ts
        imp = jnp.sum(imp_rows.reshape(n_rows // gt, gt, imp_rows.shape[1]), axis=0)
        blk = lax.broadcasted_iota(jnp.int32, (1, imp.shape[1]), 1)
        cb = (past + lax.broadcasted_iota(jnp.int32, (gt, 1), 0) % ts) // SEL_BLOCK
        forced = (blk == 0) | (blk == cb) | (blk == cb - 1)
        score = jnp.where((blk <= cb) & (blk < n_sel), imp + FORCE_BONUS * forced.astype(F32), NEG)
        removed = -3e38
        cur = jnp.where(blk < n_sel, score, removed)
        chosen = jnp.zeros(score.shape, F32)
        for _ in range(SEL_TOP):
            top = jnp.max(cur, axis=-1, keepdims=True)
            first = jnp.min(jnp.where(cur == top, blk, imp.shape[1]), axis=-1, keepdims=True)
            pick = blk == first
            chosen = jnp.where(pick & (top > 0.5 * NEG), 1.0, chosen)
            cur = jnp.where(pick, removed, cur)
        n_steps = sel_ref.shape[0]
        per_step = (past // SEL_BLOCK) // n_steps
        lane = lax.broadcasted_iota(jnp.int32, (1, LANES), 1)
        for st in range(n_steps):
            shifted = chosen if st == 0 else pltpu.roll(chosen, chosen.shape[1] - st * per_step, axis=1)
            n_here = per_step + (n_sel - n_steps * per_step if st == n_steps - 1 else 0)
            piece = jnp.where(lane < n_here, shifted[:, 0:LANES], 0.0)
            sel_ref[st] = jnp.concatenate([piece] * (n_rows // gt), axis=0).astype(BF16)
        w_past = cwin_ref.shape[0]
        pad = kwin_scr.shape[0] - w_past
        kwin_scr[0:w_past, :] = cwin_ref[:, 0:gw].astype(BF16)
        vwin_scr[0:w_past, :] = cwin_ref[:, gw:2 * gw].astype(BF16)
        zeros = jnp.zeros((pad - ts, gw), F32)
        kwin_scr[w_past:, :] = jnp.concatenate([kvn_ref[:, 0:gw], zeros], axis=0).astype(BF16)
        vwin_scr[w_past:, :] = jnp.concatenate([kvn_ref[:, gw:2 * gw], zeros], axis=0).astype(BF16)
        dist = (w_past + t_row) - lax.broadcasted_iota(jnp.int32, (1, kwin_scr.shape[0]), 1)
        ok = (dist >= 0) & (dist <= WIN)
        s = jnp.where(ok, _nt_dot(qbd, kwin_scr[...]) - slope * dist.astype(F32), NEG)
        m = jnp.max(s, axis=-1, keepdims=True)
        p = jnp.where(ok, jnp.exp(s - m), 0.0)
        l = jnp.maximum(jnp.sum(p, axis=-1, keepdims=True), 1e-30)
        o = jnp.dot(p.astype(BF16), vwin_scr[...], preferred_element_type=F32) / l
        for r, x in enumerate(_sample_rows_out(o, ts, n_groups)):
            xwin_ref[r * ts:(r + 1) * ts, :] = x
        winout_ref[0:w_past - ts, :] = cwin_ref[ts:w_past, :]
        winout_ref[w_past - ts:w_past, :] = kvn_ref[...]


def _sample_slopes(n_groups, n_rep, ts):
    slopes = _alibi_slopes(n_groups * n_rep).reshape(n_groups, n_rep).T
    return jnp.broadcast_to(jnp.repeat(slopes.reshape(-1), ts)[:, None], (n_rep * n_groups * ts, LANES))


def _nsa_sample_cmp(q, kv_win_new, cache_win, ab_pool, page_table, n_groups, sel_steps, pages):
    b, ts, qw = q.shape
    n_rep = qw // HEAD_DIM // n_groups
    gw = n_groups * HEAD_DIM
    n_pages = page_table.shape[1]
    chunks_per_page = ab_pool.shape[1]
    past = n_pages * chunks_per_page * CMP_STRIDE
    n_ch = (past + ts) // CMP_STRIDE
    assert n_ch == n_pages * chunks_per_page and n_pages % pages == 0
    n_c = n_ch - CMP_LEN // CMP_STRIDE + 1
    n_sel = -(-(past + ts) // SEL_BLOCK)
    assert n_sel <= 2 * LANES and n_sel - past // SEL_BLOCK + past // SEL_BLOCK // sel_steps <= LANES
    n_rows = n_rep * n_groups * ts
    slopes = _sample_slopes(n_groups, n_rep, ts)
    cs = jnp.arange(n_ch)[:, None] * CMP_STRIDE
    ss = jnp.arange(2 * LANES)[None, :] * SEL_BLOCK
    ov = jnp.maximum(jnp.minimum(cs + CMP_LEN, ss + SEL_BLOCK) - jnp.maximum(cs, ss), 0).astype(F32) / CMP_STRIDE
    c2s = jnp.where((jnp.arange(n_ch)[:, None] < n_c) & (jnp.arange(2 * LANES)[None, :] < n_sel), ov, 0.0).astype(BF16)
    w_past = cache_win.shape[1]
    per_b = lambda rows, w: pl.BlockSpec((None, rows, w), lambda i, j, pt: (i, 0, 0))
    page_spec = lambda k: pl.BlockSpec((None, chunks_per_page, ab_pool.shape[2]),
                                       lambda i, j, pt: (pt[i, j * pages + k], 0, 0))
    const = lambda shape: pl.BlockSpec(shape, lambda i, j, pt: (0,) * len(shape), pipeline_mode=pl.Buffered(1))
    return pl.pallas_call(
        functools.partial(_nsa_sample_cmp_kernel, pages=pages, past=past, n_c=n_c, n_sel=n_sel),
        grid_spec=pltpu.PrefetchScalarGridSpec(
            num_scalar_prefetch=1,
            grid=(b, n_pages // pages),
            in_specs=[const(slopes.shape), const(c2s.shape), per_b(ts, qw), per_b(ts, 2 * gw), per_b(w_past, 2 * gw)]
                     + [page_spec(k) for k in range(pages)],
            out_specs=[per_b(n_rep * ts, gw), per_b(n_rep * ts, gw),
                       pl.BlockSpec((None, sel_steps, n_rows, LANES), lambda i, j, pt: (i, 0, 0, 0)),
                       per_b(w_past, 2 * gw)],
            scratch_shapes=[pltpu.VMEM((n_ch, ab_pool.shape[2]), F32),
                            pltpu.VMEM((w_past + LANES, gw), BF16), pltpu.VMEM((w_past + LANES, gw), BF16)]),
        out_shape=[jax.ShapeDtypeStruct((b, n_rep * ts, gw), F32), jax.ShapeDtypeStruct((b, n_rep * ts, gw), F32),
                   jax.ShapeDtypeStruct((b, sel_steps, n_rows, LANES), BF16),
                   jax.ShapeDtypeStruct(cache_win.shape, F32)],
        compiler_params=_cparams(2),
        name="nsa_sample_cmp",
    )(page_table, slopes, c2s, q, kv_win_new, cache_win, *([ab_pool] * pages))


def _nsa_sample_sel_kernel(pt_ref, slope_ref, exp_ref, q_ref, kvn_ref, sel_ref, xcmp_ref, xwin_ref, gate_ref, *rest,
                           pages, past):
    del pt_ref
    page_refs = rest[:pages]
    o_ref, m_scr, l_scr, acc_scr, new_scr = rest[pages:]
    j = pl.program_id(1)
    ts = q_ref.shape[0]
    n_groups = kvn_ref.shape[1] // (2 * HEAD_DIM)
    gw = n_groups * HEAD_DIM
    n_rows = slope_ref.shape[0]
    page_rows = page_refs[0].shape[-1]
    tk = pages * page_rows
    slope = slope_ref[:, 0:1]
    qbd = _sample_query(q_ref, ts, n_groups)
    t_row = lax.broadcasted_iota(jnp.int32, (n_rows, 1), 0) % ts

    @pl.when(j == 0)
    def _():
        m_scr[...] = jnp.full(m_scr.shape, NEG, F32)
        l_scr[...] = jnp.zeros(l_scr.shape, F32)
        acc_scr[...] = jnp.zeros(acc_scr.shape, F32)

    def update(k, v, ok, dist, transposed):
        qk = jnp.dot(qbd, k, preferred_element_type=F32) if transposed else _nt_dot(qbd, k)
        s = jnp.where(ok, qk - slope * dist.astype(F32), NEG)
        m_old = m_scr[:, 0:1]
        m_new = jnp.maximum(m_old, jnp.max(s, axis=-1, keepdims=True))
        alpha = jnp.exp(m_old - m_new)
        p = jnp.where(ok, jnp.exp(s - m_new), 0.0)
        pv = _nt_dot(p.astype(BF16), v) if transposed else jnp.dot(p.astype(BF16), v, preferred_element_type=F32)
        l_scr[...] = jnp.broadcast_to(alpha * l_scr[:, 0:1] + jnp.sum(p, axis=-1, keepdims=True), l_scr.shape)
        acc_scr[...] = alpha * acc_scr[...] + pv
        m_scr[...] = jnp.broadcast_to(m_new, m_scr.shape)

    sel = sel_ref[j]
    k = jnp.concatenate([ref[0].reshape(gw, page_rows) for ref in page_refs], axis=1).astype(BF16)
    v = jnp.concatenate([ref[1].reshape(gw, page_rows) for ref in page_refs], axis=1).astype(BF16)
    picked = _nt_dot(sel, exp_ref[0:tk, :]) > 0.5
    kpos = j * tk + lax.broadcasted_iota(jnp.int32, (1, tk), 1)
    update(k, v, picked, (past + t_row) - kpos, True)

    @pl.when(j == pl.num_programs(1) - 1)
    def _():
        pad = new_scr.shape[0]
        new_scr[...] = jnp.concatenate([kvn_ref[...], jnp.zeros((pad - ts, 2 * gw), F32)], axis=0).astype(BF16)
        i_new = lax.broadcasted_iota(jnp.int32, (1, pad), 1)
        ok = (_nt_dot(sel, exp_ref[tk:tk + pad, :]) > 0.5) & (i_new <= t_row) & (i_new < ts)
        update(new_scr[:, 0:gw], new_scr[:, gw:2 * gw], ok, t_row - i_new, False)
        o = acc_scr[...] / jnp.maximum(l_scr[:, 0:1], 1e-30)
        n_rep = n_rows // (n_groups * ts)
        for r, x_sel in enumerate(_sample_rows_out(o, ts, n_groups)):
            g_cmp, g_sel, g_win = (1.0 / (1.0 + jnp.exp(-gate_ref[:, (i * n_rep + r) * gw:(i * n_rep + r + 1) * gw]))
                                   for i in range(3))
            rows = slice(r * ts, (r + 1) * ts)
            o_ref[:, r * gw:(r + 1) * gw] = g_cmp * xcmp_ref[rows, :] + g_sel * x_sel + g_win * xwin_ref[rows, :]


def _nsa_sample_sel(q, kv_sel_new, sel, x_cmp, x_win, gates, pool, page_table, n_groups, pages):
    b, ts, qw = q.shape
    n_rep = qw // HEAD_DIM // n_groups
    gw = n_groups * HEAD_DIM
    n_pages = page_table.shape[1]
    page_rows = pool.shape[-1]
    past = n_pages * page_rows
    n_steps = n_pages // pages
    tk = pages * page_rows
    n_rows = n_rep * n_groups * ts
    assert sel.shape[1] == n_steps and tk % SEL_BLOCK == 0 and tk // SEL_BLOCK < LANES
    slopes = _sample_slopes(n_groups, n_rep, ts)
    blk_of = jnp.concatenate([jnp.arange(tk) // SEL_BLOCK, jnp.full((LANES,), tk // SEL_BLOCK)])
    expand = (blk_of[:, None] == jnp.arange(LANES)[None, :]).astype(BF16)
    per_b = lambda rows, w: pl.BlockSpec((None, rows, w), lambda i, j, pt: (i, 0, 0))
    page_spec = lambda k: pl.BlockSpec((None,) + pool.shape[1:], lambda i, j, pt: (pt[i, j * pages + k], 0, 0, 0, 0))
    const = lambda shape: pl.BlockSpec(shape, lambda i, j, pt: (0,) * len(shape), pipeline_mode=pl.Buffered(1))
    return pl.pallas_call(
        functools.partial(_nsa_sample_sel_kernel, pages=pages, past=past),
        grid_spec=pltpu.PrefetchScalarGridSpec(
            num_scalar_prefetch=1,
            grid=(b, n_steps),
            in_specs=[const(slopes.shape), const(expand.shape), per_b(ts, qw), per_b(ts, 2 * gw),
                      pl.BlockSpec((None, n_steps, n_rows, LANES), lambda i, j, pt: (i, 0, 0, 0)),
                      per_b(n_rep * ts, gw), per_b(n_rep * ts, gw), per_b(ts, 3 * qw)]
                     + [page_spec(k) for k in range(pages)],
            out_specs=per_b(ts, qw),
            scratch_shapes=[pltpu.VMEM((n_rows, LANES), F32), pltpu.VMEM((n_rows, LANES), F32),
                            pltpu.VMEM((n_rows, gw), F32), pltpu.VMEM((LANES, 2 * gw), BF16)]),
        out_shape=jax.ShapeDtypeStruct((b, ts, qw), F32),
        compiler_params=_cparams(2),
        name="nsa_sample_sel",
    )(page_table, slopes, expand, q, kv_sel_new, sel, x_cmp, x_win, gates, *([pool] * pages))


def _masked_softmax(s, mask):
    s = jnp.where(mask, s, NEG)
    m = jnp.max(s, axis=-1, keepdims=True)
    p = jnp.where(mask, jnp.exp(s - m), 0.0)
    return p / jnp.maximum(jnp.sum(p, axis=-1, keepdims=True), 1e-30)


def _dil_attn_sample(q, ctx, n_heads):
    t = q.shape[1]
    slopes = _alibi_slopes(n_heads)
    pos = ctx.shape[1] - t + jnp.arange(t)
    lses, outs = [], []
    for window, dil in DIL_PAIRS:
        dist = jnp.arange(window // dil + 1) * dil
        idx = pos[:, None] - dist[None, :]
        kvg = jnp.take(ctx, jnp.maximum(idx, 0), axis=1)
        s = jnp.einsum('bthd,btkhd->bhtk', q, kvg[:, :, :, 0], preferred_element_type=F32) * HEAD_DIM ** -0.5
        s = s - slopes[:, None, None] * dist.astype(F32)
        s = jnp.where(idx >= 0, s, NEG)
        m = jnp.max(s, axis=-1, keepdims=True)
        p = jnp.exp(s - m)
        l = jnp.sum(p, axis=-1, keepdims=True)
        outs.append(jnp.einsum('bhtk,btkhd->bthd', p / l, kvg[:, :, :, 1]))
        lses.append(jnp.swapaxes((m + jnp.log(l))[..., 0], 1, 2))
    wts = jax.nn.softmax(jnp.stack(lses), axis=0)[..., None]
    return jnp.sum(wts * jnp.stack(outs), axis=0)


def _nsa_sample(q, kv_cmp, kv_sel, kv_win, gate_logits, past_cmp, past_sel, win_buf, w_cmp, pe_cmp):
    b, t, g, r, _ = q.shape
    scale = HEAD_DIM ** -0.5
    slopes = _alibi_slopes(g * r).reshape(g, r)
    gates = jax.nn.sigmoid(gate_logits)
    ctx_cmp = jnp.concatenate([past_cmp, kv_cmp], axis=1)
    ctx_sel = jnp.concatenate([past_sel, kv_sel], axis=1)
    ctx_win = jnp.concatenate([win_buf, kv_win], axis=1)
    tc = ctx_cmp.shape[1]
    qp = (tc - t) + jnp.arange(t)
    n_r = CMP_LEN // CMP_STRIDE
    n_ch = tc // CMP_STRIDE
    n_c = n_ch - n_r + 1
    chunks = ctx_cmp[:, :n_ch * CMP_STRIDE].reshape((b, n_ch, CMP_STRIDE) + ctx_cmp.shape[2:])
    w = w_cmp.reshape(n_r, CMP_STRIDE, 2, HEAD_DIM, HEAD_DIM)
    kvc = jnp.einsum('lcd,lcde->ce', pe_cmp, w_cmp)[None, None, :, None, :]
    for i in range(n_r):
        kvc = kvc + jnp.einsum('bnscgd,scde->bncge', chunks[:, i:i + n_c], w[i])
    cmp_end = jnp.arange(n_c) * CMP_STRIDE + CMP_LEN - 1
    kc, vc = kvc[:, :, 0], kvc[:, :, 1]
    n_sel = -(-tc // SEL_BLOCK)
    n_top = min(SEL_TOP, n_sel)
    sel_blocks = jnp.pad(ctx_sel, ((0, 0), (0, n_sel * SEL_BLOCK - tc), (0, 0), (0, 0), (0, 0)))
    sel_blocks = sel_blocks.reshape(b, n_sel, SEL_BLOCK, 2, g, HEAD_DIM).transpose(0, 4, 1, 2, 3, 5)
    cs = jnp.arange(n_c)[:, None] * CMP_STRIDE
    ss = jnp.arange(n_sel)[None, :] * SEL_BLOCK
    cmp2sel = jnp.maximum(jnp.minimum(cs + CMP_LEN, ss + SEL_BLOCK) - jnp.maximum(cs, ss), 0).astype(F32) / CMP_STRIDE
    blk = jnp.arange(n_sel)
    s = jnp.einsum('btgrd,bngd->bgrtn', q, kc, preferred_element_type=F32) * scale
    s = s - slopes[:, :, None, None] * (qp[:, None] - cmp_end[None, :]).astype(F32)
    p = _masked_softmax(s, cmp_end[None, :] <= qp[:, None])
    o_cmp = jnp.einsum('bgrtn,bngd->btgrd', p, vc)
    imp = jnp.einsum('bgrtn,nj->bgtj', p, cmp2sel)
    cb = qp // SEL_BLOCK
    blk_ok = blk[None, :] <= cb[:, None]
    forced = (blk[None, :] == 0) | (blk[None, :] == cb[:, None]) | (blk[None, :] == cb[:, None] - 1)
    score = jnp.where(blk_ok, imp + FORCE_BONUS * forced.astype(F32), NEG)
    top_s, top_i = lax.top_k(score, n_top)
    picked = jnp.sum((top_i[..., None] == blk) & (top_s > 0.5 * NEG)[..., None], axis=-2) > 0
    kpos = jnp.arange(tc)
    ok = picked[..., kpos // SEL_BLOCK] & (kpos <= qp[:, None])
    s2 = jnp.einsum('btgrd,bkgd->bgrtk', q, ctx_sel[:, :, 0], preferred_element_type=F32) * scale
    s2 = s2 - slopes[None, :, :, None, None] * (qp[:, None] - kpos).astype(F32)
    p2 = _masked_softmax(s2, ok[:, :, None])
    o_sel = jnp.einsum('bgrtk,bkgd->btgrd', p2, ctx_sel[:, :, 1])
    tw = ctx_win.shape[1]
    dw = (tw - t + jnp.arange(t))[:, None] - jnp.arange(tw)[None, :]
    okw = (dw >= 0) & (dw <= WIN)
    s3 = jnp.einsum('btgrd,bsgd->bgrts', q, ctx_win[:, :, 0], preferred_element_type=F32) * scale
    s3 = s3 - slopes[:, :, None, None] * dw.astype(F32)
    p3 = _masked_softmax(s3, okw)
    o_win = jnp.einsum('bgrts,bsgd->btgrd', p3, ctx_win[:, :, 1])
    o = gates[..., 0:1] * o_cmp + gates[..., 1:2] * o_sel + gates[..., 2:3] * o_win
    return o, ctx_win[:, t:]


def _gate_columns(w_gate, n_groups):
    d = w_gate.shape[0]
    per_pair = w_gate.reshape(d, n_groups // 2, 2 * 4 * 3)
    return jnp.pad(per_pair, ((0, 0), (0, 0), (0, LANES - 2 * 4 * 3))).reshape(d, -1)


def kernel(x_prompt, x_sample, cache_dil_kv, state_conv, state_rnn, cache_win_kv, cache_cmp_kv, cache_sel_kv, page_table, norm_mix, norm_ffn, norm_out, w_in_ab, w_out_ab, conv_w, conv_b, gate_a_w, gate_a_b, gate_x_w, gate_x_b, lru_lambda, w_in_c, w_out_c, w_cmp, pe_cmp, ffn_w1, ffn_w3, ffn_w2):
    bp, t, d = x_prompt.shape
    bs, ts, _ = x_sample.shape
    depth = norm_mix.shape[0]
    bf = lambda z: z.astype(BF16)
    tm_p, tm_s = 512, bs * ts
    yp = x_prompt.reshape(bp * t, d)
    ys = x_sample.reshape(bs * ts, d)
    outs = {k: [] for k in ("dil_p", "dil_s", "conv_p", "conv_s", "rnn_p", "rnn_s",
                            "win_p", "win_s", "cmp_p", "cmp_s", "sel_p", "sel_s")}
    for layer in range(depth):
        li = layer // 2
        last = layer == depth - 1
        ffn = (norm_ffn[layer], bf(ffn_w1[layer]), bf(ffn_w3[layer]), bf(ffn_w2[layer]), norm_out if last else None)
        if layer % 2 == 0:
            rw = conv_w.shape[2]
            aw = (w_in_ab.shape[2] - 2 * rw) // 3
            n_heads = aw // HEAD_DIM
            w_in = bf(w_in_ab[li])
            splits = (aw, 2 * aw, rw, rw)
            wa, wx = bf(_block_diag(gate_a_w[li])), bf(_block_diag(gate_x_w[li]))
            lru = (conv_w[li], conv_b[li], wa, gate_a_b[li], wx, gate_x_b[li], lru_lambda[li])
            w_outs = bf(w_out_ab[li])
            q, kv, xr, gate = _norm_proj(yp, norm_mix[layer], w_in, splits, tm_p)
            o_att = _dil_attn(q.reshape(bp, t, aw), kv.reshape(bp, t, 2 * aw), n_heads)
            o_rnn, conv_new, h_new = _rglru(xr.reshape(bp, t, rw), gate.reshape(bp, t, rw),
                                            jnp.zeros((bp, CONV_W - 1, rw), F32), jnp.zeros((bp, rw), F32),
                                            *lru, bp, 256)
            yp = _out_ffn(yp, [o_att.reshape(bp * t, aw), o_rnn.reshape(bp * t, rw)], w_outs, *ffn, tm_p)
            keep = min(DIL_MAX, t)
            outs["dil_p"].append(kv.reshape(bp, t, 2, n_heads, HEAD_DIM)[:, t - keep:])
            outs["conv_p"].append(conv_new)
            outs["rnn_p"].append(h_new)
            q, kv, xr, gate = _norm_proj(ys, norm_mix[layer], w_in, splits, tm_s)
            cache = cache_dil_kv[li]
            o_att, ctx_new = _dil_sample2(q.reshape(bs, ts, aw), kv.reshape(bs, ts, 2 * aw),
                                          jnp.transpose(cache, (0, 2, 3, 4, 1)), n_heads)
            ctx_new = jnp.transpose(ctx_new.reshape(bs, 2, n_heads, HEAD_DIM, cache.shape[1]), (0, 4, 1, 2, 3))
            o_rnn, conv_new, h_new = _rglru(xr.reshape(bs, ts, rw), gate.reshape(bs, ts, rw),
                                            state_conv[li], state_rnn[li], *lru, 8, ts)
            ys = _out_ffn(ys, [o_att.reshape(bs * ts, aw), o_rnn.reshape(bs * ts, rw)], w_outs, *ffn, tm_s)
            outs["dil_s"].append(ctx_new.reshape(cache.shape))
            outs["conv_s"].append(conv_new)
            outs["rnn_s"].append(h_new)
        else:
            n_groups = cache_win_kv.shape[4]
            kvw = 2 * n_groups * HEAD_DIM
            qw = w_out_c.shape[1]
            rep = qw // HEAD_DIM // n_groups
            w_in = bf(jnp.concatenate([w_in_c[li][:, :qw + 3 * kvw],
                                       _gate_columns(w_in_c[li][:, qw + 3 * kvw:], n_groups)], axis=1))
            gw = n_groups // 2 * LANES
            splits = (qw, kvw, kvw, kvw, gw)
            wbd, pe = _compress_weights(w_cmp[li], pe_cmp[li], n_groups)
            w_outs = bf(w_out_c[li])
            kv5 = lambda z, n: z.reshape(n, -1, 2, n_groups, HEAD_DIM)
            q, kv_cmp, kv_sel, kv_win, gates = _norm_proj(yp, norm_mix[layer], w_in, splits, tm_p)
            o = _nsa_prompt(q.reshape(bp, t, qw), kv_cmp.reshape(bp, t, kvw), kv_sel.reshape(bp, t, kvw),
                            kv_win.reshape(bp, t, kvw), gates.reshape(bp, t, gw), wbd, pe, n_groups)
            yp = _out_ffn(yp, [o.reshape(bp * t, qw)], w_outs, *ffn, tm_p)
            keep = min(WIN, t)
            outs["win_p"].append(kv5(kv_win, bp)[:, t - keep:])
            outs["cmp_p"].append(kv5(kv_cmp, bp))
            outs["sel_p"].append(kv5(kv_sel, bp))
            wq = w_in_c[li][:, :qw].reshape(d, n_groups, rep, HEAD_DIM).transpose(0, 2, 1, 3).reshape(d, qw)
            wg = w_in_c[li][:, qw + 3 * kvw:].reshape(d, n_groups, rep, 3).transpose(0, 3, 2, 1)
            wg = jnp.broadcast_to(wg[..., None], wg.shape + (HEAD_DIM,)).reshape(d, 3 * qw)
            w_in_s = bf(jnp.concatenate([wq, w_in_c[li][:, qw:qw + 3 * kvw], wg], axis=1))
            w_out_s = bf(w_out_c[li].reshape(n_groups, rep, HEAD_DIM, d).transpose(1, 0, 2, 3).reshape(qw, d))
            q, kv_cmp, kv_sel, kv_win, gates = _norm_proj(ys, norm_mix[layer], w_in_s, (qw, kvw, kvw, kvw, 3 * qw), tm_s)
            n_phys, page_rows = cache_cmp_kv.shape[1], cache_cmp_kv.shape[2]
            rows_minor = lambda pool: jnp.transpose(pool, (0, 2, 3, 4, 1))
            ab_pool = _compress_pool(rows_minor(cache_cmp_kv[li]), wbd, pe, 32)
            ab_pool = ab_pool.reshape(n_phys, page_rows // CMP_STRIDE, 2 * kvw)
            cwin = cache_win_kv[li]
            x_cmp, x_win, sel, win_new = _nsa_sample_cmp(q.reshape(bs, ts, qw), kv_win.reshape(bs, ts, kvw),
                                                         cwin.reshape(bs, cwin.shape[1], kvw), ab_pool, page_table,
                                                         n_groups, 8, 16)
            o = _nsa_sample_sel(q.reshape(bs, ts, qw), kv_sel.reshape(bs, ts, kvw), sel, x_cmp, x_win,
                                gates.reshape(bs, ts, 3 * qw), rows_minor(cache_sel_kv[li]), page_table, n_groups, 8)
            ys = _out_ffn(ys, [o.reshape(bs * ts, qw)], w_out_s, *ffn, tm_s)
            outs["win_s"].append(win_new.reshape(cwin.shape))
            outs["cmp_s"].append(kv5(kv_cmp, bs))
            outs["sel_s"].append(kv5(kv_sel, bs))
    st = lambda k: jnp.stack(outs[k])
    return (yp.reshape(bp, t, d), ys.reshape(bs, ts, d), st("dil_p"), st("dil_s"), st("conv_p"), st("conv_s"),
            st("rnn_p"), st("rnn_s"), st("win_p"), st("win_s"), st("cmp_p"), st("cmp_s"), st("sel_p"), st("sel_s"))
```

```python
import functools

import jax
import jax.numpy as jnp
from jax import lax
from jax.experimental import pallas as pl
from jax.experimental.pallas import tpu as pltpu

HEAD_DIM = 64
LANES = 128
DIL_PAIRS = ((128, 1), (512, 4), (2048, 16))
DIL_MAX = 2048
DIL_KEYS = 128
CONV_W = 4
LRU_C = 8.0
CMP_LEN = 32
CMP_STRIDE = 16
SEL_BLOCK = 64
SEL_TOP = 16
WIN = 512
NEG = -1e30
FORCE_BONUS = 1e3
EPS = 1e-6
VMEM_LIMIT = 56 * 1024 * 1024

F32 = jnp.float32
BF16 = jnp.bfloat16


def _cparams(n_grid):
    return pltpu.CompilerParams(dimension_semantics=("arbitrary",) * n_grid,
                                vmem_limit_bytes=VMEM_LIMIT)


def _resident(shape):
    return pl.BlockSpec(shape, lambda *_: (0,) * len(shape), pipeline_mode=pl.Buffered(1))


def _rms(x, g):
    return x * lax.rsqrt(jnp.mean(x * x, axis=-1, keepdims=True) + EPS) * g


def _alibi_slopes(n):
    return 2.0 ** (-8.0 * jnp.arange(1, n + 1, dtype=F32) / n)


def _norm_proj_kernel(x_ref, g_ref, w_ref, *refs, n_out, first_blocks, blocks_per_seq):
    wt_refs = refs[:len(first_blocks)]
    out_refs = refs[len(first_blocks):len(first_blocks) + n_out]
    ot_refs = refs[len(first_blocks) + n_out:]
    h = _rms(x_ref[...], g_ref[...]).astype(BF16)
    off = 0
    for o_ref in out_refs:
        n = o_ref.shape[-1]
        o_ref[...] = jnp.dot(h, w_ref[:, off:off + n], preferred_element_type=F32)
        off += n
    for wt_ref, ot_ref, first in zip(wt_refs, ot_refs, first_blocks):
        @pl.when(pl.program_id(0) % blocks_per_seq >= first)
        def _(wt_ref=wt_ref, ot_ref=ot_ref):
            ot_ref[...] = lax.dot_general(wt_ref[...], h, (((1,), (1,)), ((), ())), preferred_element_type=F32)


def _norm_proj(x, g, w, splits, tm, rows_minor=(), seq=None):
    n, d = x.shape
    assert n % tm == 0 and sum(splits) == w.shape[1]
    blocks_per_seq = (seq or n) // tm
    first_blocks = []
    t_specs, t_shapes = [], []
    for w_t, keep in rows_minor:
        assert seq % tm == 0 and keep % tm == 0
        first = (seq - keep) // tm
        first_blocks.append(first)
        t_specs.append(pl.BlockSpec((None, w_t.shape[0], tm), functools.partial(
            lambda first, i: (i // blocks_per_seq, 0, jnp.maximum(i % blocks_per_seq - first, 0)), first)))
        t_shapes.append(jax.ShapeDtypeStruct((n // seq, w_t.shape[0], keep), F32))
    return pl.pallas_call(
        functools.partial(_norm_proj_kernel, n_out=len(splits), first_blocks=tuple(first_blocks),
                          blocks_per_seq=blocks_per_seq),
        grid=(n // tm,),
        in_specs=[pl.BlockSpec((tm, d), lambda i: (i, 0)),
                  _resident((1, d)),
                  _resident(w.shape)] + [_resident(w_t.shape) for w_t, _ in rows_minor],
        out_specs=[pl.BlockSpec((tm, s), lambda i: (i, 0)) for s in splits] + t_specs,
        out_shape=[jax.ShapeDtypeStruct((n, s), F32) for s in splits] + t_shapes,
        compiler_params=_cparams(1),
        name="norm_proj",
    )(x, g.reshape(1, d), w, *[w_t for w_t, _ in rows_minor])


def _out_ffn_kernel(*refs, n_mix, n_chunks, final_norm):
    res_ref = refs[0]
    a_refs = refs[1:1 + n_mix]
    wo_ref, gf_ref, w1_ref, w3_ref, w2_ref = refs[1 + n_mix:6 + n_mix]
    rest = refs[6 + n_mix:]
    go_ref = rest[0] if final_norm else None
    o_ref = rest[-1]
    mix = jnp.concatenate([a_ref[...].astype(BF16) for a_ref in a_refs], axis=1)
    y = res_ref[...] + jnp.dot(mix, wo_ref[...], preferred_element_type=F32)
    h = _rms(y, gf_ref[...]).astype(BF16)
    ch = w1_ref.shape[1] // n_chunks
    for c in range(n_chunks):
        a = jnp.dot(h, w1_ref[:, c * ch:(c + 1) * ch], preferred_element_type=F32)
        b = jnp.dot(h, w3_ref[:, c * ch:(c + 1) * ch], preferred_element_type=F32)
        act = (a * (1.0 / (1.0 + jnp.exp(-a))) * b).astype(BF16)
        y = y + jnp.dot(act, w2_ref[c * ch:(c + 1) * ch, :], preferred_element_type=F32)
    if final_norm:
        y = _rms(y, go_ref[...])
    o_ref[...] = y


def _out_ffn(res, mixes, w_out, g_ffn, w1, w3, w2, g_out, tm):
    n, d = res.shape
    hidden = w1.shape[1]
    n_chunks = hidden // 256
    assert n % tm == 0 and hidden % 256 == 0
    final_norm = g_out is not None
    row = lambda width: pl.BlockSpec((tm, width), lambda i: (i, 0))
    in_specs = [row(d)] + [row(m.shape[1]) for m in mixes] + [_resident(w_out.shape)]
    in_specs += [_resident((1, d)), _resident(w1.shape), _resident(w3.shape), _resident(w2.shape)]
    args = [res, *mixes, w_out, g_ffn.reshape(1, d), w1, w3, w2]
    if final_norm:
        in_specs.append(_resident((1, d)))
        args.append(g_out.reshape(1, d))
    return pl.pallas_call(
        functools.partial(_out_ffn_kernel, n_mix=len(mixes), n_chunks=n_chunks, final_norm=final_norm),
        grid=(n // tm,),
        in_specs=in_specs,
        out_specs=row(d),
        out_shape=jax.ShapeDtypeStruct((n, d), F32),
        compiler_params=_cparams(1),
        name="out_ffn",
    )(*args)


def _dil_attn_kernel(slope_ref, q_ref, k_ref, v_ref, o_ref, *scratch, seq):
    ob_refs, lse_refs = scratch[:3], scratch[3:]
    hp = pl.program_id(1)
    lane_lo = lax.broadcasted_iota(jnp.int32, (1, LANES), 1) < HEAD_DIM
    tq, tk = DIL_KEYS, 2 * DIL_KEYS
    head1 = lax.broadcasted_iota(jnp.int32, (2 * tq, 1), 0) >= tq
    slope_rows = jnp.where(head1, slope_ref[pl.ds(2 * hp + 1, 1), 0:1], slope_ref[pl.ds(2 * hp, 1), 0:1])
    iq = lax.broadcasted_iota(jnp.int32, (2 * tq, tk), 0) % tq
    ik = lax.broadcasted_iota(jnp.int32, (2 * tq, tk), 1)
    unroll = 4

    for br, (_, dil) in enumerate(DIL_PAIRS):
        n_blk = seq // dil // tq
        assert (dil * n_blk) % unroll == 0

        def blocks(it, carry, br=br, dil=dil, n_blk=n_blk):
            for u in range(unroll):
                idx = it * unroll + u
                r = idx // n_blk
                blk = idx % n_blk
                a0 = blk * tq
                ka0 = jnp.maximum(blk - 1, 0) * tq
                q_rows = pl.ds(r + dil * a0, tq, stride=dil) if dil > 1 else pl.ds(a0, tq)
                k_rows = pl.ds(r + dil * ka0, tk, stride=dil) if dil > 1 else pl.ds(ka0, tk)
                q = q_ref[q_rows, :] * (HEAD_DIM ** -0.5)
                q2 = jnp.concatenate([jnp.where(lane_lo, q, 0.0), jnp.where(lane_lo, 0.0, q)], axis=0).astype(BF16)
                k = k_ref[k_rows, :].astype(BF16)
                v = v_ref[k_rows, :].astype(BF16)
                da = (a0 - ka0) + iq - ik
                valid = (da >= 0) & (da <= DIL_KEYS)
                s = jnp.where(valid, _nt_dot(q2, k) - slope_rows * (dil * da).astype(F32), NEG)
                m = jnp.max(s, axis=-1, keepdims=True)
                p = jnp.exp(s - m)
                l = jnp.sum(p, axis=-1, keepdims=True)
                out = jnp.dot(p.astype(BF16), v, preferred_element_type=F32) / l
                lse = m + jnp.log(l)
                ob_refs[br][q_rows, :] = jnp.where(lane_lo, out[0:tq], out[tq:2 * tq])
                lse_refs[br][q_rows, :] = jnp.where(lane_lo, lse[0:tq], lse[tq:2 * tq])
            return carry

        lax.fori_loop(0, dil * n_blk // unroll, blocks, 0)

    rows = 512
    def merge(i, carry):
        sl = pl.ds(i * rows, rows)
        ls = [lse_refs[b][sl, :] for b in range(3)]
        m = jnp.maximum(jnp.maximum(ls[0], ls[1]), ls[2])
        es = [jnp.exp(x - m) for x in ls]
        num = es[0] * ob_refs[0][sl, :] + es[1] * ob_refs[1][sl, :] + es[2] * ob_refs[2][sl, :]
        o_ref[sl, :] = num / (es[0] + es[1] + es[2])
        return carry
    lax.fori_loop(0, seq // rows, merge, 0)


def _dil_attn(q, kv, n_heads):
    b, t, width = q.shape
    n_hp = width // LANES
    assert t % (16 * 2 * DIL_KEYS) == 0
    slopes = jnp.broadcast_to(_alibi_slopes(n_heads)[:, None], (n_heads, LANES))
    blk = lambda off: pl.BlockSpec((None, t, LANES), lambda i, j: (i, 0, off + j))
    return pl.pallas_call(
        functools.partial(_dil_attn_kernel, seq=t),
        grid=(b, n_hp),
        in_specs=[_resident((n_heads, LANES)), blk(0), blk(0), blk(n_hp)],
        out_specs=blk(0),
        out_shape=jax.ShapeDtypeStruct((b, t, width), F32),
        scratch_shapes=[pltpu.VMEM((t, LANES), F32)] * 6,
        compiler_params=_cparams(2),
        name="dil_attn",
    )(slopes, q, kv, kv)


def _rglru_kernel(xr_ref, gate_ref, cprev_ref, hprev_ref, cw_ref, cb_ref, wa_ref, ba_ref,
                  wx_ref, bx_ref, lam_ref, y_ref, cnew_ref, hnew_ref,
                  xin_ref, a_ref, u_ref, h_ref):
    ti = pl.program_id(1)
    bb, tc, width = xr_ref.shape
    pad = 8
    tail = CONV_W - 1

    @pl.when(ti == 0)
    def _():
        xin_ref[:, pad - tail:pad, :] = cprev_ref[...]
        h_ref[...] = hprev_ref[...]

    xin_ref[:, pad:pad + tc, :] = xr_ref[...]
    lam = lam_ref[...]
    neg_softplus_c = -LRU_C * (jnp.maximum(-lam, 0.0) + jnp.log1p(jnp.exp(-jnp.abs(lam))))
    for b in range(bb):
        xc = cb_ref[...] + sum(xin_ref[b, pad - tail + k:pad - tail + k + tc, :] * cw_ref[k:k + 1, :]
                               for k in range(CONV_W))
        xcb = xc.astype(BF16)
        ra = jnp.dot(xcb, wa_ref[...], preferred_element_type=F32) + ba_ref[...]
        rx = jnp.dot(xcb, wx_ref[...], preferred_element_type=F32) + bx_ref[...]
        rg = 1.0 / (1.0 + jnp.exp(-ra))
        ig = 1.0 / (1.0 + jnp.exp(-rx))
        log_a = neg_softplus_c * rg
        a = jnp.exp(log_a)
        a_ref[b] = a
        u_ref[b] = jnp.sqrt(jnp.tanh(-log_a) * (1.0 + a * a)) * ig * xc

    def step(t, hs):
        new = []
        for b in range(bb):
            hb = a_ref[b, pl.ds(t, 1), :] * hs[b] + u_ref[b, pl.ds(t, 1), :]
            u_ref[b, pl.ds(t, 1), :] = hb
            new.append(hb)
        return tuple(new)

    hs = lax.fori_loop(0, tc, step, tuple(h_ref[b:b + 1, :] for b in range(bb)), unroll=8)
    for b in range(bb):
        h_ref[b:b + 1, :] = hs[b]
    g = gate_ref[...]
    cdf = 0.5 * (1.0 + jnp.tanh(0.7978845608028654 * (g + 0.044715 * (g * g * g))))
    y_ref[...] = u_ref[...] * (g * cdf)
    xin_ref[:, pad - tail:pad, :] = xin_ref[:, pad + tc - tail:pad + tc, :]
    cnew_ref[...] = xin_ref[:, pad - tail:pad, :]
    hnew_ref[...] = h_ref[...]


def _rglru(xr, gate, conv_prev, h_prev, conv_w, conv_b, wa_bd, ba, wx_bd, bx, lam, bb, tc):
    b, t, width = xr.shape
    assert b % bb == 0 and t % tc == 0 and tc >= CONV_W - 1
    seq_blk = pl.BlockSpec((bb, tc, width), lambda i, j: (i, j, 0))
    vec = _resident((1, width))
    return pl.pallas_call(
        _rglru_kernel,
        grid=(b // bb, t // tc),
        in_specs=[seq_blk, seq_blk,
                  pl.BlockSpec((bb, CONV_W - 1, width), lambda i, j: (i, 0, 0)),
                  pl.BlockSpec((bb, width), lambda i, j: (i, 0)),
                  _resident((CONV_W, width)), vec, _resident(wa_bd.shape), vec,
                  _resident(wx_bd.shape), vec, vec],
        out_specs=[seq_blk,
                   pl.BlockSpec((bb, CONV_W - 1, width), lambda i, j: (i, 0, 0)),
                   pl.BlockSpec((bb, width), lambda i, j: (i, 0))],
        out_shape=[jax.ShapeDtypeStruct((b, t, width), F32),
                   jax.ShapeDtypeStruct((b, CONV_W - 1, width), F32),
                   jax.ShapeDtypeStruct((b, width), F32)],
        scratch_shapes=[pltpu.VMEM((bb, tc + 8, width), F32), pltpu.VMEM((bb, tc, width), F32),
                        pltpu.VMEM((bb, tc, width), F32), pltpu.VMEM((bb, width), F32)],
        compiler_params=_cparams(2),
        name="rglru",
    )(xr, gate, conv_prev, h_prev, conv_w, conv_b.reshape(1, width), wa_bd, ba.reshape(1, width),
      wx_bd, bx.reshape(1, width), lam.reshape(1, width))


def _block_diag(w):
    n, bi, bj = w.shape
    eye = jnp.eye(n, dtype=w.dtype)
    return (eye[:, None, :, None] * w[:, :, None, :]).reshape(n * bi, n * bj)


def _lane_lo():
    return lax.broadcasted_iota(jnp.int32, (1, LANES), 1) < HEAD_DIM


def _spread_kv(k2, v2, half):
    lo = _lane_lo()
    own = lo if half == 0 else ~lo
    k_own = jnp.where(own, k2, 0.0)
    v_own = jnp.where(own, v2, 0.0)
    k_other = pltpu.roll(k_own, HEAD_DIM, axis=1)
    v_both = v_own + pltpu.roll(v_own, HEAD_DIM, axis=1)
    k_lo, k_hi = (k_own, k_other) if half == 0 else (k_other, k_own)
    return k_lo.astype(BF16), k_hi.astype(BF16), v_both.astype(BF16)


def _group_heads(q_ref, gl):
    base = gl * 4 * HEAD_DIM
    qa = (q_ref[:, base:base + LANES] * (HEAD_DIM ** -0.5)).astype(BF16)
    qb = (q_ref[:, base + LANES:base + 2 * LANES] * (HEAD_DIM ** -0.5)).astype(BF16)
    return ((qa, 0), (qa, 1), (qb, 0), (qb, 1))


def _nt_dot(a, b):
    return lax.dot_general(a, b, (((1,), (1,)), ((), ())), preferred_element_type=F32)


def _compress_kernel(*refs, n_ch, merged, by_row=False):
    x_refs, (w_ref, pe_ref, o_ref) = refs[:-3], refs[-3:]
    per_half = len(x_refs) // 2
    half = per_half * LANES
    pe_pad = 16
    for c in range(2):
        xs = jnp.concatenate([x_ref[s] if by_row else x_ref[pl.ds(s, n_ch, stride=CMP_STRIDE), :]
                              for s in range(CMP_STRIDE)
                              for x_ref in x_refs[c * per_half:(c + 1) * per_half]], axis=1).astype(BF16)
        acc = []
        for i in range(2):
            pe_rows = jnp.concatenate([pe_ref[i * CMP_STRIDE + s:i * CMP_STRIDE + s + 1, c * half:(c + 1) * half]
                                       for s in range(CMP_STRIDE)], axis=1)
            lhs = jnp.concatenate([xs, jnp.broadcast_to(pe_rows, (pe_pad, pe_rows.shape[1])).astype(BF16)], axis=0)
            acc.append(jnp.dot(lhs, w_ref[i * 2 + c], preferred_element_type=F32))
        pe_acc = acc[0][n_ch:n_ch + 1, :] + acc[1][n_ch:n_ch + 1, :]
        acc = [a[0:n_ch, :] for a in acc]
        if merged:
            o_ref[:, c * half:(c + 1) * half] = acc[0] + pltpu.roll(acc[1], n_ch - 1, axis=0) + pe_acc[0:1, :]
        else:
            o_ref[:, c * half:(c + 1) * half] = acc[0] + pe_acc[0:1, :]
            o_ref[:, (2 + c) * half:(3 + c) * half] = acc[1]


def _compress_weights(w_cmp, pe_cmp, n_groups):
    n_r = CMP_LEN // CMP_STRIDE
    gw = n_groups * HEAD_DIM
    w = w_cmp.reshape(n_r, CMP_STRIDE, 2, HEAD_DIM, HEAD_DIM).transpose(0, 2, 1, 3, 4)
    eye = jnp.eye(n_groups, dtype=w.dtype)
    wbd = (eye[None, None, None, :, None, :, None] * w[:, :, :, None, :, None, :]).reshape(n_r * 2, CMP_STRIDE * gw, gw)
    pe = jnp.broadcast_to(pe_cmp[:, :, None, :], (CMP_LEN, 2, n_groups, HEAD_DIM)).reshape(CMP_LEN, -1)
    return wbd.astype(BF16), pe


def _compress(ctx, wbd, pe):
    b, tc, width = ctx.shape
    n_ch = tc // CMP_STRIDE
    return pl.pallas_call(
        functools.partial(_compress_kernel, n_ch=n_ch, merged=True),
        grid=(b,),
        in_specs=[pl.BlockSpec((None, tc, LANES), functools.partial(lambda j, i: (i, 0, j), j))
                  for j in range(width // LANES)] + [_resident(wbd.shape), _resident(pe.shape)],
        out_specs=pl.BlockSpec((None, n_ch, width), lambda i: (i, 0, 0)),
        out_shape=jax.ShapeDtypeStruct((b, n_ch, width), F32),
        compiler_params=_cparams(1),
        name="nsa_compress",
    )(*([ctx] * (width // LANES)), wbd, pe)


def _cmp_select_kernel(qtab_ref, ktab_ref, c2s_ref, q_ref, kc_ref, vc_ref, o_ref, sel_ref, *, n_c, n_sel):
    gp, qi = pl.program_id(1), pl.program_id(2)
    tq = q_ref.shape[0]
    n_ch = kc_ref.shape[0]
    qp = qi * tq + lax.broadcasted_iota(jnp.int32, (tq, 1), 0)
    n_idx = lax.broadcasted_iota(jnp.int32, (1, n_ch), 1)
    cmp_end = n_idx * CMP_STRIDE + (CMP_LEN - 1)
    valid = (cmp_end <= qp) & (n_idx < n_c)
    blk = lax.broadcasted_iota(jnp.int32, (1, LANES), 1)
    cb = qp // SEL_BLOCK
    blk_ok = (blk <= cb) & (blk < n_sel)
    forced = (blk == 0) | (blk == cb) | (blk == cb - 1)
    blk_t = lax.broadcasted_iota(jnp.int32, (n_sel, tq), 0)
    valid4 = jnp.concatenate([valid] * 4, axis=0)
    for gl in range(2):
        k_aug, v_both = _stage_kv_aug(kc_ref, vc_ref, ktab_ref, gl)
        q_all = _stack_heads(q_ref, gl, qtab_ref, (2 * gp + gl) * 4, None)
        s = jnp.where(valid4, _nt_dot(q_all, k_aug), NEG)
        m = jnp.max(s, axis=-1, keepdims=True)
        p = jnp.where(valid4, jnp.exp(s - m), 0.0)
        p = (p / jnp.maximum(jnp.sum(p, axis=-1, keepdims=True), 1e-30)).astype(BF16)
        _unstack_heads(o_ref, gl, jnp.dot(p, v_both, preferred_element_type=F32), tq)
        imp_rows = jnp.dot(p, c2s_ref[...], preferred_element_type=F32)
        imp = imp_rows[0:tq] + imp_rows[tq:2 * tq] + imp_rows[2 * tq:3 * tq] + imp_rows[3 * tq:4 * tq]
        score = jnp.where(blk_ok, imp + FORCE_BONUS * forced.astype(F32), NEG)
        score_t = score.T[0:n_sel, :]
        rank = jnp.zeros((n_sel, tq), F32)
        for k in range(n_sel):
            row = score_t[k:k + 1, :]
            ahead = (row > score_t) | ((row == score_t) & (k < blk_t))
            rank = rank + ahead.astype(F32)
        chosen = ((rank < SEL_TOP) & (score_t > 0.5 * NEG)).astype(F32)
        chosen = jnp.concatenate([chosen, jnp.zeros((LANES - n_sel, tq), F32)], axis=0) if n_sel < LANES else chosen
        sel_ref[gl] = chosen.T.astype(BF16)


def _cmp_select(q, kvc, n_c, n_sel, n_groups, tq):
    b, t, width = q.shape
    n_ch = kvc.shape[1]
    n_gp = n_groups // 2
    n_heads = width // HEAD_DIM
    assert tq == LANES and n_sel <= LANES and n_sel % 8 == 0 and t % tq == 0
    qtab, ktab = _alibi_tables(n_heads, jnp.arange(n_ch) * CMP_STRIDE + (CMP_LEN - 1))
    cs = jnp.arange(n_ch)[:, None] * CMP_STRIDE
    ss = jnp.arange(LANES)[None, :] * SEL_BLOCK
    ov = jnp.maximum(jnp.minimum(cs + CMP_LEN, ss + SEL_BLOCK) - jnp.maximum(cs, ss), 0).astype(F32) / CMP_STRIDE
    c2s = jnp.where((jnp.arange(n_ch)[:, None] < n_c) & (jnp.arange(LANES)[None, :] < n_sel), ov, 0.0).astype(BF16)
    return pl.pallas_call(
        functools.partial(_cmp_select_kernel, n_c=n_c, n_sel=n_sel),
        grid=(b, n_gp, t // tq),
        in_specs=[_resident(qtab.shape), _resident(ktab.shape), _resident(c2s.shape),
                  pl.BlockSpec((None, tq, 2 * 4 * HEAD_DIM), lambda i, j, k: (i, k, j)),
                  pl.BlockSpec((None, n_ch, LANES), lambda i, j, k: (i, 0, j)),
                  pl.BlockSpec((None, n_ch, LANES), lambda i, j, k: (i, 0, n_gp + j))],
        out_specs=[pl.BlockSpec((None, tq, 2 * 4 * HEAD_DIM), lambda i, j, k: (i, k, j)),
                   pl.BlockSpec((None, 2, tq, LANES), lambda i, j, k: (i, j, k, 0))],
        out_shape=[jax.ShapeDtypeStruct((b, t, width), F32),
                   jax.ShapeDtypeStruct((b, n_groups, t, LANES), BF16)],
        compiler_params=_cparams(3),
        name="nsa_cmp_select",
    )(qtab, ktab, c2s, q, kvc, kvc)


def _stage_kv(k_ref, v_ref, kv_scr):
    for gl in range(2):
        k_lo, k_hi, v_both = _spread_kv(k_ref[...], v_ref[...], gl)
        kv_scr[3 * gl + 0][...] = k_lo
        kv_scr[3 * gl + 1][...] = k_hi
        kv_scr[3 * gl + 2][...] = v_both


def _sel_attn_kernel(slope_ref, exp_ref, q_ref, sel_ref, k_ref, v_ref, o_ref, *kv_scr, tk):
    gp, qi = pl.program_id(1), pl.program_id(2)
    tq = q_ref.shape[0]
    lo = _lane_lo()

    @pl.when(qi == 0)
    def _():
        _stage_kv(k_ref, v_ref, kv_scr)

    qp = qi * tq + lax.broadcasted_iota(jnp.int32, (tq, 1), 0)
    n_kt = (qi * tq + tq - 1) // tk + 1
    for gl in range(2):
        heads = _group_heads(q_ref, gl)
        slopes = [slope_ref[pl.ds((2 * gp + gl) * 4 + r, 1), 0:1] for r in range(4)]
        sel = sel_ref[gl]
        k_scr = (kv_scr[3 * gl], kv_scr[3 * gl + 1])
        v_scr = kv_scr[3 * gl + 2]

        def tile(kt, carry, heads=heads, slopes=slopes, sel=sel, k_scr=k_scr, v_scr=v_scr):
            rows = pl.ds(pl.multiple_of(kt * tk, tk), tk)
            kpos = kt * tk + lax.broadcasted_iota(jnp.int32, (1, tk), 1)
            picked = _nt_dot(sel, exp_ref[rows, :]) > 0.5
            ok = picked & (kpos <= qp)
            dist = (qp - kpos).astype(F32)
            v = v_scr[rows, :]
            new = []
            for r, (qh, variant) in enumerate(heads):
                m_old, l_old, acc_old = carry[r]
                s = jnp.where(ok, _nt_dot(qh, k_scr[variant][rows, :]) - slopes[r] * dist, NEG)
                m_new = jnp.maximum(m_old, jnp.max(s, axis=-1, keepdims=True))
                alpha = jnp.exp(m_old - m_new)
                p = jnp.where(ok, jnp.exp(s - m_new), 0.0)
                l_new = alpha * l_old + jnp.sum(p, axis=-1, keepdims=True)
                acc_new = alpha * acc_old + jnp.dot(p.astype(BF16), v, preferred_element_type=F32)
                new.append((m_new, l_new, acc_new))
            return tuple(new)

        init = tuple((jnp.full((tq, 1), NEG, F32), jnp.zeros((tq, 1), F32), jnp.zeros((tq, LANES), F32))
                     for _ in range(4))
        res = lax.fori_loop(0, n_kt, tile, init)
        outs = [acc / jnp.maximum(l, 1e-30) for (_, l, acc) in res]
        base = gl * 4 * HEAD_DIM
        o_ref[:, base:base + LANES] = jnp.where(lo, outs[0], outs[1])
        o_ref[:, base + LANES:base + 2 * LANES] = jnp.where(lo, outs[2], outs[3])


def _sel_attn(q, kv, sel, n_groups, tq, tk):
    b, t, width = q.shape
    n_gp = n_groups // 2
    n_heads = width // HEAD_DIM
    assert t % tq == 0 and t % tk == 0 and tk % SEL_BLOCK == 0
    slopes = jnp.broadcast_to(_alibi_slopes(n_heads)[:, None], (n_heads, LANES))
    expand = (jnp.arange(t)[:, None] // SEL_BLOCK == jnp.arange(LANES)[None, :]).astype(BF16)
    return pl.pallas_call(
        functools.partial(_sel_attn_kernel, tk=tk),
        grid=(b, n_gp, t // tq),
        in_specs=[_resident(slopes.shape), _resident(expand.shape),
                  pl.BlockSpec((None, tq, 2 * 4 * HEAD_DIM), lambda i, j, k: (i, k, j)),
                  pl.BlockSpec((None, 2, tq, LANES), lambda i, j, k: (i, j, k, 0)),
                  pl.BlockSpec((None, t, LANES), lambda i, j, k: (i, 0, j)),
                  pl.BlockSpec((None, t, LANES), lambda i, j, k: (i, 0, n_gp + j))],
        out_specs=pl.BlockSpec((None, tq, 2 * 4 * HEAD_DIM), lambda i, j, k: (i, k, j)),
        out_shape=jax.ShapeDtypeStruct((b, t, width), F32),
        scratch_shapes=[pltpu.VMEM((t, LANES), BF16)] * 6,
        compiler_params=_cparams(3),
        name="nsa_sel_attn",
    )(slopes, expand, q, sel, kv, kv)


ALIBI_LANES = 6
MASK_BIG = 2.0 ** 100


def _alibi_tables(n_heads, pos):
    t = pos.shape[0]
    slopes = _alibi_slopes(n_heads)
    s1 = slopes.astype(BF16).astype(F32)
    s2 = (slopes - s1).astype(BF16).astype(F32)
    s3 = (slopes - s1 - s2).astype(BF16).astype(F32)
    qtab = jnp.zeros((n_heads, LANES), F32).at[:, HEAD_DIM:HEAD_DIM + ALIBI_LANES].set(
        jnp.stack([s1, s2, s3, s1, s2, s3], axis=-1))
    hi = (pos // SEL_BLOCK * SEL_BLOCK).astype(F32)
    lo = (pos % SEL_BLOCK).astype(F32)
    ktab = jnp.zeros((t, LANES), F32).at[:, HEAD_DIM:HEAD_DIM + ALIBI_LANES].set(
        jnp.stack([hi, hi, hi, lo, lo, lo], axis=-1))
    return qtab, ktab


def _stack_heads(q_ref, gl, qtab_ref, first_head, extra):
    lo = _lane_lo()
    parts = []
    for r in range(4):
        c0 = gl * 4 * HEAD_DIM + (r // 2) * LANES
        slab = q_ref[:, c0:c0 + LANES] * (HEAD_DIM ** -0.5)
        if r % 2:
            slab = pltpu.roll(slab, HEAD_DIM, axis=1)
        qa = jnp.where(lo, slab, qtab_ref[pl.ds(first_head + r, 1), :]).astype(BF16)
        parts.append(qa if extra is None else jnp.concatenate([qa, extra], axis=1))
    return jnp.concatenate(parts, axis=0)


def _stage_kv_aug(k_ref, v_ref, ktab_ref, gl):
    lo = _lane_lo()
    own = lo if gl == 0 else ~lo
    k_own = jnp.where(own, k_ref[...], 0.0)
    v_own = jnp.where(own, v_ref[...], 0.0)
    k_low = k_own if gl == 0 else pltpu.roll(k_own, HEAD_DIM, axis=1)
    return (k_low + ktab_ref[...]).astype(BF16), (v_own + pltpu.roll(v_own, HEAD_DIM, axis=1)).astype(BF16)


def _unstack_heads(o_ref, gl, out, tq):
    lo = _lane_lo()
    base = gl * 4 * HEAD_DIM
    o_ref[:, base:base + LANES] = jnp.where(lo, out[0:tq], out[tq:2 * tq])
    o_ref[:, base + LANES:base + 2 * LANES] = jnp.where(lo, out[2 * tq:3 * tq], out[3 * tq:4 * tq])


def _sel_attn2_kernel(qtab_ref, ktab_ref, q_ref, sel_ref, k_ref, v_ref, o_ref, ka0, ka1, v0, v1, qv_ref, *, tk):
    gp, qi = pl.program_id(1), pl.program_id(2)
    tq = q_ref.shape[0]
    staged = ((ka0, v0), (ka1, v1))
    n_var = qv_ref.shape[0]
    blocks_per_tile = tk // SEL_BLOCK
    mask_lane0 = HEAD_DIM + ALIBI_LANES

    @pl.when(qi == 0)
    def _():
        for gl, (ka, vs) in enumerate(staged):
            ka[...], vs[...] = _stage_kv_aug(k_ref, v_ref, ktab_ref, gl)

    q0 = qi * tq
    n_full = q0 // tk
    qp = q0 + lax.broadcasted_iota(jnp.int32, (4 * tq, 1), 0) % tq
    lane = lax.broadcasted_iota(jnp.int32, (1, LANES), 1)
    mask_lanes = (lane >= mask_lane0) & (lane < mask_lane0 + blocks_per_tile)
    for gl, (ka, vs) in enumerate(staged):
        q_base = _stack_heads(q_ref, gl, qtab_ref, (2 * gp + gl) * 4, None)
        not_sel = sel_ref[gl].astype(F32) - 1.0
        for var in range(n_var):
            shift = (mask_lane0 - var * blocks_per_tile) % LANES
            flags = jnp.where(mask_lanes, pltpu.roll(not_sel, shift, axis=1) if shift else not_sel, 0.0).astype(BF16)
            qv_ref[var] = q_base + jnp.concatenate([flags] * 4, axis=0)

        def scores(kt, ka=ka):
            return _nt_dot(qv_ref[kt], ka[pl.ds(pl.multiple_of(kt * tk, tk), tk), :])

        def absorb(kt, s, carry, masked, vs=vs):
            m_old, l_old, acc_old = carry
            if masked:
                kpos = kt * tk + lax.broadcasted_iota(jnp.int32, (1, tk), 1)
                s = jnp.where(kpos <= qp, s, NEG)
            m_new = jnp.maximum(m_old, jnp.max(s, axis=-1, keepdims=True))
            alpha = jnp.exp(m_old - m_new)
            p = jnp.exp(s - m_new)
            l_new = alpha * l_old + jnp.sum(p, axis=-1, keepdims=True)
            v = vs[pl.ds(pl.multiple_of(kt * tk, tk), tk), :]
            acc_new = alpha * acc_old + jnp.dot(p.astype(BF16), v, preferred_element_type=F32)
            return m_new, l_new, acc_new

        def run(n):
            carry = (jnp.full((4 * tq, 1), NEG, F32), jnp.zeros((4 * tq, 1), F32), jnp.zeros((4 * tq, LANES), F32))
            ss = [scores(kt) for kt in range(n + 1)]
            for kt in range(n):
                carry = absorb(kt, ss[kt], carry, False)
            _, l, acc = absorb(n, ss[n], carry, True)
            return acc / jnp.maximum(l, 1e-30)

        out = lax.switch(n_full, [functools.partial(run, n) for n in range(n_var)])
        _unstack_heads(o_ref, gl, out, tq)


def _sel_attn2(q, kv, sel, n_groups, tq, tk):
    b, t, width = q.shape
    n_gp = n_groups // 2
    n_heads = width // HEAD_DIM
    blocks_per_tile = tk // SEL_BLOCK
    assert t % tk == 0 and tk % tq == 0 and HEAD_DIM + ALIBI_LANES + blocks_per_tile <= LANES
    qtab, ktab = _alibi_tables(n_heads, jnp.arange(t))
    blk_lane = HEAD_DIM + ALIBI_LANES + (jnp.arange(t) // SEL_BLOCK) % blocks_per_tile
    ktab = ktab + jnp.where(blk_lane[:, None] == jnp.arange(LANES)[None, :], MASK_BIG, 0.0)
    return pl.pallas_call(
        functools.partial(_sel_attn2_kernel, tk=tk),
        grid=(b, n_gp, t // tq),
        in_specs=[_resident(qtab.shape), _resident(ktab.shape),
                  pl.BlockSpec((None, tq, 2 * 4 * HEAD_DIM), lambda i, j, k: (i, k, j)),
                  pl.BlockSpec((None, 2, tq, LANES), lambda i, j, k: (i, j, k, 0)),
                  pl.BlockSpec((None, t, LANES), lambda i, j, k: (i, 0, j)),
                  pl.BlockSpec((None, t, LANES), lambda i, j, k: (i, 0, n_gp + j))],
        out_specs=pl.BlockSpec((None, tq, 2 * 4 * HEAD_DIM), lambda i, j, k: (i, k, j)),
        out_shape=jax.ShapeDtypeStruct((b, t, width), F32),
        scratch_shapes=[pltpu.VMEM((t, LANES), BF16)] * 4 + [pltpu.VMEM((t // tk, 4 * tq, LANES), BF16)],
        compiler_params=_cparams(3),
        name="nsa_sel_attn",
    )(qtab, ktab, q, sel, kv, kv)


def _win_combine_kernel(slope_ref, q_ref, gate_ref, ocmp_ref, osel_ref, k_ref, v_ref, o_ref, *kv_scr, seq):
    gp, qi = pl.program_id(1), pl.program_id(2)
    tq = q_ref.shape[0]
    tk = WIN + tq
    lo = _lane_lo()

    @pl.when(qi == 0)
    def _():
        _stage_kv(k_ref, v_ref, kv_scr)

    k0 = jnp.clip(qi * tq - WIN, 0, seq - tk)
    rows = pl.ds(pl.multiple_of(k0, tq), tk)
    qp = qi * tq + lax.broadcasted_iota(jnp.int32, (tq, 1), 0)
    kpos = k0 + lax.broadcasted_iota(jnp.int32, (1, tk), 1)
    dist_i = qp - kpos
    ok = (dist_i >= 0) & (dist_i <= WIN)
    dist = dist_i.astype(F32)
    gates = 1.0 / (1.0 + jnp.exp(-gate_ref[...]))
    for gl in range(2):
        outs = []
        v = kv_scr[3 * gl + 2][rows, :]
        for r, (qh, variant) in enumerate(_group_heads(q_ref, gl)):
            slope = slope_ref[pl.ds((2 * gp + gl) * 4 + r, 1), 0:1]
            s = jnp.where(ok, _nt_dot(qh, kv_scr[3 * gl + variant][rows, :]) - slope * dist, NEG)
            m = jnp.max(s, axis=-1, keepdims=True)
            p = jnp.where(ok, jnp.exp(s - m), 0.0)
            l = jnp.sum(p, axis=-1, keepdims=True)
            outs.append(jnp.dot(p.astype(BF16), v, preferred_element_type=F32) / jnp.maximum(l, 1e-30))
        for half in range(2):
            cols = slice(gl * 4 * HEAD_DIM + half * LANES, gl * 4 * HEAD_DIM + (half + 1) * LANES)
            o_win = jnp.where(lo, outs[2 * half], outs[2 * half + 1])
            c0 = (gl * 4 + 2 * half) * 3
            g_cmp, g_sel, g_win = (jnp.where(lo, gates[:, c0 + i:c0 + i + 1], gates[:, c0 + 3 + i:c0 + 4 + i])
                                   for i in range(3))
            o_ref[:, cols] = g_cmp * ocmp_ref[:, cols] + g_sel * osel_ref[:, cols] + g_win * o_win


def _win_combine(q, kv, gates, o_cmp, o_sel, n_groups, tq):
    b, t, width = q.shape
    n_gp = n_groups // 2
    n_heads = width // HEAD_DIM
    assert t % tq == 0 and t >= WIN + tq
    slopes = jnp.broadcast_to(_alibi_slopes(n_heads)[:, None], (n_heads, LANES))
    qblk = pl.BlockSpec((None, tq, 2 * 4 * HEAD_DIM), lambda i, j, k: (i, k, j))
    return pl.pallas_call(
        functools.partial(_win_combine_kernel, seq=t),
        grid=(b, n_gp, t // tq),
        in_specs=[_resident(slopes.shape), qblk,
                  pl.BlockSpec((None, tq, LANES), lambda i, j, k: (i, k, j)),
                  qblk, qblk,
                  pl.BlockSpec((None, t, LANES), lambda i, j, k: (i, 0, j)),
                  pl.BlockSpec((None, t, LANES), lambda i, j, k: (i, 0, n_gp + j))],
        out_specs=qblk,
        out_shape=jax.ShapeDtypeStruct((b, t, width), F32),
        scratch_shapes=[pltpu.VMEM((t, LANES), BF16)] * 6,
        compiler_params=_cparams(3),
        name="nsa_win_combine",
    )(slopes, q, gates, o_cmp, o_sel, kv, kv)


def _win_combine2_kernel(qtab_ref, ktab_ref, q_ref, gate_ref, ocmp_ref, osel_ref, k_ref, v_ref, o_ref,
                         ka0, ka1, v0, v1, *, seq):
    gp, qi = pl.program_id(1), pl.program_id(2)
    tq = q_ref.shape[0]
    tk = WIN + tq
    lo = _lane_lo()
    staged = ((ka0, v0), (ka1, v1))

    @pl.when(qi == 0)
    def _():
        for gl, (ka, vs) in enumerate(staged):
            ka[...], vs[...] = _stage_kv_aug(k_ref, v_ref, ktab_ref, gl)

    k0 = jnp.clip(qi * tq - WIN, 0, seq - tk)
    rows = pl.ds(pl.multiple_of(k0, tq), tk)
    qp = qi * tq + lax.broadcasted_iota(jnp.int32, (4 * tq, 1), 0) % tq
    dist = qp - (k0 + lax.broadcasted_iota(jnp.int32, (1, tk), 1))
    ok = (dist >= 0) & (dist <= WIN)
    gates = 1.0 / (1.0 + jnp.exp(-gate_ref[...]))
    for gl, (ka, vs) in enumerate(staged):
        q_all = _stack_heads(q_ref, gl, qtab_ref, (2 * gp + gl) * 4, None)
        s = jnp.where(ok, _nt_dot(q_all, ka[rows, :]), NEG)
        m = jnp.max(s, axis=-1, keepdims=True)
        p = jnp.where(ok, jnp.exp(s - m), 0.0)
        l = jnp.maximum(jnp.sum(p, axis=-1, keepdims=True), 1e-30)
        out = jnp.dot(p.astype(BF16), vs[rows, :], preferred_element_type=F32) / l
        for half in range(2):
            cols = slice(gl * 4 * HEAD_DIM + half * LANES, gl * 4 * HEAD_DIM + (half + 1) * LANES)
            o_win = jnp.where(lo, out[2 * half * tq:(2 * half + 1) * tq], out[(2 * half + 1) * tq:(2 * half + 2) * tq])
            c0 = (gl * 4 + 2 * half) * 3
            g_cmp, g_sel, g_win = (jnp.where(lo, gates[:, c0 + i:c0 + i + 1], gates[:, c0 + 3 + i:c0 + 4 + i])
                                   for i in range(3))
            o_ref[:, cols] = g_cmp * ocmp_ref[:, cols] + g_sel * osel_ref[:, cols] + g_win * o_win


def _win_combine2(q, kv, gates, o_cmp, o_sel, n_groups, tq):
    b, t, width = q.shape
    n_gp = n_groups // 2
    n_heads = width // HEAD_DIM
    assert t % tq == 0 and t >= WIN + tq
    qtab, ktab = _alibi_tables(n_heads, jnp.arange(t))
    qblk = pl.BlockSpec((None, tq, 2 * 4 * HEAD_DIM), lambda i, j, k: (i, k, j))
    return pl.pallas_call(
        functools.partial(_win_combine2_kernel, seq=t),
        grid=(b, n_gp, t // tq),
        in_specs=[_resident(qtab.shape), _resident(ktab.shape), qblk,
                  pl.BlockSpec((None, tq, LANES), lambda i, j, k: (i, k, j)),
                  qblk, qblk,
                  pl.BlockSpec((None, t, LANES), lambda i, j, k: (i, 0, j)),
                  pl.BlockSpec((None, t, LANES), lambda i, j, k: (i, 0, n_gp + j))],
        out_specs=qblk,
        out_shape=jax.ShapeDtypeStruct((b, t, width), F32),
        scratch_shapes=[pltpu.VMEM((t, LANES), BF16)] * 4,
        compiler_params=_cparams(3),
        name="nsa_win_combine",
    )(qtab, ktab, q, gates, o_cmp, o_sel, kv, kv)


def _nsa_prompt(q, kv_cmp, kv_sel, kv_win, gates, wbd, pe, n_groups):
    t = q.shape[1]
    kvc = _compress(kv_cmp, wbd, pe)
    n_c = t // CMP_STRIDE - CMP_LEN // CMP_STRIDE + 1
    n_sel = -(-t // SEL_BLOCK)
    o_cmp, sel = _cmp_select(q, kvc, n_c, n_sel, n_groups, LANES)
    o_sel = _sel_attn2(q, kv_sel, sel, n_groups, LANES, 8 * LANES)
    return _win_combine2(q, kv_win, gates, o_cmp, o_sel, n_groups, LANES)


def _head_block_diag(x, n_rep, lanes_per_head, rows_per_head):
    tiled = jnp.concatenate([x] * n_rep, axis=0)
    row_h = lax.broadcasted_iota(jnp.int32, tiled.shape, 0) // rows_per_head
    lane_h = lax.broadcasted_iota(jnp.int32, tiled.shape, 1) // lanes_per_head
    return jnp.where(row_h == lane_h, tiled, 0.0)


def _head_diag_rows(o, n_rep, lanes_per_head, rows_per_head):
    row_h = lax.broadcasted_iota(jnp.int32, o.shape, 0) // rows_per_head
    lane_h = lax.broadcasted_iota(jnp.int32, o.shape, 1) // lanes_per_head
    kept = jnp.where(row_h == lane_h, o, 0.0)
    return jnp.sum(kept.reshape(n_rep, rows_per_head, o.shape[1]), axis=0)


def _dil_sample_kernel(slope_ref, q_ref, kvn_ref, cache_ref, o_ref, ctx_ref, k_scr, v_scr):
    ts, width = q_ref.shape
    past = cache_ref.shape[0]
    n_heads = width // HEAD_DIM
    rows = n_heads * ts
    pad_rows = k_scr.shape[0] - past
    new_k = jnp.concatenate([kvn_ref[:, :width], jnp.zeros((pad_rows - ts, width), F32)], axis=0)
    new_v = jnp.concatenate([kvn_ref[:, width:], jnp.zeros((pad_rows - ts, width), F32)], axis=0)
    k_scr[0:past, :] = cache_ref[:, :width].astype(BF16)
    v_scr[0:past, :] = cache_ref[:, width:].astype(BF16)
    k_scr[past:, :] = new_k.astype(BF16)
    v_scr[past:, :] = new_v.astype(BF16)
    qbd = _head_block_diag(q_ref[...] * (HEAD_DIM ** -0.5), n_heads, HEAD_DIM, ts).astype(BF16)
    s = _nt_dot(qbd, k_scr[...])
    t_row = lax.broadcasted_iota(jnp.int32, (rows, 1), 0) % ts
    dist = (past + t_row) - lax.broadcasted_iota(jnp.int32, (1, k_scr.shape[0]), 1)
    count = jnp.zeros(s.shape, F32)
    for window, dil in DIL_PAIRS:
        assert dil & (dil - 1) == 0
        count = count + ((dist >= 0) & (dist <= window) & ((dist & (dil - 1)) == 0)).astype(F32)
    s = jnp.where(count > 0, s - slope_ref[:, 0:1] * dist.astype(F32), NEG)
    m = jnp.max(s, axis=-1, keepdims=True)
    p = count * jnp.exp(s - m)
    l = jnp.sum(p, axis=-1, keepdims=True)
    o = jnp.dot(p.astype(BF16), v_scr[...], preferred_element_type=F32) / l
    o_ref[...] = _head_diag_rows(o, n_heads, HEAD_DIM, ts)
    ctx_ref[0:past - ts, :] = cache_ref[ts:past, :]
    ctx_ref[past - ts:past, :] = kvn_ref[...]


def _dil_count(dist):
    count = jnp.zeros(dist.shape, F32)
    for window, dil in DIL_PAIRS:
        assert dil & (dil - 1) == 0
        count = count + ((dist >= 0) & (dist <= window) & ((dist & (dil - 1)) == 0)).astype(F32)
    return count


def _dil_sample2_kernel(slope_ref, q_ref, kvn_ref, cache_ref, o_ref, ctx_ref):
    ts, width = q_ref.shape
    past = cache_ref.shape[-1]
    n_heads = width // HEAD_DIM
    rows = n_heads * ts
    slope = slope_ref[:, 0:1]
    k_t = cache_ref[0].reshape(width, past)
    v_t = cache_ref[1].reshape(width, past)
    new = jnp.concatenate([kvn_ref[...], jnp.zeros((LANES - ts, 2 * width), F32)], axis=0)
    qbd = _head_block_diag(q_ref[...] * (HEAD_DIM ** -0.5), n_heads, HEAD_DIM, ts).astype(BF16)
    t_row = lax.broadcasted_iota(jnp.int32, (rows, 1), 0) % ts
    d_old = (past + t_row) - lax.broadcasted_iota(jnp.int32, (1, past), 1)
    j_new = lax.broadcasted_iota(jnp.int32, (1, LANES), 1)
    d_new = jnp.where(j_new < ts, t_row - j_new, -1)
    c_old, c_new = _dil_count(d_old), _dil_count(d_new)
    s_old = jnp.dot(qbd, k_t.astype(BF16), preferred_element_type=F32)
    s_new = _nt_dot(qbd, new[:, 0:width].astype(BF16))
    s_old = jnp.where(c_old > 0, s_old - slope * d_old.astype(F32), NEG)
    s_new = jnp.where(c_new > 0, s_new - slope * d_new.astype(F32), NEG)
    m = jnp.maximum(jnp.max(s_old, axis=-1, keepdims=True), jnp.max(s_new, axis=-1, keepdims=True))
    p_old = c_old * jnp.exp(s_old - m)
    p_new = c_new * jnp.exp(s_new - m)
    l = jnp.sum(p_old, axis=-1, keepdims=True) + jnp.sum(p_new, axis=-1, keepdims=True)
    o = _nt_dot(p_old.astype(BF16), v_t.astype(BF16)) + jnp.dot(p_new.astype(BF16), new[:, width:].astype(BF16),
                                                               preferred_element_type=F32)
    o_ref[...] = _head_diag_rows(o / l, n_heads, HEAD_DIM, ts)
    both = cache_ref[...].reshape(2 * width, past)
    rolled = pltpu.roll(both, past - ts, axis=1)
    new_t = pltpu.roll(new.T, LANES - ts, axis=1)
    ctx_ref[:, 0:past - LANES] = rolled[:, 0:past - LANES]
    ctx_ref[:, past - LANES:past] = jnp.where(j_new >= LANES - ts, new_t, rolled[:, past - LANES:past])


def _dil_sample2(q, kv_new, cache_t, n_heads):
    b, ts, width = q.shape
    past = cache_t.shape[-1]
    assert ts % 8 == 0 and past % LANES == 0
    slopes = jnp.broadcast_to(jnp.repeat(_alibi_slopes(n_heads), ts)[:, None], (n_heads * ts, LANES))
    per_b = lambda rows, w: pl.BlockSpec((None, rows, w), lambda i: (i, 0, 0))
    return pl.pallas_call(
        _dil_sample2_kernel,
        grid=(b,),
        in_specs=[_resident(slopes.shape), per_b(ts, width), per_b(ts, 2 * width),
                  pl.BlockSpec((None,) + cache_t.shape[1:], lambda i: (i, 0, 0, 0, 0))],
        out_specs=[per_b(ts, width), per_b(2 * width, past)],
        out_shape=[jax.ShapeDtypeStruct((b, ts, width), F32), jax.ShapeDtypeStruct((b, 2 * width, past), F32)],
        compiler_params=_cparams(1),
        name="dil_sample",
    )(slopes, q, kv_new, cache_t)


def _dil_sample(q, kv_new, cache, n_heads):
    b, ts, width = q.shape
    past = cache.shape[1]
    assert ts % 8 == 0 and past % LANES == 0
    slopes = jnp.broadcast_to(jnp.repeat(_alibi_slopes(n_heads), ts)[:, None], (n_heads * ts, LANES))
    per_b = lambda rows, w: pl.BlockSpec((None, rows, w), lambda i: (i, 0, 0))
    return pl.pallas_call(
        _dil_sample_kernel,
        grid=(b,),
        in_specs=[_resident(slopes.shape), per_b(ts, width), per_b(ts, 2 * width), per_b(past, 2 * width)],
        out_specs=[per_b(ts, width), per_b(past, 2 * width)],
        out_shape=[jax.ShapeDtypeStruct((b, ts, width), F32), jax.ShapeDtypeStruct(cache.shape, F32)],
        scratch_shapes=[pltpu.VMEM((past + LANES, width), BF16)] * 2,
        compiler_params=_cparams(1),
        name="dil_sample",
    )(slopes, q, kv_new, cache)


def _compress_pool_kernel(pool_ref, w_ref, pe_ref, o_ref, *slabs):
    pages, _, n_groups, hd, page_rows = pool_ref.shape
    pairs = n_groups * hd // LANES
    chunks = page_rows // CMP_STRIDE
    out_row = lax.broadcasted_iota(jnp.int32, (page_rows, page_rows), 0)
    src_row = (out_row % chunks) * CMP_STRIDE + out_row // chunks
    perm = (src_row == lax.broadcasted_iota(jnp.int32, (page_rows, page_rows), 1)).astype(BF16)

    batch = 8

    def to_rows(pb, carry):
        for i in range(batch):
            p = pb * batch + i
            for c in range(2):
                for gp in range(pairs):
                    tile_t = pool_ref[p, c, pl.ds(gp * (LANES // hd), LANES // hd)].reshape(LANES, page_rows)
                    rows = _nt_dot(perm, tile_t.astype(BF16))
                    for s in range(CMP_STRIDE):
                        slabs[c * pairs + gp][s, pl.ds(pl.multiple_of(p * chunks, chunks), chunks), :] = (
                            rows[s * chunks:(s + 1) * chunks, :])
        return carry

    assert pages % batch == 0 and chunks == 8
    lax.fori_loop(0, pages // batch, to_rows, 0)
    _compress_kernel(*slabs, w_ref, pe_ref, o_ref, n_ch=pages * chunks, merged=False, by_row=True)


def _compress_pool(pool, wbd, pe, pages_per_step):
    n_pages, _, n_groups, hd, page_rows = pool.shape
    width = 2 * n_groups * hd
    assert n_pages % pages_per_step == 0 and page_rows % (8 * CMP_STRIDE) == 0 and page_rows == LANES
    n_ch = pages_per_step * page_rows // CMP_STRIDE
    return pl.pallas_call(
        _compress_pool_kernel,
        grid=(n_pages // pages_per_step,),
        in_specs=[pl.BlockSpec((pages_per_step,) + pool.shape[1:], lambda i: (i, 0, 0, 0, 0)),
                  _resident(wbd.shape), _resident(pe.shape)],
        out_specs=pl.BlockSpec((n_ch, 2 * width), lambda i: (i, 0)),
        out_shape=jax.ShapeDtypeStruct((n_pages * page_rows // CMP_STRIDE, 2 * width), F32),
        scratch_shapes=[pltpu.VMEM((CMP_STRIDE, n_ch, LANES), F32)] * (width // LANES),
        compiler_params=_cparams(1),
        name="nsa_compress_pool",
    )(pool, wbd, pe)


def _sample_query(q_ref, ts, n_groups):
    gw = n_groups * HEAD_DIM
    n_rep = q_ref.shape[1] // gw
    return jnp.concatenate(
        [_head_block_diag(q_ref[:, r * gw:(r + 1) * gw] * (HEAD_DIM ** -0.5), n_groups, HEAD_DIM, ts) for r in range(n_rep)],
        axis=0).astype(BF16)


def _sample_rows_out(o, ts, n_groups):
    per_r = n_groups * ts
    return [_head_diag_rows(o[r * per_r:(r + 1) * per_r, :], n_groups, HEAD_DIM, ts) for r in range(o.shape[0] // per_r)]


def _nsa_sample_cmp_kernel(pt_ref, slope_ref, c2s_ref, q_ref, kvn_ref, cwin_ref, *rest, pages, past, n_c, n_sel):
    del pt_ref
    ab_refs = rest[:pages]
    xcmp_ref, xwin_ref, sel_ref, winout_ref, ab_scr, kwin_scr, vwin_scr = rest[pages:]
    j = pl.program_id(1)
    ts = q_ref.shape[0]
    n_groups = kvn_ref.shape[1] // (2 * HEAD_DIM)
    gw = n_groups * HEAD_DIM
    for k in range(pages):
        ab_scr[pl.ds(pl.multiple_of((j * pages + k) * 8, 8), 8), :] = ab_refs[k][...]

    @pl.when(j == pl.num_programs(1) - 1)
    def _():
        n_ch = ab_scr.shape[0]
        n_rows = slope_ref.shape[0]
        slope = slope_ref[:, 0:1]
        qbd = _sample_query(q_ref, ts, n_groups)
        t_row = lax.broadcasted_iota(jnp.int32, (n_rows, 1), 0) % ts
        qp = past + t_row
        kvc = ab_scr[:, 0:2 * gw] + pltpu.roll(ab_scr[:, 2 * gw:4 * gw], n_ch - 1, axis=0)
        n_idx = lax.broadcasted_iota(jnp.int32, (1, n_ch), 1)
        cmp_end = n_idx * CMP_STRIDE + (CMP_LEN - 1)
        valid = (cmp_end <= qp) & (n_idx < n_c)
        s = jnp.where(valid, _nt_dot(qbd, kvc[:, 0:gw].astype(BF16)) - slope * (qp - cmp_end).astype(F32), NEG)
        m = jnp.max(s, axis=-1, keepdims=True)
        p = jnp.where(valid, jnp.exp(s - m), 0.0)
        p = (p / jnp.maximum(jnp.sum(p, axis=-1, keepdims=True), 1e-30)).astype(BF16)
        for r, x in enumerate(_sample_rows_out(jnp.dot(p, kvc[:, gw:2 * gw].astype(BF16), preferred_element_type=F32), ts, n_groups)):
            xcmp_ref[r * ts:(r + 1) * ts, :] = x
        imp_rows = jnp.dot(p, c2s_ref[...], preferred_element_type=F32)
        gt = n_groups * ts
        imp = jnp.sum(imp_rows.reshape(n_rows // gt, gt, imp_rows.shape[1]), axis=0)
        blk = lax.broadcasted_iota(jnp.int32, (1, imp.shape[1]), 1)
        cb = (past + lax.broadcasted_iota(jnp.int32, (gt, 1), 0) % ts) // SEL_BLOCK
        forced = (blk == 0) | (blk == cb) | (blk == cb - 1)
        score = jnp.where((blk <= cb) & (blk < n_sel), imp + FORCE_BONUS * forced.astype(F32), NEG)
        removed = -3e38
        cur = jnp.where(blk < n_sel, score, removed)
        chosen = jnp.zeros(score.shape, F32)
        for _ in range(SEL_TOP):
            top = jnp.max(cur, axis=-1, keepdims=True)
            first = jnp.min(jnp.where(cur == top, blk, imp.shape[1]), axis=-1, keepdims=True)
            pick = blk == first
            chosen = jnp.where(pick & (top > 0.5 * NEG), 1.0, chosen)
            cur = jnp.where(pick, removed, cur)
        n_steps = sel_ref.shape[0]
        per_step = (past // SEL_BLOCK) // n_steps
        lane = lax.broadcasted_iota(jnp.int32, (1, LANES), 1)
        for st in range(n_steps):
            shifted = chosen if st == 0 else pltpu.roll(chosen, chosen.shape[1] - st * per_step, axis=1)
            n_here = per_step + (n_sel - n_steps * per_step if st == n_steps - 1 else 0)
            piece = jnp.where(lane < n_here, shifted[:, 0:LANES], 0.0)
            sel_ref[st] = jnp.concatenate([piece] * (n_rows // gt), axis=0).astype(BF16)
        w_past = cwin_ref.shape[0]
        pad = kwin_scr.shape[0] - w_past
        kwin_scr[0:w_past, :] = cwin_ref[:, 0:gw].astype(BF16)
        vwin_scr[0:w_past, :] = cwin_ref[:, gw:2 * gw].astype(BF16)
        zeros = jnp.zeros((pad - ts, gw), F32)
        kwin_scr[w_past:, :] = jnp.concatenate([kvn_ref[:, 0:gw], zeros], axis=0).astype(BF16)
        vwin_scr[w_past:, :] = jnp.concatenate([kvn_ref[:, gw:2 * gw], zeros], axis=0).astype(BF16)
        dist = (w_past + t_row) - lax.broadcasted_iota(jnp.int32, (1, kwin_scr.shape[0]), 1)
        ok = (dist >= 0) & (dist <= WIN)
        s = jnp.where(ok, _nt_dot(qbd, kwin_scr[...]) - slope * dist.astype(F32), NEG)
        m = jnp.max(s, axis=-1, keepdims=True)
        p = jnp.where(ok, jnp.exp(s - m), 0.0)
        l = jnp.maximum(jnp.sum(p, axis=-1, keepdims=True), 1e-30)
        o = jnp.dot(p.astype(BF16), vwin_scr[...], preferred_element_type=F32) / l
        for r, x in enumerate(_sample_rows_out(o, ts, n_groups)):
            xwin_ref[r * ts:(r + 1) * ts, :] = x
        winout_ref[0:w_past - ts, :] = cwin_ref[ts:w_past, :]
        winout_ref[w_past - ts:w_past, :] = kvn_ref[...]


def _sample_slopes(n_groups, n_rep, ts):
    slopes = _alibi_slopes(n_groups * n_rep).reshape(n_groups, n_rep).T
    return jnp.broadcast_to(jnp.repeat(slopes.reshape(-1), ts)[:, None], (n_rep * n_groups * ts, LANES))


def _nsa_sample_cmp(q, kv_win_new, cache_win, ab_pool, page_table, n_groups, sel_steps, pages):
    b, ts, qw = q.shape
    n_rep = qw // HEAD_DIM // n_groups
    gw = n_groups * HEAD_DIM
    n_pages = page_table.shape[1]
    chunks_per_page = ab_pool.shape[1]
    past = n_pages * chunks_per_page * CMP_STRIDE
    n_ch = (past + ts) // CMP_STRIDE
    assert n_ch == n_pages * chunks_per_page and n_pages % pages == 0
    n_c = n_ch - CMP_LEN // CMP_STRIDE + 1
    n_sel = -(-(past + ts) // SEL_BLOCK)
    assert n_sel <= 2 * LANES and n_sel - past // SEL_BLOCK + past // SEL_BLOCK // sel_steps <= LANES
    n_rows = n_rep * n_groups * ts
    slopes = _sample_slopes(n_groups, n_rep, ts)
    cs = jnp.arange(n_ch)[:, None] * CMP_STRIDE
    ss = jnp.arange(2 * LANES)[None, :] * SEL_BLOCK
    ov = jnp.maximum(jnp.minimum(cs + CMP_LEN, ss + SEL_BLOCK) - jnp.maximum(cs, ss), 0).astype(F32) / CMP_STRIDE
    c2s = jnp.where((jnp.arange(n_ch)[:, None] < n_c) & (jnp.arange(2 * LANES)[None, :] < n_sel), ov, 0.0).astype(BF16)
    w_past = cache_win.shape[1]
    per_b = lambda rows, w: pl.BlockSpec((None, rows, w), lambda i, j, pt: (i, 0, 0))
    page_spec = lambda k: pl.BlockSpec((None, chunks_per_page, ab_pool.shape[2]),
                                       lambda i, j, pt: (pt[i, j * pages + k], 0, 0))
    const = lambda shape: pl.BlockSpec(shape, lambda i, j, pt: (0,) * len(shape), pipeline_mode=pl.Buffered(1))
    return pl.pallas_call(
        functools.partial(_nsa_sample_cmp_kernel, pages=pages, past=past, n_c=n_c, n_sel=n_sel),
        grid_spec=pltpu.PrefetchScalarGridSpec(
            num_scalar_prefetch=1,
            grid=(b, n_pages // pages),
            in_specs=[const(slopes.shape), const(c2s.shape), per_b(ts, qw), per_b(ts, 2 * gw), per_b(w_past, 2 * gw)]
                     + [page_spec(k) for k in range(pages)],
            out_specs=[per_b(n_rep * ts, gw), per_b(n_rep * ts, gw),
                       pl.BlockSpec((None, sel_steps, n_rows, LANES), lambda i, j, pt: (i, 0, 0, 0)),
                       per_b(w_past, 2 * gw)],
            scratch_shapes=[pltpu.VMEM((n_ch, ab_pool.shape[2]), F32),
                            pltpu.VMEM((w_past + LANES, gw), BF16), pltpu.VMEM((w_past + LANES, gw), BF16)]),
        out_shape=[jax.ShapeDtypeStruct((b, n_rep * ts, gw), F32), jax.ShapeDtypeStruct((b, n_rep * ts, gw), F32),
                   jax.ShapeDtypeStruct((b, sel_steps, n_rows, LANES), BF16),
                   jax.ShapeDtypeStruct(cache_win.shape, F32)],
        compiler_params=_cparams(2),
        name="nsa_sample_cmp",
    )(page_table, slopes, c2s, q, kv_win_new, cache_win, *([ab_pool] * pages))


def _nsa_sample_sel_kernel(pt_ref, slope_ref, exp_ref, q_ref, kvn_ref, sel_ref, xcmp_ref, xwin_ref, gate_ref, *rest,
                           pages, past):
    del pt_ref
    page_refs = rest[:pages]
    o_ref, m_scr, l_scr, acc_scr, new_scr = rest[pages:]
    j = pl.program_id(1)
    ts = q_ref.shape[0]
    n_groups = kvn_ref.shape[1] // (2 * HEAD_DIM)
    gw = n_groups * HEAD_DIM
    n_rows = slope_ref.shape[0]
    page_rows = page_refs[0].shape[-1]
    tk = pages * page_rows
    slope = slope_ref[:, 0:1]
    qbd = _sample_query(q_ref, ts, n_groups)
    t_row = lax.broadcasted_iota(jnp.int32, (n_rows, 1), 0) % ts

    @pl.when(j == 0)
    def _():
        m_scr[...] = jnp.full(m_scr.shape, NEG, F32)
        l_scr[...] = jnp.zeros(l_scr.shape, F32)
        acc_scr[...] = jnp.zeros(acc_scr.shape, F32)

    def update(k, v, ok, dist, transposed):
        qk = jnp.dot(qbd, k, preferred_element_type=F32) if transposed else _nt_dot(qbd, k)
        s = jnp.where(ok, qk - slope * dist.astype(F32), NEG)
        m_old = m_scr[:, 0:1]
        m_new = jnp.maximum(m_old, jnp.max(s, axis=-1, keepdims=True))
        alpha = jnp.exp(m_old - m_new)
        p = jnp.where(ok, jnp.exp(s - m_new), 0.0)
        pv = _nt_dot(p.astype(BF16), v) if transposed else jnp.dot(p.astype(BF16), v, preferred_element_type=F32)
        l_scr[...] = jnp.broadcast_to(alpha * l_scr[:, 0:1] + jnp.sum(p, axis=-1, keepdims=True), l_scr.shape)
        acc_scr[...] = alpha * acc_scr[...] + pv
        m_scr[...] = jnp.broadcast_to(m_new, m_scr.shape)

    sel = sel_ref[j]
    k = jnp.concatenate([ref[0].reshape(gw, page_rows) for ref in page_refs], axis=1).astype(BF16)
    v = jnp.concatenate([ref[1].reshape(gw, page_rows) for ref in page_refs], axis=1).astype(BF16)
    picked = _nt_dot(sel, exp_ref[0:tk, :]) > 0.5
    kpos = j * tk + lax.broadcasted_iota(jnp.int32, (1, tk), 1)
    update(k, v, picked, (past + t_row) - kpos, True)

    @pl.when(j == pl.num_programs(1) - 1)
    def _():
        pad = new_scr.shape[0]
        new_scr[...] = jnp.concatenate([kvn_ref[...], jnp.zeros((pad - ts, 2 * gw), F32)], axis=0).astype(BF16)
        i_new = lax.broadcasted_iota(jnp.int32, (1, pad), 1)
        ok = (_nt_dot(sel, exp_ref[tk:tk + pad, :]) > 0.5) & (i_new <= t_row) & (i_new < ts)
        update(new_scr[:, 0:gw], new_scr[:, gw:2 * gw], ok, t_row - i_new, False)
        o = acc_scr[...] / jnp.maximum(l_scr[:, 0:1], 1e-30)
        n_rep = n_rows // (n_groups * ts)
        for r, x_sel in enumerate(_sample_rows_out(o, ts, n_groups)):
            g_cmp, g_sel, g_win = (1.0 / (1.0 + jnp.exp(-gate_ref[:, (i * n_rep + r) * gw:(i * n_rep + r + 1) * gw]))
                                   for i in range(3))
            rows = slice(r * ts, (r + 1) * ts)
            o_ref[:, r * gw:(r + 1) * gw] = g_cmp * xcmp_ref[rows, :] + g_sel * x_sel + g_win * xwin_ref[rows, :]


def _nsa_sample_sel(q, kv_sel_new, sel, x_cmp, x_win, gates, pool, page_table, n_groups, pages):
    b, ts, qw = q.shape
    n_rep = qw // HEAD_DIM // n_groups
    gw = n_groups * HEAD_DIM
    n_pages = page_table.shape[1]
    page_rows = pool.shape[-1]
    past = n_pages * page_rows
    n_steps = n_pages // pages
    tk = pages * page_rows
    n_rows = n_rep * n_groups * ts
    assert sel.shape[1] == n_steps and tk % SEL_BLOCK == 0 and tk // SEL_BLOCK < LANES
    slopes = _sample_slopes(n_groups, n_rep, ts)
    blk_of = jnp.concatenate([jnp.arange(tk) // SEL_BLOCK, jnp.full((LANES,), tk // SEL_BLOCK)])
    expand = (blk_of[:, None] == jnp.arange(LANES)[None, :]).astype(BF16)
    per_b = lambda rows, w: pl.BlockSpec((None, rows, w), lambda i, j, pt: (i, 0, 0))
    page_spec = lambda k: pl.BlockSpec((None,) + pool.shape[1:], lambda i, j, pt: (pt[i, j * pages + k], 0, 0, 0, 0))
    const = lambda shape: pl.BlockSpec(shape, lambda i, j, pt: (0,) * len(shape), pipeline_mode=pl.Buffered(1))
    return pl.pallas_call(
        functools.partial(_nsa_sample_sel_kernel, pages=pages, past=past),
        grid_spec=pltpu.PrefetchScalarGridSpec(
            num_scalar_prefetch=1,
            grid=(b, n_steps),
            in_specs=[const(slopes.shape), const(expand.shape), per_b(ts, qw), per_b(ts, 2 * gw),
                      pl.BlockSpec((None, n_steps, n_rows, LANES), lambda i, j, pt: (i, 0, 0, 0)),
                      per_b(n_rep * ts, gw), per_b(n_rep * ts, gw), per_b(ts, 3 * qw)]
                     + [page_spec(k) for k in range(pages)],
            out_specs=per_b(ts, qw),
            scratch_shapes=[pltpu.VMEM((n_rows, LANES), F32), pltpu.VMEM((n_rows, LANES), F32),
                            pltpu.VMEM((n_rows, gw), F32), pltpu.VMEM((LANES, 2 * gw), BF16)]),
        out_shape=jax.ShapeDtypeStruct((b, ts, qw), F32),
        compiler_params=_cparams(2),
        name="nsa_sample_sel",
    )(page_table, slopes, expand, q, kv_sel_new, sel, x_cmp, x_win, gates, *([pool] * pages))


def _masked_softmax(s, mask):
    s = jnp.where(mask, s, NEG)
    m = jnp.max(s, axis=-1, keepdims=True)
    p = jnp.where(mask, jnp.exp(s - m), 0.0)
    return p / jnp.maximum(jnp.sum(p, axis=-1, keepdims=True), 1e-30)


def _dil_attn_sample(q, ctx, n_heads):
    t = q.shape[1]
    slopes = _alibi_slopes(n_heads)
    pos = ctx.shape[1] - t + jnp.arange(t)
    lses, outs = [], []
    for window, dil in DIL_PAIRS:
        dist = jnp.arange(window // dil + 1) * dil
        idx = pos[:, None] - dist[None, :]
        kvg = jnp.take(ctx, jnp.maximum(idx, 0), axis=1)
        s = jnp.einsum('bthd,btkhd->bhtk', q, kvg[:, :, :, 0], preferred_element_type=F32) * HEAD_DIM ** -0.5
        s = s - slopes[:, None, None] * dist.astype(F32)
        s = jnp.where(idx >= 0, s, NEG)
        m = jnp.max(s, axis=-1, keepdims=True)
        p = jnp.exp(s - m)
        l = jnp.sum(p, axis=-1, keepdims=True)
        outs.append(jnp.einsum('bhtk,btkhd->bthd', p / l, kvg[:, :, :, 1]))
        lses.append(jnp.swapaxes((m + jnp.log(l))[..., 0], 1, 2))
    wts = jax.nn.softmax(jnp.stack(lses), axis=0)[..., None]
    return jnp.sum(wts * jnp.stack(outs), axis=0)


def _nsa_sample(q, kv_cmp, kv_sel, kv_win, gate_logits, past_cmp, past_sel, win_buf, w_cmp, pe_cmp):
    b, t, g, r, _ = q.shape
    scale = HEAD_DIM ** -0.5
    slopes = _alibi_slopes(g * r).reshape(g, r)
    gates = jax.nn.sigmoid(gate_logits)
    ctx_cmp = jnp.concatenate([past_cmp, kv_cmp], axis=1)
    ctx_sel = jnp.concatenate([past_sel, kv_sel], axis=1)
    ctx_win = jnp.concatenate([win_buf, kv_win], axis=1)
    tc = ctx_cmp.shape[1]
    qp = (tc - t) + jnp.arange(t)
    n_r = CMP_LEN // CMP_STRIDE
    n_ch = tc // CMP_STRIDE
    n_c = n_ch - n_r + 1
    chunks = ctx_cmp[:, :n_ch * CMP_STRIDE].reshape((b, n_ch, CMP_STRIDE) + ctx_cmp.shape[2:])
    w = w_cmp.reshape(n_r, CMP_STRIDE, 2, HEAD_DIM, HEAD_DIM)
    kvc = jnp.einsum('lcd,lcde->ce', pe_cmp, w_cmp)[None, None, :, None, :]
    for i in range(n_r):
        kvc = kvc + jnp.einsum('bnscgd,scde->bncge', chunks[:, i:i + n_c], w[i])
    cmp_end = jnp.arange(n_c) * CMP_STRIDE + CMP_LEN - 1
    kc, vc = kvc[:, :, 0], kvc[:, :, 1]
    n_sel = -(-tc // SEL_BLOCK)
    n_top = min(SEL_TOP, n_sel)
    sel_blocks = jnp.pad(ctx_sel, ((0, 0), (0, n_sel * SEL_BLOCK - tc), (0, 0), (0, 0), (0, 0)))
    sel_blocks = sel_blocks.reshape(b, n_sel, SEL_BLOCK, 2, g, HEAD_DIM).transpose(0, 4, 1, 2, 3, 5)
    cs = jnp.arange(n_c)[:, None] * CMP_STRIDE
    ss = jnp.arange(n_sel)[None, :] * SEL_BLOCK
    cmp2sel = jnp.maximum(jnp.minimum(cs + CMP_LEN, ss + SEL_BLOCK) - jnp.maximum(cs, ss), 0).astype(F32) / CMP_STRIDE
    blk = jnp.arange(n_sel)
    s = jnp.einsum('btgrd,bngd->bgrtn', q, kc, preferred_element_type=F32) * scale
    s = s - slopes[:, :, None, None] * (qp[:, None] - cmp_end[None, :]).astype(F32)
    p = _masked_softmax(s, cmp_end[None, :] <= qp[:, None])
    o_cmp = jnp.einsum('bgrtn,bngd->btgrd', p, vc)
    imp = jnp.einsum('bgrtn,nj->bgtj', p, cmp2sel)
    cb = qp // SEL_BLOCK
    blk_ok = blk[None, :] <= cb[:, None]
    forced = (blk[None, :] == 0) | (blk[None, :] == cb[:, None]) | (blk[None, :] == cb[:, None] - 1)
    score = jnp.where(blk_ok, imp + FORCE_BONUS * forced.astype(F32), NEG)
    top_s, top_i = lax.top_k(score, n_top)
    picked = jnp.sum((top_i[..., None] == blk) & (top_s > 0.5 * NEG)[..., None], axis=-2) > 0
    kpos = jnp.arange(tc)
    ok = picked[..., kpos // SEL_BLOCK] & (kpos <= qp[:, None])
    s2 = jnp.einsum('btgrd,bkgd->bgrtk', q, ctx_sel[:, :, 0], preferred_element_type=F32) * scale
    s2 = s2 - slopes[None, :, :, None, None] * (qp[:, None] - kpos).astype(F32)
    p2 = _masked_softmax(s2, ok[:, :, None])
    o_sel = jnp.einsum('bgrtk,bkgd->btgrd', p2, ctx_sel[:, :, 1])
    tw = ctx_win.shape[1]
    dw = (tw - t + jnp.arange(t))[:, None] - jnp.arange(tw)[None, :]
    okw = (dw >= 0) & (dw <= WIN)
    s3 = jnp.einsum('btgrd,bsgd->bgrts', q, ctx_win[:, :, 0], preferred_element_type=F32) * scale
    s3 = s3 - slopes[:, :, None, None] * dw.astype(F32)
    p3 = _masked_softmax(s3, okw)
    o_win = jnp.einsum('bgrts,bsgd->btgrd', p3, ctx_win[:, :, 1])
    o = gates[..., 0:1] * o_cmp + gates[..., 1:2] * o_sel + gates[..., 2:3] * o_win
    return o, ctx_win[:, t:]


def _gate_columns(w_gate, n_groups):
    d = w_gate.shape[0]
    per_pair = w_gate.reshape(d, n_groups // 2, 2 * 4 * 3)
    return jnp.pad(per_pair, ((0, 0), (0, 0), (0, LANES - 2 * 4 * 3))).reshape(d, -1)


def kernel(x_prompt, x_sample, cache_dil_kv, state_conv, state_rnn, cache_win_kv, cache_cmp_kv, cache_sel_kv, page_table, norm_mix, norm_ffn, norm_out, w_in_ab, w_out_ab, conv_w, conv_b, gate_a_w, gate_a_b, gate_x_w, gate_x_b, lru_lambda, w_in_c, w_out_c, w_cmp, pe_cmp, ffn_w1, ffn_w3, ffn_w2):
    bp, t, d = x_prompt.shape
    bs, ts, _ = x_sample.shape
    depth = norm_mix.shape[0]
    bf = lambda z: z.astype(BF16)
    tm_p, tm_s = 512, bs * ts
    yp = x_prompt.reshape(bp * t, d)
    ys = x_sample.reshape(bs * ts, d)
    outs = {k: [] for k in ("dil_p", "dil_s", "conv_p", "conv_s", "rnn_p", "rnn_s",
                            "win_p", "win_s", "cmp_p", "cmp_s", "sel_p", "sel_s")}
    for layer in range(depth):
        li = layer // 2
        last = layer == depth - 1
        ffn = (norm_ffn[layer], bf(ffn_w1[layer]), bf(ffn_w3[layer]), bf(ffn_w2[layer]), norm_out if last else None)
        if layer % 2 == 0:
            rw = conv_w.shape[2]
            aw = (w_in_ab.shape[2] - 2 * rw) // 3
            n_heads = aw // HEAD_DIM
            w_in = bf(w_in_ab[li])
            splits = (aw, 2 * aw, rw, rw)
            wa, wx = bf(_block_diag(gate_a_w[li])), bf(_block_diag(gate_x_w[li]))
            lru = (conv_w[li], conv_b[li], wa, gate_a_b[li], wx, gate_x_b[li], lru_lambda[li])
            w_outs = bf(w_out_ab[li])
            keep = min(DIL_MAX, t)
            as_cache = lambda z, groups, rows: jnp.transpose(z.reshape(bp, 2, groups, HEAD_DIM, rows), (0, 4, 1, 2, 3))
            q, kv, xr, gate, kv_t = _norm_proj(yp, norm_mix[layer], w_in, splits, tm_p,
                                               rows_minor=[(w_in[:, aw:3 * aw].T, keep)], seq=t)
            o_att = _dil_attn(q.reshape(bp, t, aw), kv.reshape(bp, t, 2 * aw), n_heads)
            o_rnn, conv_new, h_new = _rglru(xr.reshape(bp, t, rw), gate.reshape(bp, t, rw),
                                            jnp.zeros((bp, CONV_W - 1, rw), F32), jnp.zeros((bp, rw), F32),
                                            *lru, bp, 256)
            yp = _out_ffn(yp, [o_att.reshape(bp * t, aw), o_rnn.reshape(bp * t, rw)], w_outs, *ffn, tm_p)
            outs["dil_p"].append(as_cache(kv_t, n_heads, keep))
            outs["conv_p"].append(conv_new)
            outs["rnn_p"].append(h_new)
            q, kv, xr, gate = _norm_proj(ys, norm_mix[layer], w_in, splits, tm_s)
            cache = cache_dil_kv[li]
            o_att, ctx_new = _dil_sample2(q.reshape(bs, ts, aw), kv.reshape(bs, ts, 2 * aw),
                                          jnp.transpose(cache, (0, 2, 3, 4, 1)), n_heads)
            ctx_new = jnp.transpose(ctx_new.reshape(bs, 2, n_heads, HEAD_DIM, cache.shape[1]), (0, 4, 1, 2, 3))
            o_rnn, conv_new, h_new = _rglru(xr.reshape(bs, ts, rw), gate.reshape(bs, ts, rw),
                                            state_conv[li], state_rnn[li], *lru, 8, ts)
            ys = _out_ffn(ys, [o_att.reshape(bs * ts, aw), o_rnn.reshape(bs * ts, rw)], w_outs, *ffn, tm_s)
            outs["dil_s"].append(ctx_new.reshape(cache.shape))
            outs["conv_s"].append(conv_new)
            outs["rnn_s"].append(h_new)
        else:
            n_groups = cache_win_kv.shape[4]
            kvw = 2 * n_groups * HEAD_DIM
            qw = w_out_c.shape[1]
            rep = qw // HEAD_DIM // n_groups
            w_in = bf(jnp.concatenate([w_in_c[li][:, :qw + 3 * kvw],
                                       _gate_columns(w_in_c[li][:, qw + 3 * kvw:], n_groups)], axis=1))
            gw = n_groups // 2 * LANES
            splits = (qw, kvw, kvw, kvw, gw)
            wbd, pe = _compress_weights(w_cmp[li], pe_cmp[li], n_groups)
            w_outs = bf(w_out_c[li])
            kv5 = lambda z, n: z.reshape(n, -1, 2, n_groups, HEAD_DIM)
            keep = min(WIN, t)
            as_cache = lambda z, rows: jnp.transpose(z.reshape(bp, 2, n_groups, HEAD_DIM, rows), (0, 4, 1, 2, 3))
            kv_t = lambda i: w_in[:, qw + i * kvw:qw + (i + 1) * kvw].T
            q, kv_cmp, kv_sel, kv_win, gates, cmp_t, sel_t, win_t = _norm_proj(
                yp, norm_mix[layer], w_in, splits, tm_p, rows_minor=[(kv_t(0), t), (kv_t(1), t), (kv_t(2), keep)], seq=t)
            o = _nsa_prompt(q.reshape(bp, t, qw), kv_cmp.reshape(bp, t, kvw), kv_sel.reshape(bp, t, kvw),
                            kv_win.reshape(bp, t, kvw), gates.reshape(bp, t, gw), wbd, pe, n_groups)
            yp = _out_ffn(yp, [o.reshape(bp * t, qw)], w_outs, *ffn, tm_p)
            outs["win_p"].append(as_cache(win_t, keep))
            outs["cmp_p"].append(as_cache(cmp_t, t))
            outs["sel_p"].append(as_cache(sel_t, t))
            wq = w_in_c[li][:, :qw].reshape(d, n_groups, rep, HEAD_DIM).transpose(0, 2, 1, 3).reshape(d, qw)
            wg = w_in_c[li][:, qw + 3 * kvw:].reshape(d, n_groups, rep, 3).transpose(0, 3, 2, 1)
            wg = jnp.broadcast_to(wg[..., None], wg.shape + (HEAD_DIM,)).reshape(d, 3 * qw)
            w_in_s = bf(jnp.concatenate([wq, w_in_c[li][:, qw:qw + 3 * kvw], wg], axis=1))
            w_out_s = bf(w_out_c[li].reshape(n_groups, rep, HEAD_DIM, d).transpose(1, 0, 2, 3).reshape(qw, d))
            q, kv_cmp, kv_sel, kv_win, gates = _norm_proj(ys, norm_mix[layer], w_in_s, (qw, kvw, kvw, kvw, 3 * qw), tm_s)
            n_phys, page_rows = cache_cmp_kv.shape[1], cache_cmp_kv.shape[2]
            rows_minor = lambda pool: jnp.transpose(pool, (0, 2, 3, 4, 1))
            ab_pool = _compress_pool(rows_minor(cache_cmp_kv[li]), wbd, pe, 32)
            ab_pool = ab_pool.reshape(n_phys, page_rows // CMP_STRIDE, 2 * kvw)
            cwin = cache_win_kv[li]
            x_cmp, x_win, sel, win_new = _nsa_sample_cmp(q.reshape(bs, ts, qw), kv_win.reshape(bs, ts, kvw),
                                                         cwin.reshape(bs, cwin.shape[1], kvw), ab_pool, page_table,
                                                         n_groups, 8, 16)
            o = _nsa_sample_sel(q.reshape(bs, ts, qw), kv_sel.reshape(bs, ts, kvw), sel, x_cmp, x_win,
                                gates.reshape(bs, ts, 3 * qw), rows_minor(cache_sel_kv[li]), page_table, n_groups, 8)
            ys = _out_ffn(ys, [o.reshape(bs * ts, qw)], w_out_s, *ffn, tm_s)
            outs["win_s"].append(win_new.reshape(cwin.shape))
            outs["cmp_s"].append(kv5(kv_cmp, bs))
            outs["sel_s"].append(kv5(kv_sel, bs))
    st = lambda k: jnp.stack(outs[k])
    return (yp.reshape(bp, t, d), ys.reshape(bs, ts, d), st("dil_p"), st("dil_s"), st("conv_p"), st("conv_s"),
            st("rnn_p"), st("rnn_s"), st("win_p"), st("win_s"), st("cmp_p"), st("cmp_s"), st("sel_p"), st("sel_s"))
```

```python
import functools

import jax
import jax.numpy as jnp
from jax import lax
from jax.experimental import pallas as pl
from jax.experimental.pallas import tpu as pltpu

HEAD_DIM = 64
LANES = 128
DIL_PAIRS = ((128, 1), (512, 4), (2048, 16))
DIL_MAX = 2048
DIL_KEYS = 128
CONV_W = 4
LRU_C = 8.0
CMP_LEN = 32
CMP_STRIDE = 16
SEL_BLOCK = 64
SEL_TOP = 16
WIN = 512
NEG = -1e30
FORCE_BONUS = 1e3
EPS = 1e-6
ALIBI_LANES = 6
MASK_BIG = 2.0 ** 100
VMEM_LIMIT = 56 * 1024 * 1024

F32 = jnp.float32
BF16 = jnp.bfloat16


def _cparams(n_grid):
    return pltpu.CompilerParams(dimension_semantics=("arbitrary",) * n_grid,
                                vmem_limit_bytes=VMEM_LIMIT)


def _resident(shape):
    return pl.BlockSpec(shape, lambda *_: (0,) * len(shape), pipeline_mode=pl.Buffered(1))


def _rms(x, g):
    return x * lax.rsqrt(jnp.mean(x * x, axis=-1, keepdims=True) + EPS) * g


def _alibi_slopes(n):
    return 2.0 ** (-8.0 * jnp.arange(1, n + 1, dtype=F32) / n)


def _lane_lo():
    return lax.broadcasted_iota(jnp.int32, (1, LANES), 1) < HEAD_DIM


def _nt_dot(a, b):
    return lax.dot_general(a, b, (((1,), (1,)), ((), ())), preferred_element_type=F32)


def _norm_proj_kernel(x_ref, g_ref, w_ref, *refs, n_out, first_blocks, blocks_per_seq):
    wt_refs = refs[:len(first_blocks)]
    out_refs = refs[len(first_blocks):len(first_blocks) + n_out]
    ot_refs = refs[len(first_blocks) + n_out:]
    h = _rms(x_ref[...], g_ref[...]).astype(BF16)
    off = 0
    for o_ref in out_refs:
        n = o_ref.shape[-1]
        o_ref[...] = jnp.dot(h, w_ref[:, off:off + n], preferred_element_type=F32)
        off += n
    for wt_ref, ot_ref, first in zip(wt_refs, ot_refs, first_blocks):
        @pl.when(pl.program_id(0) % blocks_per_seq >= first)
        def _(wt_ref=wt_ref, ot_ref=ot_ref):
            ot_ref[...] = _nt_dot(wt_ref[...], h)


def _norm_proj(x, g, w, splits, tm, rows_minor=(), seq=None):
    n, d = x.shape
    assert n % tm == 0 and sum(splits) == w.shape[1]
    blocks_per_seq = (seq or n) // tm
    first_blocks = []
    t_specs, t_shapes = [], []
    for w_t, keep in rows_minor:
        assert seq % tm == 0 and keep % tm == 0
        first = (seq - keep) // tm
        first_blocks.append(first)
        t_specs.append(pl.BlockSpec((None, w_t.shape[0], tm), functools.partial(
            lambda first, i: (i // blocks_per_seq, 0, jnp.maximum(i % blocks_per_seq - first, 0)), first)))
        t_shapes.append(jax.ShapeDtypeStruct((n // seq, w_t.shape[0], keep), F32))
    return pl.pallas_call(
        functools.partial(_norm_proj_kernel, n_out=len(splits), first_blocks=tuple(first_blocks),
                          blocks_per_seq=blocks_per_seq),
        grid=(n // tm,),
        in_specs=[pl.BlockSpec((tm, d), lambda i: (i, 0)),
                  _resident((1, d)),
                  _resident(w.shape)] + [_resident(w_t.shape) for w_t, _ in rows_minor],
        out_specs=[pl.BlockSpec((tm, s), lambda i: (i, 0)) for s in splits] + t_specs,
        out_shape=[jax.ShapeDtypeStruct((n, s), F32) for s in splits] + t_shapes,
        compiler_params=_cparams(1),
        name="norm_proj",
    )(x, g.reshape(1, d), w, *[w_t for w_t, _ in rows_minor])


def _out_ffn_kernel(*refs, n_mix, n_chunks, final_norm):
    res_ref = refs[0]
    a_refs = refs[1:1 + n_mix]
    wo_ref, gf_ref, w1_ref, w3_ref, w2_ref = refs[1 + n_mix:6 + n_mix]
    rest = refs[6 + n_mix:]
    go_ref = rest[0] if final_norm else None
    o_ref = rest[-1]
    mix = jnp.concatenate([a_ref[...].astype(BF16) for a_ref in a_refs], axis=1)
    y = res_ref[...] + jnp.dot(mix, wo_ref[...], preferred_element_type=F32)
    h = _rms(y, gf_ref[...]).astype(BF16)
    ch = w1_ref.shape[1] // n_chunks
    for c in range(n_chunks):
        a = jnp.dot(h, w1_ref[:, c * ch:(c + 1) * ch], preferred_element_type=F32)
        b = jnp.dot(h, w3_ref[:, c * ch:(c + 1) * ch], preferred_element_type=F32)
        act = (a * (1.0 / (1.0 + jnp.exp(-a))) * b).astype(BF16)
        y = y + jnp.dot(act, w2_ref[c * ch:(c + 1) * ch, :], preferred_element_type=F32)
    if final_norm:
        y = _rms(y, go_ref[...])
    o_ref[...] = y


def _out_ffn(res, mixes, w_out, g_ffn, w1, w3, w2, g_out, tm):
    n, d = res.shape
    hidden = w1.shape[1]
    n_chunks = hidden // 256
    assert n % tm == 0 and hidden % 256 == 0
    final_norm = g_out is not None
    row = lambda width: pl.BlockSpec((tm, width), lambda i: (i, 0))
    in_specs = [row(d)] + [row(m.shape[1]) for m in mixes] + [_resident(w_out.shape)]
    in_specs += [_resident((1, d)), _resident(w1.shape), _resident(w3.shape), _resident(w2.shape)]
    args = [res, *mixes, w_out, g_ffn.reshape(1, d), w1, w3, w2]
    if final_norm:
        in_specs.append(_resident((1, d)))
        args.append(g_out.reshape(1, d))
    return pl.pallas_call(
        functools.partial(_out_ffn_kernel, n_mix=len(mixes), n_chunks=n_chunks, final_norm=final_norm),
        grid=(n // tm,),
        in_specs=in_specs,
        out_specs=row(d),
        out_shape=jax.ShapeDtypeStruct((n, d), F32),
        compiler_params=_cparams(1),
        name="out_ffn",
    )(*args)


def _dil_attn_kernel(slope_ref, q_ref, k_ref, v_ref, o_ref, *scratch, seq):
    ob_refs, lse_refs = scratch[:3], scratch[3:]
    hp = pl.program_id(1)
    lane_lo = _lane_lo()
    tq, tk = DIL_KEYS, 2 * DIL_KEYS
    head1 = lax.broadcasted_iota(jnp.int32, (2 * tq, 1), 0) >= tq
    slope_rows = jnp.where(head1, slope_ref[pl.ds(2 * hp + 1, 1), 0:1], slope_ref[pl.ds(2 * hp, 1), 0:1])
    iq = lax.broadcasted_iota(jnp.int32, (2 * tq, tk), 0) % tq
    ik = lax.broadcasted_iota(jnp.int32, (2 * tq, tk), 1)
    unroll = 4

    for br, (_, dil) in enumerate(DIL_PAIRS):
        n_blk = seq // dil // tq
        assert (dil * n_blk) % unroll == 0

        def blocks(it, carry, br=br, dil=dil, n_blk=n_blk):
            for u in range(unroll):
                idx = it * unroll + u
                r = idx // n_blk
                blk = idx % n_blk
                a0 = blk * tq
                ka0 = jnp.maximum(blk - 1, 0) * tq
                q_rows = pl.ds(r + dil * a0, tq, stride=dil) if dil > 1 else pl.ds(a0, tq)
                k_rows = pl.ds(r + dil * ka0, tk, stride=dil) if dil > 1 else pl.ds(ka0, tk)
                q = q_ref[q_rows, :] * (HEAD_DIM ** -0.5)
                q2 = jnp.concatenate([jnp.where(lane_lo, q, 0.0), jnp.where(lane_lo, 0.0, q)], axis=0).astype(BF16)
                k = k_ref[k_rows, :].astype(BF16)
                v = v_ref[k_rows, :].astype(BF16)
                da = (a0 - ka0) + iq - ik
                valid = (da >= 0) & (da <= DIL_KEYS)
                s = jnp.where(valid, _nt_dot(q2, k) - slope_rows * (dil * da).astype(F32), NEG)
                m = jnp.max(s, axis=-1, keepdims=True)
                p = jnp.exp(s - m)
                l = jnp.sum(p, axis=-1, keepdims=True)
                out = jnp.dot(p.astype(BF16), v, preferred_element_type=F32) / l
                lse = m + jnp.log(l)
                ob_refs[br][q_rows, :] = jnp.where(lane_lo, out[0:tq], out[tq:2 * tq])
                lse_refs[br][q_rows, :] = jnp.where(lane_lo, lse[0:tq], lse[tq:2 * tq])
            return carry

        lax.fori_loop(0, dil * n_blk // unroll, blocks, 0)

    rows = 512
    def merge(i, carry):
        sl = pl.ds(i * rows, rows)
        ls = [lse_refs[b][sl, :] for b in range(3)]
        m = jnp.maximum(jnp.maximum(ls[0], ls[1]), ls[2])
        es = [jnp.exp(x - m) for x in ls]
        num = es[0] * ob_refs[0][sl, :] + es[1] * ob_refs[1][sl, :] + es[2] * ob_refs[2][sl, :]
        o_ref[sl, :] = num / (es[0] + es[1] + es[2])
        return carry
    lax.fori_loop(0, seq // rows, merge, 0)


def _dil_attn(q, kv, n_heads):
    b, t, width = q.shape
    n_hp = width // LANES
    assert t % (16 * 2 * DIL_KEYS) == 0
    slopes = jnp.broadcast_to(_alibi_slopes(n_heads)[:, None], (n_heads, LANES))
    blk = lambda off: pl.BlockSpec((None, t, LANES), lambda i, j: (i, 0, off + j))
    return pl.pallas_call(
        functools.partial(_dil_attn_kernel, seq=t),
        grid=(b, n_hp),
        in_specs=[_resident((n_heads, LANES)), blk(0), blk(0), blk(n_hp)],
        out_specs=blk(0),
        out_shape=jax.ShapeDtypeStruct((b, t, width), F32),
        scratch_shapes=[pltpu.VMEM((t, LANES), F32)] * 6,
        compiler_params=_cparams(2),
        name="dil_attn",
    )(slopes, q, kv, kv)


def _rglru_kernel(xr_ref, gate_ref, cprev_ref, hprev_ref, cw_ref, cb_ref, wa_ref, ba_ref,
                  wx_ref, bx_ref, lam_ref, y_ref, cnew_ref, hnew_ref,
                  xin_ref, a_ref, u_ref, h_ref):
    ti = pl.program_id(1)
    bb, tc, width = xr_ref.shape
    pad = 8
    tail = CONV_W - 1

    @pl.when(ti == 0)
    def _():
        xin_ref[:, pad - tail:pad, :] = cprev_ref[...]
        h_ref[...] = hprev_ref[...]

    xin_ref[:, pad:pad + tc, :] = xr_ref[...]
    lam = lam_ref[...]
    neg_softplus_c = -LRU_C * (jnp.maximum(-lam, 0.0) + jnp.log1p(jnp.exp(-jnp.abs(lam))))
    for b in range(bb):
        xc = cb_ref[...] + sum(xin_ref[b, pad - tail + k:pad - tail + k + tc, :] * cw_ref[k:k + 1, :]
                               for k in range(CONV_W))
        xcb = xc.astype(BF16)
        ra = jnp.dot(xcb, wa_ref[...], preferred_element_type=F32) + ba_ref[...]
        rx = jnp.dot(xcb, wx_ref[...], preferred_element_type=F32) + bx_ref[...]
        rg = 1.0 / (1.0 + jnp.exp(-ra))
        ig = 1.0 / (1.0 + jnp.exp(-rx))
        log_a = neg_softplus_c * rg
        a = jnp.exp(log_a)
        a_ref[b] = a
        u_ref[b] = jnp.sqrt(jnp.tanh(-log_a) * (1.0 + a * a)) * ig * xc

    def step(t, hs):
        new = []
        for b in range(bb):
            hb = a_ref[b, pl.ds(t, 1), :] * hs[b] + u_ref[b, pl.ds(t, 1), :]
            u_ref[b, pl.ds(t, 1), :] = hb
            new.append(hb)
        return tuple(new)

    hs = lax.fori_loop(0, tc, step, tuple(h_ref[b:b + 1, :] for b in range(bb)), unroll=8)
    for b in range(bb):
        h_ref[b:b + 1, :] = hs[b]
    g = gate_ref[...]
    cdf = 0.5 * (1.0 + jnp.tanh(0.7978845608028654 * (g + 0.044715 * (g * g * g))))
    y_ref[...] = u_ref[...] * (g * cdf)
    xin_ref[:, pad - tail:pad, :] = xin_ref[:, pad + tc - tail:pad + tc, :]
    cnew_ref[...] = xin_ref[:, pad - tail:pad, :]
    hnew_ref[...] = h_ref[...]


def _rglru(xr, gate, conv_prev, h_prev, conv_w, conv_b, wa_bd, ba, wx_bd, bx, lam, bb, tc):
    b, t, width = xr.shape
    assert b % bb == 0 and t % tc == 0 and tc >= CONV_W - 1
    seq_blk = pl.BlockSpec((bb, tc, width), lambda i, j: (i, j, 0))
    vec = _resident((1, width))
    return pl.pallas_call(
        _rglru_kernel,
        grid=(b // bb, t // tc),
        in_specs=[seq_blk, seq_blk,
                  pl.BlockSpec((bb, CONV_W - 1, width), lambda i, j: (i, 0, 0)),
                  pl.BlockSpec((bb, width), lambda i, j: (i, 0)),
                  _resident((CONV_W, width)), vec, _resident(wa_bd.shape), vec,
                  _resident(wx_bd.shape), vec, vec],
        out_specs=[seq_blk,
                   pl.BlockSpec((bb, CONV_W - 1, width), lambda i, j: (i, 0, 0)),
                   pl.BlockSpec((bb, width), lambda i, j: (i, 0))],
        out_shape=[jax.ShapeDtypeStruct((b, t, width), F32),
                   jax.ShapeDtypeStruct((b, CONV_W - 1, width), F32),
                   jax.ShapeDtypeStruct((b, width), F32)],
        scratch_shapes=[pltpu.VMEM((bb, tc + 8, width), F32), pltpu.VMEM((bb, tc, width), F32),
                        pltpu.VMEM((bb, tc, width), F32), pltpu.VMEM((bb, width), F32)],
        compiler_params=_cparams(2),
        name="rglru",
    )(xr, gate, conv_prev, h_prev, conv_w, conv_b.reshape(1, width), wa_bd, ba.reshape(1, width),
      wx_bd, bx.reshape(1, width), lam.reshape(1, width))


def _block_diag(w):
    n, bi, bj = w.shape
    eye = jnp.eye(n, dtype=w.dtype)
    return (eye[:, None, :, None] * w[:, :, None, :]).reshape(n * bi, n * bj)


def _compress_kernel(*refs, n_ch, merged, by_row=False):
    x_refs, (w_ref, pe_ref, o_ref) = refs[:-3], refs[-3:]
    per_half = len(x_refs) // 2
    half = per_half * LANES
    pe_pad = 16
    for c in range(2):
        xs = jnp.concatenate([x_ref[s] if by_row else x_ref[pl.ds(s, n_ch, stride=CMP_STRIDE), :]
                              for s in range(CMP_STRIDE)
                              for x_ref in x_refs[c * per_half:(c + 1) * per_half]], axis=1).astype(BF16)
        acc = []
        for i in range(2):
            pe_rows = jnp.concatenate([pe_ref[i * CMP_STRIDE + s:i * CMP_STRIDE + s + 1, c * half:(c + 1) * half]
                                       for s in range(CMP_STRIDE)], axis=1)
            lhs = jnp.concatenate([xs, jnp.broadcast_to(pe_rows, (pe_pad, pe_rows.shape[1])).astype(BF16)], axis=0)
            acc.append(jnp.dot(lhs, w_ref[i * 2 + c], preferred_element_type=F32))
        pe_acc = acc[0][n_ch:n_ch + 1, :] + acc[1][n_ch:n_ch + 1, :]
        acc = [a[0:n_ch, :] for a in acc]
        if merged:
            o_ref[:, c * half:(c + 1) * half] = acc[0] + pltpu.roll(acc[1], n_ch - 1, axis=0) + pe_acc[0:1, :]
        else:
            o_ref[:, c * half:(c + 1) * half] = acc[0] + pe_acc[0:1, :]
            o_ref[:, (2 + c) * half:(3 + c) * half] = acc[1]


def _compress_weights(w_cmp, pe_cmp, n_groups):
    n_r = CMP_LEN // CMP_STRIDE
    gw = n_groups * HEAD_DIM
    w = w_cmp.reshape(n_r, CMP_STRIDE, 2, HEAD_DIM, HEAD_DIM).transpose(0, 2, 1, 3, 4)
    eye = jnp.eye(n_groups, dtype=w.dtype)
    wbd = (eye[None, None, None, :, None, :, None] * w[:, :, :, None, :, None, :]).reshape(n_r * 2, CMP_STRIDE * gw, gw)
    pe = jnp.broadcast_to(pe_cmp[:, :, None, :], (CMP_LEN, 2, n_groups, HEAD_DIM)).reshape(CMP_LEN, -1)
    return wbd.astype(BF16), pe


def _compress(ctx, wbd, pe):
    b, tc, width = ctx.shape
    n_ch = tc // CMP_STRIDE
    return pl.pallas_call(
        functools.partial(_compress_kernel, n_ch=n_ch, merged=True),
        grid=(b,),
        in_specs=[pl.BlockSpec((None, tc, LANES), functools.partial(lambda j, i: (i, 0, j), j))
                  for j in range(width // LANES)] + [_resident(wbd.shape), _resident(pe.shape)],
        out_specs=pl.BlockSpec((None, n_ch, width), lambda i: (i, 0, 0)),
        out_shape=jax.ShapeDtypeStruct((b, n_ch, width), F32),
        compiler_params=_cparams(1),
        name="nsa_compress",
    )(*([ctx] * (width // LANES)), wbd, pe)


def _alibi_tables(n_heads, pos):
    t = pos.shape[0]
    slopes = _alibi_slopes(n_heads)
    s1 = slopes.astype(BF16).astype(F32)
    s2 = (slopes - s1).astype(BF16).astype(F32)
    s3 = (slopes - s1 - s2).astype(BF16).astype(F32)
    qtab = jnp.zeros((n_heads, LANES), F32).at[:, HEAD_DIM:HEAD_DIM + ALIBI_LANES].set(
        jnp.stack([s1, s2, s3, s1, s2, s3], axis=-1))
    hi = (pos // SEL_BLOCK * SEL_BLOCK).astype(F32)
    lo = (pos % SEL_BLOCK).astype(F32)
    ktab = jnp.zeros((t, LANES), F32).at[:, HEAD_DIM:HEAD_DIM + ALIBI_LANES].set(
        jnp.stack([hi, hi, hi, lo, lo, lo], axis=-1))
    return qtab, ktab


def _stack_heads(q_ref, gl, qtab_ref, first_head):
    lo = _lane_lo()
    parts = []
    for r in range(4):
        c0 = gl * 4 * HEAD_DIM + (r // 2) * LANES
        slab = q_ref[:, c0:c0 + LANES] * (HEAD_DIM ** -0.5)
        if r % 2:
            slab = pltpu.roll(slab, HEAD_DIM, axis=1)
        parts.append(jnp.where(lo, slab, qtab_ref[pl.ds(first_head + r, 1), :]).astype(BF16))
    return jnp.concatenate(parts, axis=0)


def _stage_kv_aug(k_ref, v_ref, ktab_ref, gl):
    lo = _lane_lo()
    own = lo if gl == 0 else ~lo
    k_own = jnp.where(own, k_ref[...], 0.0)
    v_own = jnp.where(own, v_ref[...], 0.0)
    k_low = k_own if gl == 0 else pltpu.roll(k_own, HEAD_DIM, axis=1)
    return (k_low + ktab_ref[...]).astype(BF16), (v_own + pltpu.roll(v_own, HEAD_DIM, axis=1)).astype(BF16)


def _unstack_heads(o_ref, gl, out, tq):
    lo = _lane_lo()
    base = gl * 4 * HEAD_DIM
    o_ref[:, base:base + LANES] = jnp.where(lo, out[0:tq], out[tq:2 * tq])
    o_ref[:, base + LANES:base + 2 * LANES] = jnp.where(lo, out[2 * tq:3 * tq], out[3 * tq:4 * tq])


def _cmp_select_kernel(qtab_ref, ktab_ref, c2s_ref, q_ref, kc_ref, vc_ref, o_ref, sel_ref, *, n_c, n_sel):
    gp, qi = pl.program_id(1), pl.program_id(2)
    tq = q_ref.shape[0]
    n_ch = kc_ref.shape[0]
    qp = qi * tq + lax.broadcasted_iota(jnp.int32, (tq, 1), 0)
    n_idx = lax.broadcasted_iota(jnp.int32, (1, n_ch), 1)
    cmp_end = n_idx * CMP_STRIDE + (CMP_LEN - 1)
    valid = (cmp_end <= qp) & (n_idx < n_c)
    blk = lax.broadcasted_iota(jnp.int32, (1, LANES), 1)
    cb = qp // SEL_BLOCK
    blk_ok = (blk <= cb) & (blk < n_sel)
    forced = (blk == 0) | (blk == cb) | (blk == cb - 1)
    blk_t = lax.broadcasted_iota(jnp.int32, (n_sel, tq), 0)
    valid4 = jnp.concatenate([valid] * 4, axis=0)
    for gl in range(2):
        k_aug, v_both = _stage_kv_aug(kc_ref, vc_ref, ktab_ref, gl)
        q_all = _stack_heads(q_ref, gl, qtab_ref, (2 * gp + gl) * 4)
        s = jnp.where(valid4, _nt_dot(q_all, k_aug), NEG)
        m = jnp.max(s, axis=-1, keepdims=True)
        p = jnp.where(valid4, jnp.exp(s - m), 0.0)
        p = (p / jnp.maximum(jnp.sum(p, axis=-1, keepdims=True), 1e-30)).astype(BF16)
        _unstack_heads(o_ref, gl, jnp.dot(p, v_both, preferred_element_type=F32), tq)
        imp_rows = jnp.dot(p, c2s_ref[...], preferred_element_type=F32)
        imp = imp_rows[0:tq] + imp_rows[tq:2 * tq] + imp_rows[2 * tq:3 * tq] + imp_rows[3 * tq:4 * tq]
        score = jnp.where(blk_ok, imp + FORCE_BONUS * forced.astype(F32), NEG)
        score_t = score.T[0:n_sel, :]
        rank = jnp.zeros((n_sel, tq), F32)
        for k in range(n_sel):
            row = score_t[k:k + 1, :]
            ahead = (row > score_t) | ((row == score_t) & (k < blk_t))
            rank = rank + ahead.astype(F32)
        chosen = ((rank < SEL_TOP) & (score_t > 0.5 * NEG)).astype(F32)
        chosen = jnp.concatenate([chosen, jnp.zeros((LANES - n_sel, tq), F32)], axis=0) if n_sel < LANES else chosen
        sel_ref[gl] = chosen.T.astype(BF16)


def _cmp_select(q, kvc, n_c, n_sel, n_groups, tq):
    b, t, width = q.shape
    n_ch = kvc.shape[1]
    n_gp = n_groups // 2
    n_heads = width // HEAD_DIM
    assert tq == LANES and n_sel <= LANES and n_sel % 8 == 0 and t % tq == 0
    qtab, ktab = _alibi_tables(n_heads, jnp.arange(n_ch) * CMP_STRIDE + (CMP_LEN - 1))
    cs = jnp.arange(n_ch)[:, None] * CMP_STRIDE
    ss = jnp.arange(LANES)[None, :] * SEL_BLOCK
    ov = jnp.maximum(jnp.minimum(cs + CMP_LEN, ss + SEL_BLOCK) - jnp.maximum(cs, ss), 0).astype(F32) / CMP_STRIDE
    c2s = jnp.where((jnp.arange(n_ch)[:, None] < n_c) & (jnp.arange(LANES)[None, :] < n_sel), ov, 0.0).astype(BF16)
    return pl.pallas_call(
        functools.partial(_cmp_select_kernel, n_c=n_c, n_sel=n_sel),
        grid=(b, n_gp, t // tq),
        in_specs=[_resident(qtab.shape), _resident(ktab.shape), _resident(c2s.shape),
                  pl.BlockSpec((None, tq, 2 * 4 * HEAD_DIM), lambda i, j, k: (i, k, j)),
                  pl.BlockSpec((None, n_ch, LANES), lambda i, j, k: (i, 0, j)),
                  pl.BlockSpec((None, n_ch, LANES), lambda i, j, k: (i, 0, n_gp + j))],
        out_specs=[pl.BlockSpec((None, tq, 2 * 4 * HEAD_DIM), lambda i, j, k: (i, k, j)),
                   pl.BlockSpec((None, 2, tq, LANES), lambda i, j, k: (i, j, k, 0))],
        out_shape=[jax.ShapeDtypeStruct((b, t, width), F32),
                   jax.ShapeDtypeStruct((b, n_groups, t, LANES), BF16)],
        compiler_params=_cparams(3),
        name="nsa_cmp_select",
    )(qtab, ktab, c2s, q, kvc, kvc)


def _sel_attn_kernel(qtab_ref, ktab_ref, q_ref, sel_ref, k_ref, v_ref, o_ref, ka0, ka1, v0, v1, qv_ref, *, tk):
    gp, qi = pl.program_id(1), pl.program_id(2)
    tq = q_ref.shape[0]
    staged = ((ka0, v0), (ka1, v1))
    n_var = qv_ref.shape[0]
    blocks_per_tile = tk // SEL_BLOCK
    mask_lane0 = HEAD_DIM + ALIBI_LANES

    @pl.when(qi == 0)
    def _():
        for gl, (ka, vs) in enumerate(staged):
            ka[...], vs[...] = _stage_kv_aug(k_ref, v_ref, ktab_ref, gl)

    q0 = qi * tq
    n_full = q0 // tk
    qp = q0 + lax.broadcasted_iota(jnp.int32, (4 * tq, 1), 0) % tq
    lane = lax.broadcasted_iota(jnp.int32, (1, LANES), 1)
    mask_lanes = (lane >= mask_lane0) & (lane < mask_lane0 + blocks_per_tile)
    for gl, (ka, vs) in enumerate(staged):
        q_base = _stack_heads(q_ref, gl, qtab_ref, (2 * gp + gl) * 4)
        not_sel = sel_ref[gl].astype(F32) - 1.0
        for var in range(n_var):
            shift = (mask_lane0 - var * blocks_per_tile) % LANES
            flags = jnp.where(mask_lanes, pltpu.roll(not_sel, shift, axis=1) if shift else not_sel, 0.0).astype(BF16)
            qv_ref[var] = q_base + jnp.concatenate([flags] * 4, axis=0)

        def scores(kt, width, ka=ka):
            return _nt_dot(qv_ref[kt], ka[kt * tk:kt * tk + width, :])

        def absorb(kt, s, carry, masked, vs=vs):
            m_old, l_old, acc_old = carry
            width = s.shape[1]
            if masked:
                kpos = kt * tk + lax.broadcasted_iota(jnp.int32, (1, width), 1)
                s = jnp.where(kpos <= qp, s, NEG)
            m_new = jnp.maximum(m_old, jnp.max(s, axis=-1, keepdims=True))
            alpha = jnp.exp(m_old - m_new)
            p = jnp.exp(s - m_new)
            l_new = alpha * l_old + jnp.sum(p, axis=-1, keepdims=True)
            acc_new = alpha * acc_old + jnp.dot(p.astype(BF16), vs[kt * tk:kt * tk + width, :], preferred_element_type=F32)
            return m_new, l_new, acc_new

        def run(n, last_width):
            carry = (jnp.full((4 * tq, 1), NEG, F32), jnp.zeros((4 * tq, 1), F32), jnp.zeros((4 * tq, LANES), F32))
            ss = [scores(kt, tk) for kt in range(n)] + [scores(n, last_width)]
            for kt in range(n):
                carry = absorb(kt, ss[kt], carry, False)
            _, l, acc = absorb(n, ss[n], carry, True)
            return acc / jnp.maximum(l, 1e-30)

        variants = [functools.partial(run, n, w) for n in range(n_var) for w in (tk // 2, tk)]
        out = lax.switch(2 * n_full + (q0 % tk) // (tk // 2), variants)
        _unstack_heads(o_ref, gl, out, tq)


def _sel_attn(q, kv, sel, n_groups, tq, tk):
    b, t, width = q.shape
    n_gp = n_groups // 2
    n_heads = width // HEAD_DIM
    blocks_per_tile = tk // SEL_BLOCK
    assert t % tk == 0 and (tk // 2) % tq == 0 and HEAD_DIM + ALIBI_LANES + blocks_per_tile <= LANES
    qtab, ktab = _alibi_tables(n_heads, jnp.arange(t))
    blk_lane = HEAD_DIM + ALIBI_LANES + (jnp.arange(t) // SEL_BLOCK) % blocks_per_tile
    ktab = ktab + jnp.where(blk_lane[:, None] == jnp.arange(LANES)[None, :], MASK_BIG, 0.0)
    return pl.pallas_call(
        functools.partial(_sel_attn_kernel, tk=tk),
        grid=(b, n_gp, t // tq),
        in_specs=[_resident(qtab.shape), _resident(ktab.shape),
                  pl.BlockSpec((None, tq, 2 * 4 * HEAD_DIM), lambda i, j, k: (i, k, j)),
                  pl.BlockSpec((None, 2, tq, LANES), lambda i, j, k: (i, j, k, 0)),
                  pl.BlockSpec((None, t, LANES), lambda i, j, k: (i, 0, j)),
                  pl.BlockSpec((None, t, LANES), lambda i, j, k: (i, 0, n_gp + j))],
        out_specs=pl.BlockSpec((None, tq, 2 * 4 * HEAD_DIM), lambda i, j, k: (i, k, j)),
        out_shape=jax.ShapeDtypeStruct((b, t, width), F32),
        scratch_shapes=[pltpu.VMEM((t, LANES), BF16)] * 4 + [pltpu.VMEM((t // tk, 4 * tq, LANES), BF16)],
        compiler_params=_cparams(3),
        name="nsa_sel_attn",
    )(qtab, ktab, q, sel, kv, kv)


def _win_combine_kernel(qtab_ref, ktab_ref, q_ref, gate_ref, ocmp_ref, osel_ref, k_ref, v_ref, o_ref,
                        ka0, ka1, v0, v1, *, seq):
    gp, qi = pl.program_id(1), pl.program_id(2)
    tq = q_ref.shape[0]
    tk = WIN + tq
    lo = _lane_lo()
    staged = ((ka0, v0), (ka1, v1))

    @pl.when(qi == 0)
    def _():
        for gl, (ka, vs) in enumerate(staged):
            ka[...], vs[...] = _stage_kv_aug(k_ref, v_ref, ktab_ref, gl)

    k0 = jnp.clip(qi * tq - WIN, 0, seq - tk)
    rows = pl.ds(pl.multiple_of(k0, tq), tk)
    qp = qi * tq + lax.broadcasted_iota(jnp.int32, (4 * tq, 1), 0) % tq
    dist = qp - (k0 + lax.broadcasted_iota(jnp.int32, (1, tk), 1))
    ok = (dist >= 0) & (dist <= WIN)
    gates = 1.0 / (1.0 + jnp.exp(-gate_ref[...]))
    for gl, (ka, vs) in enumerate(staged):
        q_all = _stack_heads(q_ref, gl, qtab_ref, (2 * gp + gl) * 4)
        s = jnp.where(ok, _nt_dot(q_all, ka[rows, :]), NEG)
        m = jnp.max(s, axis=-1, keepdims=True)
        p = jnp.where(ok, jnp.exp(s - m), 0.0)
        l = jnp.maximum(jnp.sum(p, axis=-1, keepdims=True), 1e-30)
        out = jnp.dot(p.astype(BF16), vs[rows, :], preferred_element_type=F32) / l
        for half in range(2):
            cols = slice(gl * 4 * HEAD_DIM + half * LANES, gl * 4 * HEAD_DIM + (half + 1) * LANES)
            o_win = jnp.where(lo, out[2 * half * tq:(2 * half + 1) * tq], out[(2 * half + 1) * tq:(2 * half + 2) * tq])
            c0 = (gl * 4 + 2 * half) * 3
            g_cmp, g_sel, g_win = (jnp.where(lo, gates[:, c0 + i:c0 + i + 1], gates[:, c0 + 3 + i:c0 + 4 + i])
                                   for i in range(3))
            o_ref[:, cols] = g_cmp * ocmp_ref[:, cols] + g_sel * osel_ref[:, cols] + g_win * o_win


def _win_combine(q, kv, gates, o_cmp, o_sel, n_groups, tq):
    b, t, width = q.shape
    n_gp = n_groups // 2
    n_heads = width // HEAD_DIM
    assert t % tq == 0 and t >= WIN + tq
    qtab, ktab = _alibi_tables(n_heads, jnp.arange(t))
    qblk = pl.BlockSpec((None, tq, 2 * 4 * HEAD_DIM), lambda i, j, k: (i, k, j))
    return pl.pallas_call(
        functools.partial(_win_combine_kernel, seq=t),
        grid=(b, n_gp, t // tq),
        in_specs=[_resident(qtab.shape), _resident(ktab.shape), qblk,
                  pl.BlockSpec((None, tq, LANES), lambda i, j, k: (i, k, j)),
                  qblk, qblk,
                  pl.BlockSpec((None, t, LANES), lambda i, j, k: (i, 0, j)),
                  pl.BlockSpec((None, t, LANES), lambda i, j, k: (i, 0, n_gp + j))],
        out_specs=qblk,
        out_shape=jax.ShapeDtypeStruct((b, t, width), F32),
        scratch_shapes=[pltpu.VMEM((t, LANES), BF16)] * 4,
        compiler_params=_cparams(3),
        name="nsa_win_combine",
    )(qtab, ktab, q, gates, o_cmp, o_sel, kv, kv)


def _nsa_prompt(q, kv_cmp, kv_sel, kv_win, gates, wbd, pe, n_groups):
    t = q.shape[1]
    kvc = _compress(kv_cmp, wbd, pe)
    n_c = t // CMP_STRIDE - CMP_LEN // CMP_STRIDE + 1
    n_sel = -(-t // SEL_BLOCK)
    o_cmp, sel = _cmp_select(q, kvc, n_c, n_sel, n_groups, LANES)
    o_sel = _sel_attn(q, kv_sel, sel, n_groups, LANES, 8 * LANES)
    return _win_combine(q, kv_win, gates, o_cmp, o_sel, n_groups, LANES)


def _head_block_diag(x, n_rep, lanes_per_head, rows_per_head):
    tiled = jnp.concatenate([x] * n_rep, axis=0)
    row_h = lax.broadcasted_iota(jnp.int32, tiled.shape, 0) // rows_per_head
    lane_h = lax.broadcasted_iota(jnp.int32, tiled.shape, 1) // lanes_per_head
    return jnp.where(row_h == lane_h, tiled, 0.0)


def _head_diag_rows(o, n_rep, lanes_per_head, rows_per_head):
    row_h = lax.broadcasted_iota(jnp.int32, o.shape, 0) // rows_per_head
    lane_h = lax.broadcasted_iota(jnp.int32, o.shape, 1) // lanes_per_head
    kept = jnp.where(row_h == lane_h, o, 0.0)
    return jnp.sum(kept.reshape(n_rep, rows_per_head, o.shape[1]), axis=0)


def _dil_count(dist):
    count = jnp.zeros(dist.shape, F32)
    for window, dil in DIL_PAIRS:
        assert dil & (dil - 1) == 0
        count = count + ((dist >= 0) & (dist <= window) & ((dist & (dil - 1)) == 0)).astype(F32)
    return count


def _dil_sample_kernel(slope_ref, q_ref, kvn_ref, cache_ref, o_ref, ctx_ref):
    ts, width = q_ref.shape
    past = cache_ref.shape[-1]
    n_heads = width // HEAD_DIM
    rows = n_heads * ts
    slope = slope_ref[:, 0:1]
    k_t = cache_ref[0].reshape(width, past)
    v_t = cache_ref[1].reshape(width, past)
    new = jnp.concatenate([kvn_ref[...], jnp.zeros((LANES - ts, 2 * width), F32)], axis=0)
    qbd = _head_block_diag(q_ref[...] * (HEAD_DIM ** -0.5), n_heads, HEAD_DIM, ts).astype(BF16)
    t_row = lax.broadcasted_iota(jnp.int32, (rows, 1), 0) % ts
    d_old = (past + t_row) - lax.broadcasted_iota(jnp.int32, (1, past), 1)
    j_new = lax.broadcasted_iota(jnp.int32, (1, LANES), 1)
    d_new = jnp.where(j_new < ts, t_row - j_new, -1)
    c_old, c_new = _dil_count(d_old), _dil_count(d_new)
    s_old = jnp.dot(qbd, k_t.astype(BF16), preferred_element_type=F32)
    s_new = _nt_dot(qbd, new[:, 0:width].astype(BF16))
    s_old = jnp.where(c_old > 0, s_old - slope * d_old.astype(F32), NEG)
    s_new = jnp.where(c_new > 0, s_new - slope * d_new.astype(F32), NEG)
    m = jnp.maximum(jnp.max(s_old, axis=-1, keepdims=True), jnp.max(s_new, axis=-1, keepdims=True))
    p_old = c_old * jnp.exp(s_old - m)
    p_new = c_new * jnp.exp(s_new - m)
    l = jnp.sum(p_old, axis=-1, keepdims=True) + jnp.sum(p_new, axis=-1, keepdims=True)
    o = _nt_dot(p_old.astype(BF16), v_t.astype(BF16)) + jnp.dot(p_new.astype(BF16), new[:, width:].astype(BF16),
                                                               preferred_element_type=F32)
    o_ref[...] = _head_diag_rows(o / l, n_heads, HEAD_DIM, ts)
    both = cache_ref[...].reshape(2 * width, past)
    rolled = pltpu.roll(both, past - ts, axis=1)
    new_t = pltpu.roll(new.T, LANES - ts, axis=1)
    ctx_ref[:, 0:past - LANES] = rolled[:, 0:past - LANES]
    ctx_ref[:, past - LANES:past] = jnp.where(j_new >= LANES - ts, new_t, rolled[:, past - LANES:past])


def _dil_sample(q, kv_new, cache_t, n_heads):
    b, ts, width = q.shape
    past = cache_t.shape[-1]
    assert ts % 8 == 0 and past % LANES == 0
    slopes = jnp.broadcast_to(jnp.repeat(_alibi_slopes(n_heads), ts)[:, None], (n_heads * ts, LANES))
    per_b = lambda rows, w: pl.BlockSpec((None, rows, w), lambda i: (i, 0, 0))
    return pl.pallas_call(
        _dil_sample_kernel,
        grid=(b,),
        in_specs=[_resident(slopes.shape), per_b(ts, width), per_b(ts, 2 * width),
                  pl.BlockSpec((None,) + cache_t.shape[1:], lambda i: (i, 0, 0, 0, 0))],
        out_specs=[per_b(ts, width), per_b(2 * width, past)],
        out_shape=[jax.ShapeDtypeStruct((b, ts, width), F32), jax.ShapeDtypeStruct((b, 2 * width, past), F32)],
        compiler_params=_cparams(1),
        name="dil_sample",
    )(slopes, q, kv_new, cache_t)


def _compress_pool_kernel(pool_ref, w_ref, pe_ref, o_ref, *slabs):
    pages, _, n_groups, hd, page_rows = pool_ref.shape
    pairs = n_groups * hd // LANES
    chunks = page_rows // CMP_STRIDE
    out_row = lax.broadcasted_iota(jnp.int32, (page_rows, page_rows), 0)
    src_row = (out_row % chunks) * CMP_STRIDE + out_row // chunks
    perm = (src_row == lax.broadcasted_iota(jnp.int32, (page_rows, page_rows), 1)).astype(BF16)

    batch = 8

    def to_rows(pb, carry):
        for i in range(batch):
            p = pb * batch + i
            for c in range(2):
                for gp in range(pairs):
                    tile_t = pool_ref[p, c, pl.ds(gp * (LANES // hd), LANES // hd)].reshape(LANES, page_rows)
                    rows = _nt_dot(perm, tile_t.astype(BF16))
                    for s in range(CMP_STRIDE):
                        slabs[c * pairs + gp][s, pl.ds(pl.multiple_of(p * chunks, chunks), chunks), :] = (
                            rows[s * chunks:(s + 1) * chunks, :])
        return carry

    assert pages % batch == 0 and chunks == 8
    lax.fori_loop(0, pages // batch, to_rows, 0)
    _compress_kernel(*slabs, w_ref, pe_ref, o_ref, n_ch=pages * chunks, merged=False, by_row=True)


def _compress_pool(pool, wbd, pe, pages_per_step):
    n_pages, _, n_groups, hd, page_rows = pool.shape
    width = 2 * n_groups * hd
    assert n_pages % pages_per_step == 0 and page_rows % (8 * CMP_STRIDE) == 0 and page_rows == LANES
    n_ch = pages_per_step * page_rows // CMP_STRIDE
    return pl.pallas_call(
        _compress_pool_kernel,
        grid=(n_pages // pages_per_step,),
        in_specs=[pl.BlockSpec((pages_per_step,) + pool.shape[1:], lambda i: (i, 0, 0, 0, 0)),
                  _resident(wbd.shape), _resident(pe.shape)],
        out_specs=pl.BlockSpec((n_ch, 2 * width), lambda i: (i, 0)),
        out_shape=jax.ShapeDtypeStruct((n_pages * page_rows // CMP_STRIDE, 2 * width), F32),
        scratch_shapes=[pltpu.VMEM((CMP_STRIDE, n_ch, LANES), F32)] * (width // LANES),
        compiler_params=_cparams(1),
        name="nsa_compress_pool",
    )(pool, wbd, pe)


def _sample_query(q_ref, ts, n_groups):
    gw = n_groups * HEAD_DIM
    n_rep = q_ref.shape[1] // gw
    return jnp.concatenate(
        [_head_block_diag(q_ref[:, r * gw:(r + 1) * gw] * (HEAD_DIM ** -0.5), n_groups, HEAD_DIM, ts) for r in range(n_rep)],
        axis=0).astype(BF16)


def _sample_rows_out(o, ts, n_groups):
    per_r = n_groups * ts
    return [_head_diag_rows(o[r * per_r:(r + 1) * per_r, :], n_groups, HEAD_DIM, ts) for r in range(o.shape[0] // per_r)]


def _nsa_sample_cmp_kernel(pt_ref, slope_ref, c2s_ref, q_ref, kvn_ref, cwin_ref, *rest, pages, past, n_c, n_sel):
    del pt_ref
    ab_refs = rest[:pages]
    xcmp_ref, xwin_ref, sel_ref, winout_ref, ab_scr, kwin_scr, vwin_scr = rest[pages:]
    j = pl.program_id(1)
    ts = q_ref.shape[0]
    n_groups = kvn_ref.shape[1] // (2 * HEAD_DIM)
    gw = n_groups * HEAD_DIM
    for k in range(pages):
        ab_scr[pl.ds(pl.multiple_of((j * pages + k) * 8, 8), 8), :] = ab_refs[k][...]

    @pl.when(j == pl.num_programs(1) - 1)
    def _():
        n_ch = ab_scr.shape[0]
        n_rows = slope_ref.shape[0]
        slope = slope_ref[:, 0:1]
        qbd = _sample_query(q_ref, ts, n_groups)
        t_row = lax.broadcasted_iota(jnp.int32, (n_rows, 1), 0) % ts
        qp = past + t_row
        kvc = ab_scr[:, 0:2 * gw] + pltpu.roll(ab_scr[:, 2 * gw:4 * gw], n_ch - 1, axis=0)
        n_idx = lax.broadcasted_iota(jnp.int32, (1, n_ch), 1)
        cmp_end = n_idx * CMP_STRIDE + (CMP_LEN - 1)
        valid = (cmp_end <= qp) & (n_idx < n_c)
        s = jnp.where(valid, _nt_dot(qbd, kvc[:, 0:gw].astype(BF16)) - slope * (qp - cmp_end).astype(F32), NEG)
        m = jnp.max(s, axis=-1, keepdims=True)
        p = jnp.where(valid, jnp.exp(s - m), 0.0)
        p = (p / jnp.maximum(jnp.sum(p, axis=-1, keepdims=True), 1e-30)).astype(BF16)
        for r, x in enumerate(_sample_rows_out(jnp.dot(p, kvc[:, gw:2 * gw].astype(BF16), preferred_element_type=F32), ts, n_groups)):
            xcmp_ref[r * ts:(r + 1) * ts, :] = x
        imp_rows = jnp.dot(p, c2s_ref[...], preferred_element_type=F32)
        gt = n_groups * ts
        imp = jnp.sum(imp_rows.reshape(n_rows // gt, gt, imp_rows.shape[1]), axis=0)
        blk = lax.broadcasted_iota(jnp.int32, (1, imp.shape[1]), 1)
        cb = (past + lax.broadcasted_iota(jnp.int32, (gt, 1), 0) % ts) // SEL_BLOCK
        forced = (blk == 0) | (blk == cb) | (blk == cb - 1)
        score = jnp.where((blk <= cb) & (blk < n_sel), imp + FORCE_BONUS * forced.astype(F32), NEG)
        removed = -3e38
        cur = jnp.where(blk < n_sel, score, removed)
        chosen = jnp.zeros(score.shape, F32)
        for _ in range(SEL_TOP):
            top = jnp.max(cur, axis=-1, keepdims=True)
            first = jnp.min(jnp.where(cur == top, blk, imp.shape[1]), axis=-1, keepdims=True)
            pick = blk == first
            chosen = jnp.where(pick & (top > 0.5 * NEG), 1.0, chosen)
            cur = jnp.where(pick, removed, cur)
        n_steps = sel_ref.shape[0]
        per_step = (past // SEL_BLOCK) // n_steps
        lane = lax.broadcasted_iota(jnp.int32, (1, LANES), 1)
        for st in range(n_steps):
            shifted = chosen if st == 0 else pltpu.roll(chosen, chosen.shape[1] - st * per_step, axis=1)
            n_here = per_step + (n_sel - n_steps * per_step if st == n_steps - 1 else 0)
            piece = jnp.where(lane < n_here, shifted[:, 0:LANES], 0.0)
            sel_ref[st] = jnp.concatenate([piece] * (n_rows // gt), axis=0).astype(BF16)
        w_past = cwin_ref.shape[0]
        pad = kwin_scr.shape[0] - w_past
        kwin_scr[0:w_past, :] = cwin_ref[:, 0:gw].astype(BF16)
        vwin_scr[0:w_past, :] = cwin_ref[:, gw:2 * gw].astype(BF16)
        zeros = jnp.zeros((pad - ts, gw), F32)
        kwin_scr[w_past:, :] = jnp.concatenate([kvn_ref[:, 0:gw], zeros], axis=0).astype(BF16)
        vwin_scr[w_past:, :] = jnp.concatenate([kvn_ref[:, gw:2 * gw], zeros], axis=0).astype(BF16)
        dist = (w_past + t_row) - lax.broadcasted_iota(jnp.int32, (1, kwin_scr.shape[0]), 1)
        ok = (dist >= 0) & (dist <= WIN)
        s = jnp.where(ok, _nt_dot(qbd, kwin_scr[...]) - slope * dist.astype(F32), NEG)
        m = jnp.max(s, axis=-1, keepdims=True)
        p = jnp.where(ok, jnp.exp(s - m), 0.0)
        l = jnp.maximum(jnp.sum(p, axis=-1, keepdims=True), 1e-30)
        o = jnp.dot(p.astype(BF16), vwin_scr[...], preferred_element_type=F32) / l
        for r, x in enumerate(_sample_rows_out(o, ts, n_groups)):
            xwin_ref[r * ts:(r + 1) * ts, :] = x
        winout_ref[0:w_past - ts, :] = cwin_ref[ts:w_past, :]
        winout_ref[w_past - ts:w_past, :] = kvn_ref[...]


def _sample_slopes(n_groups, n_rep, ts):
    slopes = _alibi_slopes(n_groups * n_rep).reshape(n_groups, n_rep).T
    return jnp.broadcast_to(jnp.repeat(slopes.reshape(-1), ts)[:, None], (n_rep * n_groups * ts, LANES))


def _nsa_sample_cmp(q, kv_win_new, cache_win, ab_pool, page_table, n_groups, sel_steps, pages):
    b, ts, qw = q.shape
    n_rep = qw // HEAD_DIM // n_groups
    gw = n_groups * HEAD_DIM
    n_pages = page_table.shape[1]
    chunks_per_page = ab_pool.shape[1]
    past = n_pages * chunks_per_page * CMP_STRIDE
    n_ch = (past + ts) // CMP_STRIDE
    assert n_ch == n_pages * chunks_per_page and n_pages % pages == 0
    n_c = n_ch - CMP_LEN // CMP_STRIDE + 1
    n_sel = -(-(past + ts) // SEL_BLOCK)
    assert n_sel <= 2 * LANES and n_sel - past // SEL_BLOCK + past // SEL_BLOCK // sel_steps <= LANES
    n_rows = n_rep * n_groups * ts
    slopes = _sample_slopes(n_groups, n_rep, ts)
    cs = jnp.arange(n_ch)[:, None] * CMP_STRIDE
    ss = jnp.arange(2 * LANES)[None, :] * SEL_BLOCK
    ov = jnp.maximum(jnp.minimum(cs + CMP_LEN, ss + SEL_BLOCK) - jnp.maximum(cs, ss), 0).astype(F32) / CMP_STRIDE
    c2s = jnp.where((jnp.arange(n_ch)[:, None] < n_c) & (jnp.arange(2 * LANES)[None, :] < n_sel), ov, 0.0).astype(BF16)
    w_past = cache_win.shape[1]
    per_b = lambda rows, w: pl.BlockSpec((None, rows, w), lambda i, j, pt: (i, 0, 0))
    page_spec = lambda k: pl.BlockSpec((None, chunks_per_page, ab_pool.shape[2]),
                                       lambda i, j, pt: (pt[i, j * pages + k], 0, 0))
    const = lambda shape: pl.BlockSpec(shape, lambda i, j, pt: (0,) * len(shape), pipeline_mode=pl.Buffered(1))
    return pl.pallas_call(
        functools.partial(_nsa_sample_cmp_kernel, pages=pages, past=past, n_c=n_c, n_sel=n_sel),
        grid_spec=pltpu.PrefetchScalarGridSpec(
            num_scalar_prefetch=1,
            grid=(b, n_pages // pages),
            in_specs=[const(slopes.shape), const(c2s.shape), per_b(ts, qw), per_b(ts, 2 * gw), per_b(w_past, 2 * gw)]
                     + [page_spec(k) for k in range(pages)],
            out_specs=[per_b(n_rep * ts, gw), per_b(n_rep * ts, gw),
                       pl.BlockSpec((None, sel_steps, n_rows, LANES), lambda i, j, pt: (i, 0, 0, 0)),
                       per_b(w_past, 2 * gw)],
            scratch_shapes=[pltpu.VMEM((n_ch, ab_pool.shape[2]), F32),
                            pltpu.VMEM((w_past + LANES, gw), BF16), pltpu.VMEM((w_past + LANES, gw), BF16)]),
        out_shape=[jax.ShapeDtypeStruct((b, n_rep * ts, gw), F32), jax.ShapeDtypeStruct((b, n_rep * ts, gw), F32),
                   jax.ShapeDtypeStruct((b, sel_steps, n_rows, LANES), BF16),
                   jax.ShapeDtypeStruct(cache_win.shape, F32)],
        compiler_params=_cparams(2),
        name="nsa_sample_cmp",
    )(page_table, slopes, c2s, q, kv_win_new, cache_win, *([ab_pool] * pages))


def _nsa_sample_sel_kernel(pt_ref, slope_ref, exp_ref, q_ref, kvn_ref, sel_ref, xcmp_ref, xwin_ref, gate_ref, *rest,
                           pages, past):
    del pt_ref
    page_refs = rest[:pages]
    o_ref, m_scr, l_scr, acc_scr, new_scr = rest[pages:]
    j = pl.program_id(1)
    ts = q_ref.shape[0]
    n_groups = kvn_ref.shape[1] // (2 * HEAD_DIM)
    gw = n_groups * HEAD_DIM
    n_rows = slope_ref.shape[0]
    page_rows = page_refs[0].shape[-1]
    tk = pages * page_rows
    slope = slope_ref[:, 0:1]
    qbd = _sample_query(q_ref, ts, n_groups)
    t_row = lax.broadcasted_iota(jnp.int32, (n_rows, 1), 0) % ts

    @pl.when(j == 0)
    def _():
        m_scr[...] = jnp.full(m_scr.shape, NEG, F32)
        l_scr[...] = jnp.zeros(l_scr.shape, F32)
        acc_scr[...] = jnp.zeros(acc_scr.shape, F32)

    def update(k, v, ok, dist, transposed):
        qk = jnp.dot(qbd, k, preferred_element_type=F32) if transposed else _nt_dot(qbd, k)
        s = jnp.where(ok, qk - slope * dist.astype(F32), NEG)
        m_old = m_scr[:, 0:1]
        m_new = jnp.maximum(m_old, jnp.max(s, axis=-1, keepdims=True))
        alpha = jnp.exp(m_old - m_new)
        p = jnp.where(ok, jnp.exp(s - m_new), 0.0)
        pv = _nt_dot(p.astype(BF16), v) if transposed else jnp.dot(p.astype(BF16), v, preferred_element_type=F32)
        l_scr[...] = jnp.broadcast_to(alpha * l_scr[:, 0:1] + jnp.sum(p, axis=-1, keepdims=True), l_scr.shape)
        acc_scr[...] = alpha * acc_scr[...] + pv
        m_scr[...] = jnp.broadcast_to(m_new, m_scr.shape)

    sel = sel_ref[j]
    k = jnp.concatenate([ref[0].reshape(gw, page_rows) for ref in page_refs], axis=1).astype(BF16)
    v = jnp.concatenate([ref[1].reshape(gw, page_rows) for ref in page_refs], axis=1).astype(BF16)
    picked = _nt_dot(sel, exp_ref[0:tk, :]) > 0.5
    kpos = j * tk + lax.broadcasted_iota(jnp.int32, (1, tk), 1)
    update(k, v, picked, (past + t_row) - kpos, True)

    @pl.when(j == pl.num_programs(1) - 1)
    def _():
        pad = new_scr.shape[0]
        new_scr[...] = jnp.concatenate([kvn_ref[...], jnp.zeros((pad - ts, 2 * gw), F32)], axis=0).astype(BF16)
        i_new = lax.broadcasted_iota(jnp.int32, (1, pad), 1)
        ok = (_nt_dot(sel, exp_ref[tk:tk + pad, :]) > 0.5) & (i_new <= t_row) & (i_new < ts)
        update(new_scr[:, 0:gw], new_scr[:, gw:2 * gw], ok, t_row - i_new, False)
        o = acc_scr[...] / jnp.maximum(l_scr[:, 0:1], 1e-30)
        n_rep = n_rows // (n_groups * ts)
        for r, x_sel in enumerate(_sample_rows_out(o, ts, n_groups)):
            g_cmp, g_sel, g_win = (1.0 / (1.0 + jnp.exp(-gate_ref[:, (i * n_rep + r) * gw:(i * n_rep + r + 1) * gw]))
                                   for i in range(3))
            rows = slice(r * ts, (r + 1) * ts)
            o_ref[:, r * gw:(r + 1) * gw] = g_cmp * xcmp_ref[rows, :] + g_sel * x_sel + g_win * xwin_ref[rows, :]


def _nsa_sample_sel(q, kv_sel_new, sel, x_cmp, x_win, gates, pool, page_table, n_groups, pages):
    b, ts, qw = q.shape
    n_rep = qw // HEAD_DIM // n_groups
    gw = n_groups * HEAD_DIM
    n_pages = page_table.shape[1]
    page_rows = pool.shape[-1]
    past = n_pages * page_rows
    n_steps = n_pages // pages
    tk = pages * page_rows
    n_rows = n_rep * n_groups * ts
    assert sel.shape[1] == n_steps and tk % SEL_BLOCK == 0 and tk // SEL_BLOCK < LANES
    slopes = _sample_slopes(n_groups, n_rep, ts)
    blk_of = jnp.concatenate([jnp.arange(tk) // SEL_BLOCK, jnp.full((LANES,), tk // SEL_BLOCK)])
    expand = (blk_of[:, None] == jnp.arange(LANES)[None, :]).astype(BF16)
    per_b = lambda rows, w: pl.BlockSpec((None, rows, w), lambda i, j, pt: (i, 0, 0))
    page_spec = lambda k: pl.BlockSpec((None,) + pool.shape[1:], lambda i, j, pt: (pt[i, j * pages + k], 0, 0, 0, 0))
    const = lambda shape: pl.BlockSpec(shape, lambda i, j, pt: (0,) * len(shape), pipeline_mode=pl.Buffered(1))
    return pl.pallas_call(
        functools.partial(_nsa_sample_sel_kernel, pages=pages, past=past),
        grid_spec=pltpu.PrefetchScalarGridSpec(
            num_scalar_prefetch=1,
            grid=(b, n_steps),
            in_specs=[const(slopes.shape), const(expand.shape), per_b(ts, qw), per_b(ts, 2 * gw),
                      pl.BlockSpec((None, n_steps, n_rows, LANES), lambda i, j, pt: (i, 0, 0, 0)),
                      per_b(n_rep * ts, gw), per_b(n_rep * ts, gw), per_b(ts, 3 * qw)]
                     + [page_spec(k) for k in range(pages)],
            out_specs=per_b(ts, qw),
            scratch_shapes=[pltpu.VMEM((n_rows, LANES), F32), pltpu.VMEM((n_rows, LANES), F32),
                            pltpu.VMEM((n_rows, gw), F32), pltpu.VMEM((LANES, 2 * gw), BF16)]),
        out_shape=jax.ShapeDtypeStruct((b, ts, qw), F32),
        compiler_params=_cparams(2),
        name="nsa_sample_sel",
    )(page_table, slopes, expand, q, kv_sel_new, sel, x_cmp, x_win, gates, *([pool] * pages))


def _gate_columns(w_gate, n_groups):
    d = w_gate.shape[0]
    per_pair = w_gate.reshape(d, n_groups // 2, 2 * 4 * 3)
    return jnp.pad(per_pair, ((0, 0), (0, 0), (0, LANES - 2 * 4 * 3))).reshape(d, -1)


def kernel(x_prompt, x_sample, cache_dil_kv, state_conv, state_rnn, cache_win_kv, cache_cmp_kv, cache_sel_kv, page_table, norm_mix, norm_ffn, norm_out, w_in_ab, w_out_ab, conv_w, conv_b, gate_a_w, gate_a_b, gate_x_w, gate_x_b, lru_lambda, w_in_c, w_out_c, w_cmp, pe_cmp, ffn_w1, ffn_w3, ffn_w2):
    bp, t, d = x_prompt.shape
    bs, ts, _ = x_sample.shape
    depth = norm_mix.shape[0]
    bf = lambda z: z.astype(BF16)
    tm_p, tm_s = 512, bs * ts
    yp = x_prompt.reshape(bp * t, d)
    ys = x_sample.reshape(bs * ts, d)
    outs = {k: [] for k in ("dil_p", "dil_s", "conv_p", "conv_s", "rnn_p", "rnn_s",
                            "win_p", "win_s", "cmp_p", "cmp_s", "sel_p", "sel_s")}
    for layer in range(depth):
        li = layer // 2
        last = layer == depth - 1
        ffn = (norm_ffn[layer], bf(ffn_w1[layer]), bf(ffn_w3[layer]), bf(ffn_w2[layer]), norm_out if last else None)
        if layer % 2 == 0:
            rw = conv_w.shape[2]
            aw = (w_in_ab.shape[2] - 2 * rw) // 3
            n_heads = aw // HEAD_DIM
            w_in = bf(w_in_ab[li])
            splits = (aw, 2 * aw, rw, rw)
            wa, wx = bf(_block_diag(gate_a_w[li])), bf(_block_diag(gate_x_w[li]))
            lru = (conv_w[li], conv_b[li], wa, gate_a_b[li], wx, gate_x_b[li], lru_lambda[li])
            w_outs = bf(w_out_ab[li])
            keep = min(DIL_MAX, t)
            as_cache = lambda z, groups, rows: jnp.transpose(z.reshape(bp, 2, groups, HEAD_DIM, rows), (0, 4, 1, 2, 3))
            q, kv, xr, gate, kv_t = _norm_proj(yp, norm_mix[layer], w_in, splits, tm_p,
                                               rows_minor=[(w_in[:, aw:3 * aw].T, keep)], seq=t)
            o_att = _dil_attn(q.reshape(bp, t, aw), kv.reshape(bp, t, 2 * aw), n_heads)
            o_rnn, conv_new, h_new = _rglru(xr.reshape(bp, t, rw), gate.reshape(bp, t, rw),
                                            jnp.zeros((bp, CONV_W - 1, rw), F32), jnp.zeros((bp, rw), F32),
                                            *lru, bp, 256)
            yp = _out_ffn(yp, [o_att.reshape(bp * t, aw), o_rnn.reshape(bp * t, rw)], w_outs, *ffn, tm_p)
            outs["dil_p"].append(as_cache(kv_t, n_heads, keep))
            outs["conv_p"].append(conv_new)
            outs["rnn_p"].append(h_new)
            q, kv, xr, gate = _norm_proj(ys, norm_mix[layer], w_in, splits, tm_s)
            cache = cache_dil_kv[li]
            o_att, ctx_new = _dil_sample(q.reshape(bs, ts, aw), kv.reshape(bs, ts, 2 * aw),
                                         jnp.transpose(cache, (0, 2, 3, 4, 1)), n_heads)
            ctx_new = jnp.transpose(ctx_new.reshape(bs, 2, n_heads, HEAD_DIM, cache.shape[1]), (0, 4, 1, 2, 3))
            o_rnn, conv_new, h_new = _rglru(xr.reshape(bs, ts, rw), gate.reshape(bs, ts, rw),
                                            state_conv[li], state_rnn[li], *lru, 8, ts)
            ys = _out_ffn(ys, [o_att.reshape(bs * ts, aw), o_rnn.reshape(bs * ts, rw)], w_outs, *ffn, tm_s)
            outs["dil_s"].append(ctx_new)
            outs["conv_s"].append(conv_new)
            outs["rnn_s"].append(h_new)
        else:
            n_groups = cache_win_kv.shape[4]
            kvw = 2 * n_groups * HEAD_DIM
            qw = w_out_c.shape[1]
            rep = qw // HEAD_DIM // n_groups
            w_in = bf(jnp.concatenate([w_in_c[li][:, :qw + 3 * kvw],
                                       _gate_columns(w_in_c[li][:, qw + 3 * kvw:], n_groups)], axis=1))
            gw = n_groups // 2 * LANES
            splits = (qw, kvw, kvw, kvw, gw)
            wbd, pe = _compress_weights(w_cmp[li], pe_cmp[li], n_groups)
            w_outs = bf(w_out_c[li])
            kv5 = lambda z, n: z.reshape(n, -1, 2, n_groups, HEAD_DIM)
            keep = min(WIN, t)
            as_cache = lambda z, rows: jnp.transpose(z.reshape(bp, 2, n_groups, HEAD_DIM, rows), (0, 4, 1, 2, 3))
            kv_t = lambda i: w_in[:, qw + i * kvw:qw + (i + 1) * kvw].T
            q, kv_cmp, kv_sel, kv_win, gates, cmp_t, sel_t, win_t = _norm_proj(
                yp, norm_mix[layer], w_in, splits, tm_p, rows_minor=[(kv_t(0), t), (kv_t(1), t), (kv_t(2), keep)], seq=t)
            o = _nsa_prompt(q.reshape(bp, t, qw), kv_cmp.reshape(bp, t, kvw), kv_sel.reshape(bp, t, kvw),
                            kv_win.reshape(bp, t, kvw), gates.reshape(bp, t, gw), wbd, pe, n_groups)
            yp = _out_ffn(yp, [o.reshape(bp * t, qw)], w_outs, *ffn, tm_p)
            outs["win_p"].append(as_cache(win_t, keep))
            outs["cmp_p"].append(as_cache(cmp_t, t))
            outs["sel_p"].append(as_cache(sel_t, t))
            wq = w_in_c[li][:, :qw].reshape(d, n_groups, rep, HEAD_DIM).transpose(0, 2, 1, 3).reshape(d, qw)
            wg = w_in_c[li][:, qw + 3 * kvw:].reshape(d, n_groups, rep, 3).transpose(0, 3, 2, 1)
            wg = jnp.broadcast_to(wg[..., None], wg.shape + (HEAD_DIM,)).reshape(d, 3 * qw)
            w_in_s = bf(jnp.concatenate([wq, w_in_c[li][:, qw:qw + 3 * kvw], wg], axis=1))
            w_out_s = bf(w_out_c[li].reshape(n_groups, rep, HEAD_DIM, d).transpose(1, 0, 2, 3).reshape(qw, d))
            q, kv_cmp, kv_sel, kv_win, gates = _norm_proj(ys, norm_mix[layer], w_in_s, (qw, kvw, kvw, kvw, 3 * qw), tm_s)
            n_phys, page_rows = cache_cmp_kv.shape[1], cache_cmp_kv.shape[2]
            rows_minor = lambda pool: jnp.transpose(pool, (0, 2, 3, 4, 1))
            ab_pool = _compress_pool(rows_minor(cache_cmp_kv[li]), wbd, pe, 32)
            ab_pool = ab_pool.reshape(n_phys, page_rows // CMP_STRIDE, 2 * kvw)
            cwin = cache_win_kv[li]
            x_cmp, x_win, sel, win_new = _nsa_sample_cmp(q.reshape(bs, ts, qw), kv_win.reshape(bs, ts, kvw),
                                                         cwin.reshape(bs, cwin.shape[1], kvw), ab_pool, page_table,
                                                         n_groups, 4, 16)
            o = _nsa_sample_sel(q.reshape(bs, ts, qw), kv_sel.reshape(bs, ts, kvw), sel, x_cmp, x_win,
                                gates.reshape(bs, ts, 3 * qw), rows_minor(cache_sel_kv[li]), page_table, n_groups, 16)
            ys = _out_ffn(ys, [o.reshape(bs * ts, qw)], w_out_s, *ffn, tm_s)
            outs["win_s"].append(win_new.reshape(cwin.shape))
            outs["cmp_s"].append(kv5(kv_cmp, bs))
            outs["sel_s"].append(kv5(kv_sel, bs))
    st = lambda k: jnp.stack(outs[k])
    return (yp.reshape(bp, t, d), ys.reshape(bs, ts, d), st("dil_p"), st("dil_s"), st("conv_p"), st("conv_s"),
            st("rnn_p"), st("rnn_s"), st("win_p"), st("win_s"), st("cmp_p"), st("cmp_s"), st("sel_p"), st("sel_s"))
```

```python
import functools

import jax
import jax.numpy as jnp
from jax import lax
from jax.experimental import pallas as pl
from jax.experimental.pallas import tpu as pltpu

HEAD_DIM = 64
LANES = 128
DIL_PAIRS = ((128, 1), (512, 4), (2048, 16))
DIL_MAX = 2048
DIL_KEYS = 128
CONV_W = 4
LRU_C = 8.0
CMP_LEN = 32
CMP_STRIDE = 16
SEL_BLOCK = 64
SEL_TOP = 16
WIN = 512
NEG = -1e30
FORCE_BONUS = 1e3
EPS = 1e-6
ALIBI_LANES = 6
MASK_BIG = 2.0 ** 100
VMEM_LIMIT = 56 * 1024 * 1024

F32 = jnp.float32
BF16 = jnp.bfloat16


def _cparams(n_grid):
    return pltpu.CompilerParams(dimension_semantics=("arbitrary",) * n_grid,
                                vmem_limit_bytes=VMEM_LIMIT)


def _resident(shape):
    return pl.BlockSpec(shape, lambda *_: (0,) * len(shape), pipeline_mode=pl.Buffered(1))


def _rms(x, g):
    return x * lax.rsqrt(jnp.mean(x * x, axis=-1, keepdims=True) + EPS) * g


def _alibi_slopes(n):
    return 2.0 ** (-8.0 * jnp.arange(1, n + 1, dtype=F32) / n)


def _lane_lo():
    return lax.broadcasted_iota(jnp.int32, (1, LANES), 1) < HEAD_DIM


def _nt_dot(a, b):
    return lax.dot_general(a, b, (((1,), (1,)), ((), ())), preferred_element_type=F32)


def _norm_proj_kernel(x_ref, g_ref, w_ref, *refs, n_out, first_blocks, blocks_per_seq):
    wt_refs = refs[:len(first_blocks)]
    out_refs = refs[len(first_blocks):len(first_blocks) + n_out]
    ot_refs = refs[len(first_blocks) + n_out:]
    h = _rms(x_ref[...], g_ref[...]).astype(BF16)
    off = 0
    for o_ref in out_refs:
        n = o_ref.shape[-1]
        o_ref[...] = jnp.dot(h, w_ref[:, off:off + n], preferred_element_type=F32)
        off += n
    for wt_ref, ot_ref, first in zip(wt_refs, ot_refs, first_blocks):
        @pl.when(pl.program_id(0) % blocks_per_seq >= first)
        def _(wt_ref=wt_ref, ot_ref=ot_ref):
            ot_ref[...] = _nt_dot(wt_ref[...], h)


def _norm_proj(x, g, w, splits, tm, rows_minor=(), seq=None):
    n, d = x.shape
    assert n % tm == 0 and sum(splits) == w.shape[1]
    blocks_per_seq = (seq or n) // tm
    first_blocks = []
    t_specs, t_shapes = [], []
    for w_t, keep in rows_minor:
        assert seq % tm == 0 and keep % tm == 0
        first = (seq - keep) // tm
        first_blocks.append(first)
        t_specs.append(pl.BlockSpec((None, w_t.shape[0], tm), functools.partial(
            lambda first, i: (i // blocks_per_seq, 0, jnp.maximum(i % blocks_per_seq - first, 0)), first)))
        t_shapes.append(jax.ShapeDtypeStruct((n // seq, w_t.shape[0], keep), F32))
    return pl.pallas_call(
        functools.partial(_norm_proj_kernel, n_out=len(splits), first_blocks=tuple(first_blocks),
                          blocks_per_seq=blocks_per_seq),
        grid=(n // tm,),
        in_specs=[pl.BlockSpec((tm, d), lambda i: (i, 0)),
                  _resident((1, d)),
                  _resident(w.shape)] + [_resident(w_t.shape) for w_t, _ in rows_minor],
        out_specs=[pl.BlockSpec((tm, s), lambda i: (i, 0)) for s in splits] + t_specs,
        out_shape=[jax.ShapeDtypeStruct((n, s), F32) for s in splits] + t_shapes,
        compiler_params=_cparams(1),
        name="norm_proj",
    )(x, g.reshape(1, d), w, *[w_t for w_t, _ in rows_minor])


def _out_ffn_kernel(*refs, n_mix, n_chunks, final_norm):
    res_ref = refs[0]
    a_refs = refs[1:1 + n_mix]
    wo_ref, gf_ref, w1_ref, w3_ref, w2_ref = refs[1 + n_mix:6 + n_mix]
    rest = refs[6 + n_mix:]
    go_ref = rest[0] if final_norm else None
    o_ref = rest[-1]
    mix = jnp.concatenate([a_ref[...].astype(BF16) for a_ref in a_refs], axis=1)
    y = res_ref[...] + jnp.dot(mix, wo_ref[...], preferred_element_type=F32)
    h = _rms(y, gf_ref[...]).astype(BF16)
    ch = w1_ref.shape[1] // n_chunks
    for c in range(n_chunks):
        a = jnp.dot(h, w1_ref[:, c * ch:(c + 1) * ch], preferred_element_type=F32)
        b = jnp.dot(h, w3_ref[:, c * ch:(c + 1) * ch], preferred_element_type=F32)
        act = (a * (1.0 / (1.0 + jnp.exp(-a))) * b).astype(BF16)
        y = y + jnp.dot(act, w2_ref[c * ch:(c + 1) * ch, :], preferred_element_type=F32)
    if final_norm:
        y = _rms(y, go_ref[...])
    o_ref[...] = y


def _out_ffn(res, mixes, w_out, g_ffn, w1, w3, w2, g_out, tm):
    n, d = res.shape
    hidden = w1.shape[1]
    n_chunks = hidden // 256
    assert n % tm == 0 and hidden % 256 == 0
    final_norm = g_out is not None
    row = lambda width: pl.BlockSpec((tm, width), lambda i: (i, 0))
    in_specs = [row(d)] + [row(m.shape[1]) for m in mixes] + [_resident(w_out.shape)]
    in_specs += [_resident((1, d)), _resident(w1.shape), _resident(w3.shape), _resident(w2.shape)]
    args = [res, *mixes, w_out, g_ffn.reshape(1, d), w1, w3, w2]
    if final_norm:
        in_specs.append(_resident((1, d)))
        args.append(g_out.reshape(1, d))
    return pl.pallas_call(
        functools.partial(_out_ffn_kernel, n_mix=len(mixes), n_chunks=n_chunks, final_norm=final_norm),
        grid=(n // tm,),
        in_specs=in_specs,
        out_specs=row(d),
        out_shape=jax.ShapeDtypeStruct((n, d), F32),
        compiler_params=_cparams(1),
        name="out_ffn",
    )(*args)


def _dil_attn_kernel(slope_ref, q_ref, k_ref, v_ref, o_ref, *scratch, seq):
    ob_refs, lse_refs = scratch[:3], scratch[3:]
    hp = pl.program_id(1)
    lane_lo = _lane_lo()
    tq, tk = DIL_KEYS, 2 * DIL_KEYS
    head1 = lax.broadcasted_iota(jnp.int32, (2 * tq, 1), 0) >= tq
    slope_rows = jnp.where(head1, slope_ref[pl.ds(2 * hp + 1, 1), 0:1], slope_ref[pl.ds(2 * hp, 1), 0:1])
    iq = lax.broadcasted_iota(jnp.int32, (2 * tq, tk), 0) % tq
    ik = lax.broadcasted_iota(jnp.int32, (2 * tq, tk), 1)
    unroll = 8

    for br, (_, dil) in enumerate(DIL_PAIRS):
        n_blk = seq // dil // tq
        assert (dil * n_blk) % unroll == 0

        def blocks(it, carry, br=br, dil=dil, n_blk=n_blk):
            for u in range(unroll):
                idx = it * unroll + u
                r = idx // n_blk
                blk = idx % n_blk
                a0 = blk * tq
                ka0 = jnp.maximum(blk - 1, 0) * tq
                q_rows = pl.ds(r + dil * a0, tq, stride=dil) if dil > 1 else pl.ds(a0, tq)
                k_rows = pl.ds(r + dil * ka0, tk, stride=dil) if dil > 1 else pl.ds(ka0, tk)
                q = q_ref[q_rows, :] * (HEAD_DIM ** -0.5)
                q2 = jnp.concatenate([jnp.where(lane_lo, q, 0.0), jnp.where(lane_lo, 0.0, q)], axis=0).astype(BF16)
                k = k_ref[k_rows, :].astype(BF16)
                v = v_ref[k_rows, :].astype(BF16)
                da = (a0 - ka0) + iq - ik
                valid = (da >= 0) & (da <= DIL_KEYS)
                s = jnp.where(valid, _nt_dot(q2, k) - slope_rows * (dil * da).astype(F32), NEG)
                m = jnp.max(s, axis=-1, keepdims=True)
                p = jnp.exp(s - m)
                l = jnp.sum(p, axis=-1, keepdims=True)
                out = jnp.dot(p.astype(BF16), v, preferred_element_type=F32) / l
                lse = m + jnp.log(l)
                ob_refs[br][q_rows, :] = jnp.where(lane_lo, out[0:tq], out[tq:2 * tq])
                lse_refs[br][q_rows, :] = jnp.where(lane_lo, lse[0:tq], lse[tq:2 * tq])
            return carry

        lax.fori_loop(0, dil * n_blk // unroll, blocks, 0)

    rows = 512
    def merge(i, carry):
        sl = pl.ds(i * rows, rows)
        ls = [lse_refs[b][sl, :] for b in range(3)]
        m = jnp.maximum(jnp.maximum(ls[0], ls[1]), ls[2])
        es = [jnp.exp(x - m) for x in ls]
        num = es[0] * ob_refs[0][sl, :] + es[1] * ob_refs[1][sl, :] + es[2] * ob_refs[2][sl, :]
        o_ref[sl, :] = num / (es[0] + es[1] + es[2])
        return carry
    lax.fori_loop(0, seq // rows, merge, 0)


def _dil_attn(q, kv, n_heads):
    b, t, width = q.shape
    n_hp = width // LANES
    assert t % (16 * 2 * DIL_KEYS) == 0
    slopes = jnp.broadcast_to(_alibi_slopes(n_heads)[:, None], (n_heads, LANES))
    blk = lambda off: pl.BlockSpec((None, t, LANES), lambda i, j: (i, 0, off + j))
    return pl.pallas_call(
        functools.partial(_dil_attn_kernel, seq=t),
        grid=(b, n_hp),
        in_specs=[_resident((n_heads, LANES)), blk(0), blk(0), blk(n_hp)],
        out_specs=blk(0),
        out_shape=jax.ShapeDtypeStruct((b, t, width), F32),
        scratch_shapes=[pltpu.VMEM((t, LANES), F32)] * 6,
        compiler_params=_cparams(2),
        name="dil_attn",
    )(slopes, q, kv, kv)


def _rglru_kernel(xr_ref, gate_ref, cprev_ref, hprev_ref, cw_ref, cb_ref, wa_ref, ba_ref,
                  wx_ref, bx_ref, lam_ref, y_ref, cnew_ref, hnew_ref,
                  xin_ref, a_ref, u_ref, h_ref):
    ti = pl.program_id(1)
    bb, tc, width = xr_ref.shape
    pad = 8
    tail = CONV_W - 1

    @pl.when(ti == 0)
    def _():
        xin_ref[:, pad - tail:pad, :] = cprev_ref[...]
        h_ref[...] = hprev_ref[...]

    xin_ref[:, pad:pad + tc, :] = xr_ref[...]
    lam = lam_ref[...]
    neg_softplus_c = -LRU_C * (jnp.maximum(-lam, 0.0) + jnp.log1p(jnp.exp(-jnp.abs(lam))))
    for b in range(bb):
        xc = cb_ref[...] + sum(xin_ref[b, pad - tail + k:pad - tail + k + tc, :] * cw_ref[k:k + 1, :]
                               for k in range(CONV_W))
        xcb = xc.astype(BF16)
        ra = jnp.dot(xcb, wa_ref[...], preferred_element_type=F32) + ba_ref[...]
        rx = jnp.dot(xcb, wx_ref[...], preferred_element_type=F32) + bx_ref[...]
        rg = 1.0 / (1.0 + jnp.exp(-ra))
        ig = 1.0 / (1.0 + jnp.exp(-rx))
        log_a = neg_softplus_c * rg
        a = jnp.exp(log_a)
        a_ref[b] = a
        u_ref[b] = jnp.sqrt(jnp.tanh(-log_a) * (1.0 + a * a)) * ig * xc

    def step(t, hs):
        new = []
        for b in range(bb):
            hb = a_ref[b, pl.ds(t, 1), :] * hs[b] + u_ref[b, pl.ds(t, 1), :]
            u_ref[b, pl.ds(t, 1), :] = hb
            new.append(hb)
        return tuple(new)

    hs = lax.fori_loop(0, tc, step, tuple(h_ref[b:b + 1, :] for b in range(bb)), unroll=8)
    for b in range(bb):
        h_ref[b:b + 1, :] = hs[b]
    g = gate_ref[...]
    cdf = 0.5 * (1.0 + jnp.tanh(0.7978845608028654 * (g + 0.044715 * (g * g * g))))
    y_ref[...] = u_ref[...] * (g * cdf)
    xin_ref[:, pad - tail:pad, :] = xin_ref[:, pad + tc - tail:pad + tc, :]
    cnew_ref[...] = xin_ref[:, pad - tail:pad, :]
    hnew_ref[...] = h_ref[...]


def _rglru(xr, gate, conv_prev, h_prev, conv_w, conv_b, wa_bd, ba, wx_bd, bx, lam, bb, tc):
    b, t, width = xr.shape
    assert b % bb == 0 and t % tc == 0 and tc >= CONV_W - 1
    seq_blk = pl.BlockSpec((bb, tc, width), lambda i, j: (i, j, 0))
    vec = _resident((1, width))
    return pl.pallas_call(
        _rglru_kernel,
        grid=(b // bb, t // tc),
        in_specs=[seq_blk, seq_blk,
                  pl.BlockSpec((bb, CONV_W - 1, width), lambda i, j: (i, 0, 0)),
                  pl.BlockSpec((bb, width), lambda i, j: (i, 0)),
                  _resident((CONV_W, width)), vec, _resident(wa_bd.shape), vec,
                  _resident(wx_bd.shape), vec, vec],
        out_specs=[seq_blk,
                   pl.BlockSpec((bb, CONV_W - 1, width), lambda i, j: (i, 0, 0)),
                   pl.BlockSpec((bb, width), lambda i, j: (i, 0))],
        out_shape=[jax.ShapeDtypeStruct((b, t, width), F32),
                   jax.ShapeDtypeStruct((b, CONV_W - 1, width), F32),
                   jax.ShapeDtypeStruct((b, width), F32)],
        scratch_shapes=[pltpu.VMEM((bb, tc + 8, width), F32), pltpu.VMEM((bb, tc, width), F32),
                        pltpu.VMEM((bb, tc, width), F32), pltpu.VMEM((bb, width), F32)],
        compiler_params=_cparams(2),
        name="rglru",
    )(xr, gate, conv_prev, h_prev, conv_w, conv_b.reshape(1, width), wa_bd, ba.reshape(1, width),
      wx_bd, bx.reshape(1, width), lam.reshape(1, width))


def _block_diag(w):
    n, bi, bj = w.shape
    eye = jnp.eye(n, dtype=w.dtype)
    return (eye[:, None, :, None] * w[:, :, None, :]).reshape(n * bi, n * bj)


def _compress_kernel(*refs, n_ch, merged, by_row=False):
    x_refs, (w_ref, pe_ref, o_ref) = refs[:-3], refs[-3:]
    per_half = len(x_refs) // 2
    half = per_half * LANES
    pe_pad = 16
    for c in range(2):
        xs = jnp.concatenate([x_ref[s] if by_row else x_ref[pl.ds(s, n_ch, stride=CMP_STRIDE), :]
                              for s in range(CMP_STRIDE)
                              for x_ref in x_refs[c * per_half:(c + 1) * per_half]], axis=1).astype(BF16)
        acc = []
        for i in range(2):
            pe_rows = jnp.concatenate([pe_ref[i * CMP_STRIDE + s:i * CMP_STRIDE + s + 1, c * half:(c + 1) * half]
                                       for s in range(CMP_STRIDE)], axis=1)
            lhs = jnp.concatenate([xs, jnp.broadcast_to(pe_rows, (pe_pad, pe_rows.shape[1])).astype(BF16)], axis=0)
            acc.append(jnp.dot(lhs, w_ref[i * 2 + c], preferred_element_type=F32))
        pe_acc = acc[0][n_ch:n_ch + 1, :] + acc[1][n_ch:n_ch + 1, :]
        acc = [a[0:n_ch, :] for a in acc]
        if merged:
            o_ref[:, c * half:(c + 1) * half] = acc[0] + pltpu.roll(acc[1], n_ch - 1, axis=0) + pe_acc[0:1, :]
        else:
            o_ref[:, c * half:(c + 1) * half] = acc[0] + pe_acc[0:1, :]
            o_ref[:, (2 + c) * half:(3 + c) * half] = acc[1]


def _compress_weights(w_cmp, pe_cmp, n_groups):
    n_r = CMP_LEN // CMP_STRIDE
    gw = n_groups * HEAD_DIM
    w = w_cmp.reshape(n_r, CMP_STRIDE, 2, HEAD_DIM, HEAD_DIM).transpose(0, 2, 1, 3, 4)
    eye = jnp.eye(n_groups, dtype=w.dtype)
    wbd = (eye[None, None, None, :, None, :, None] * w[:, :, :, None, :, None, :]).reshape(n_r * 2, CMP_STRIDE * gw, gw)
    pe = jnp.broadcast_to(pe_cmp[:, :, None, :], (CMP_LEN, 2, n_groups, HEAD_DIM)).reshape(CMP_LEN, -1)
    return wbd.astype(BF16), pe


def _compress(ctx, wbd, pe):
    b, tc, width = ctx.shape
    n_ch = tc // CMP_STRIDE
    return pl.pallas_call(
        functools.partial(_compress_kernel, n_ch=n_ch, merged=True),
        grid=(b,),
        in_specs=[pl.BlockSpec((None, tc, LANES), functools.partial(lambda j, i: (i, 0, j), j))
                  for j in range(width // LANES)] + [_resident(wbd.shape), _resident(pe.shape)],
        out_specs=pl.BlockSpec((None, n_ch, width), lambda i: (i, 0, 0)),
        out_shape=jax.ShapeDtypeStruct((b, n_ch, width), F32),
        compiler_params=_cparams(1),
        name="nsa_compress",
    )(*([ctx] * (width // LANES)), wbd, pe)


def _alibi_tables(n_heads, pos):
    t = pos.shape[0]
    slopes = _alibi_slopes(n_heads)
    s1 = slopes.astype(BF16).astype(F32)
    s2 = (slopes - s1).astype(BF16).astype(F32)
    s3 = (slopes - s1 - s2).astype(BF16).astype(F32)
    qtab = jnp.zeros((n_heads, LANES), F32).at[:, HEAD_DIM:HEAD_DIM + ALIBI_LANES].set(
        jnp.stack([s1, s2, s3, s1, s2, s3], axis=-1))
    hi = (pos // SEL_BLOCK * SEL_BLOCK).astype(F32)
    lo = (pos % SEL_BLOCK).astype(F32)
    ktab = jnp.zeros((t, LANES), F32).at[:, HEAD_DIM:HEAD_DIM + ALIBI_LANES].set(
        jnp.stack([hi, hi, hi, lo, lo, lo], axis=-1))
    return qtab, ktab


def _stack_heads(q_ref, gl, qtab_ref, first_head):
    lo = _lane_lo()
    parts = []
    for r in range(4):
        c0 = gl * 4 * HEAD_DIM + (r // 2) * LANES
        slab = q_ref[:, c0:c0 + LANES] * (HEAD_DIM ** -0.5)
        if r % 2:
            slab = pltpu.roll(slab, HEAD_DIM, axis=1)
        parts.append(jnp.where(lo, slab, qtab_ref[pl.ds(first_head + r, 1), :]).astype(BF16))
    return jnp.concatenate(parts, axis=0)


def _stage_kv_aug(k_ref, v_ref, ktab_ref, gl):
    lo = _lane_lo()
    own = lo if gl == 0 else ~lo
    k_own = jnp.where(own, k_ref[...], 0.0)
    v_own = jnp.where(own, v_ref[...], 0.0)
    k_low = k_own if gl == 0 else pltpu.roll(k_own, HEAD_DIM, axis=1)
    return (k_low + ktab_ref[...]).astype(BF16), (v_own + pltpu.roll(v_own, HEAD_DIM, axis=1)).astype(BF16)


def _unstack_heads(o_ref, gl, out, tq):
    lo = _lane_lo()
    base = gl * 4 * HEAD_DIM
    o_ref[:, base:base + LANES] = jnp.where(lo, out[0:tq], out[tq:2 * tq])
    o_ref[:, base + LANES:base + 2 * LANES] = jnp.where(lo, out[2 * tq:3 * tq], out[3 * tq:4 * tq])


def _cmp_select_kernel(qtab_ref, ktab_ref, c2s_ref, q_ref, kc_ref, vc_ref, o_ref, sel_ref, *, n_c, n_sel):
    gp, qi = pl.program_id(1), pl.program_id(2)
    tq = q_ref.shape[0]
    n_ch = kc_ref.shape[0]
    qp = qi * tq + lax.broadcasted_iota(jnp.int32, (tq, 1), 0)
    n_idx = lax.broadcasted_iota(jnp.int32, (1, n_ch), 1)
    cmp_end = n_idx * CMP_STRIDE + (CMP_LEN - 1)
    valid = (cmp_end <= qp) & (n_idx < n_c)
    blk = lax.broadcasted_iota(jnp.int32, (1, LANES), 1)
    cb = qp // SEL_BLOCK
    blk_ok = (blk <= cb) & (blk < n_sel)
    forced = (blk == 0) | (blk == cb) | (blk == cb - 1)
    blk_t = lax.broadcasted_iota(jnp.int32, (n_sel, tq), 0)
    valid4 = jnp.concatenate([valid] * 4, axis=0)
    for gl in range(2):
        k_aug, v_both = _stage_kv_aug(kc_ref, vc_ref, ktab_ref, gl)
        q_all = _stack_heads(q_ref, gl, qtab_ref, (2 * gp + gl) * 4)
        s = jnp.where(valid4, _nt_dot(q_all, k_aug), NEG)
        m = jnp.max(s, axis=-1, keepdims=True)
        p = jnp.where(valid4, jnp.exp(s - m), 0.0)
        p = (p / jnp.maximum(jnp.sum(p, axis=-1, keepdims=True), 1e-30)).astype(BF16)
        _unstack_heads(o_ref, gl, jnp.dot(p, v_both, preferred_element_type=F32), tq)
        imp_rows = jnp.dot(p, c2s_ref[...], preferred_element_type=F32)
        imp = imp_rows[0:tq] + imp_rows[tq:2 * tq] + imp_rows[2 * tq:3 * tq] + imp_rows[3 * tq:4 * tq]
        score = jnp.where(blk_ok, imp + FORCE_BONUS * forced.astype(F32), NEG)
        score_t = score.T[0:n_sel, :]
        rank = jnp.zeros((n_sel, tq), F32)
        for k in range(n_sel):
            row = score_t[k:k + 1, :]
            ahead = (row > score_t) | ((row == score_t) & (k < blk_t))
            rank = rank + ahead.astype(F32)
        chosen = ((rank < SEL_TOP) & (score_t > 0.5 * NEG)).astype(F32)
        chosen = jnp.concatenate([chosen, jnp.zeros((LANES - n_sel, tq), F32)], axis=0) if n_sel < LANES else chosen
        sel_ref[gl] = chosen.T.astype(BF16)


def _cmp_select(q, kvc, n_c, n_sel, n_groups, tq):
    b, t, width = q.shape
    n_ch = kvc.shape[1]
    n_gp = n_groups // 2
    n_heads = width // HEAD_DIM
    assert tq == LANES and n_sel <= LANES and n_sel % 8 == 0 and t % tq == 0
    qtab, ktab = _alibi_tables(n_heads, jnp.arange(n_ch) * CMP_STRIDE + (CMP_LEN - 1))
    cs = jnp.arange(n_ch)[:, None] * CMP_STRIDE
    ss = jnp.arange(LANES)[None, :] * SEL_BLOCK
    ov = jnp.maximum(jnp.minimum(cs + CMP_LEN, ss + SEL_BLOCK) - jnp.maximum(cs, ss), 0).astype(F32) / CMP_STRIDE
    c2s = jnp.where((jnp.arange(n_ch)[:, None] < n_c) & (jnp.arange(LANES)[None, :] < n_sel), ov, 0.0).astype(BF16)
    return pl.pallas_call(
        functools.partial(_cmp_select_kernel, n_c=n_c, n_sel=n_sel),
        grid=(b, n_gp, t // tq),
        in_specs=[_resident(qtab.shape), _resident(ktab.shape), _resident(c2s.shape),
                  pl.BlockSpec((None, tq, 2 * 4 * HEAD_DIM), lambda i, j, k: (i, k, j)),
                  pl.BlockSpec((None, n_ch, LANES), lambda i, j, k: (i, 0, j)),
                  pl.BlockSpec((None, n_ch, LANES), lambda i, j, k: (i, 0, n_gp + j))],
        out_specs=[pl.BlockSpec((None, tq, 2 * 4 * HEAD_DIM), lambda i, j, k: (i, k, j)),
                   pl.BlockSpec((None, 2, tq, LANES), lambda i, j, k: (i, j, k, 0))],
        out_shape=[jax.ShapeDtypeStruct((b, t, width), F32),
                   jax.ShapeDtypeStruct((b, n_groups, t, LANES), BF16)],
        compiler_params=_cparams(3),
        name="nsa_cmp_select",
    )(qtab, ktab, c2s, q, kvc, kvc)


def _sel_attn_kernel(qtab_ref, ktab_ref, q_ref, sel_ref, k_ref, v_ref, o_ref, ka0, ka1, v0, v1, qv_ref, *, tk):
    gp, qi = pl.program_id(1), pl.program_id(2)
    tq = q_ref.shape[0]
    staged = ((ka0, v0), (ka1, v1))
    n_var = qv_ref.shape[0]
    blocks_per_tile = tk // SEL_BLOCK
    mask_lane0 = HEAD_DIM + ALIBI_LANES

    @pl.when(qi == 0)
    def _():
        for gl, (ka, vs) in enumerate(staged):
            ka[...], vs[...] = _stage_kv_aug(k_ref, v_ref, ktab_ref, gl)

    q0 = qi * tq
    n_full = q0 // tk
    qp = q0 + lax.broadcasted_iota(jnp.int32, (4 * tq, 1), 0) % tq
    lane = lax.broadcasted_iota(jnp.int32, (1, LANES), 1)
    mask_lanes = (lane >= mask_lane0) & (lane < mask_lane0 + blocks_per_tile)
    for gl, (ka, vs) in enumerate(staged):
        q_base = _stack_heads(q_ref, gl, qtab_ref, (2 * gp + gl) * 4)
        not_sel = sel_ref[gl].astype(F32) - 1.0
        for var in range(n_var):
            shift = (mask_lane0 - var * blocks_per_tile) % LANES
            flags = jnp.where(mask_lanes, pltpu.roll(not_sel, shift, axis=1) if shift else not_sel, 0.0).astype(BF16)
            qv_ref[var] = q_base + jnp.concatenate([flags] * 4, axis=0)

        def scores(kt, width, ka=ka):
            return _nt_dot(qv_ref[kt], ka[kt * tk:kt * tk + width, :])

        def absorb(kt, s, carry, masked, vs=vs):
            m_old, l_old, acc_old = carry
            width = s.shape[1]
            if masked:
                kpos = kt * tk + lax.broadcasted_iota(jnp.int32, (1, width), 1)
                s = jnp.where(kpos <= qp, s, NEG)
            m_new = jnp.maximum(m_old, jnp.max(s, axis=-1, keepdims=True))
            alpha = jnp.exp(m_old - m_new)
            p = jnp.exp(s - m_new)
            l_new = alpha * l_old + jnp.sum(p, axis=-1, keepdims=True)
            acc_new = alpha * acc_old + jnp.dot(p.astype(BF16), vs[kt * tk:kt * tk + width, :], preferred_element_type=F32)
            return m_new, l_new, acc_new

        def run(n, last_width):
            carry = (jnp.full((4 * tq, 1), NEG, F32), jnp.zeros((4 * tq, 1), F32), jnp.zeros((4 * tq, LANES), F32))
            ss = [scores(kt, tk) for kt in range(n)] + [scores(n, last_width)]
            for kt in range(n):
                carry = absorb(kt, ss[kt], carry, False)
            _, l, acc = absorb(n, ss[n], carry, True)
            return acc / jnp.maximum(l, 1e-30)

        variants = [functools.partial(run, n, w) for n in range(n_var) for w in (tk // 2, tk)]
        out = lax.switch(2 * n_full + (q0 % tk) // (tk // 2), variants)
        _unstack_heads(o_ref, gl, out, tq)


def _sel_attn(q, kv, sel, n_groups, tq, tk):
    b, t, width = q.shape
    n_gp = n_groups // 2
    n_heads = width // HEAD_DIM
    blocks_per_tile = tk // SEL_BLOCK
    assert t % tk == 0 and (tk // 2) % tq == 0 and HEAD_DIM + ALIBI_LANES + blocks_per_tile <= LANES
    qtab, ktab = _alibi_tables(n_heads, jnp.arange(t))
    blk_lane = HEAD_DIM + ALIBI_LANES + (jnp.arange(t) // SEL_BLOCK) % blocks_per_tile
    ktab = ktab + jnp.where(blk_lane[:, None] == jnp.arange(LANES)[None, :], MASK_BIG, 0.0)
    return pl.pallas_call(
        functools.partial(_sel_attn_kernel, tk=tk),
        grid=(b, n_gp, t // tq),
        in_specs=[_resident(qtab.shape), _resident(ktab.shape),
                  pl.BlockSpec((None, tq, 2 * 4 * HEAD_DIM), lambda i, j, k: (i, k, j)),
                  pl.BlockSpec((None, 2, tq, LANES), lambda i, j, k: (i, j, k, 0)),
                  pl.BlockSpec((None, t, LANES), lambda i, j, k: (i, 0, j)),
                  pl.BlockSpec((None, t, LANES), lambda i, j, k: (i, 0, n_gp + j))],
        out_specs=pl.BlockSpec((None, tq, 2 * 4 * HEAD_DIM), lambda i, j, k: (i, k, j)),
        out_shape=jax.ShapeDtypeStruct((b, t, width), F32),
        scratch_shapes=[pltpu.VMEM((t, LANES), BF16)] * 4 + [pltpu.VMEM((t // tk, 4 * tq, LANES), BF16)],
        compiler_params=_cparams(3),
        name="nsa_sel_attn",
    )(qtab, ktab, q, sel, kv, kv)


def _win_combine_kernel(qtab_ref, ktab_ref, q_ref, gate_ref, ocmp_ref, osel_ref, k_ref, v_ref, o_ref,
                        ka0, ka1, v0, v1, *, seq):
    gp, qi = pl.program_id(1), pl.program_id(2)
    tq = q_ref.shape[0]
    tk = WIN + tq
    lo = _lane_lo()
    staged = ((ka0, v0), (ka1, v1))

    @pl.when(qi == 0)
    def _():
        for gl, (ka, vs) in enumerate(staged):
            ka[...], vs[...] = _stage_kv_aug(k_ref, v_ref, ktab_ref, gl)

    k0 = jnp.clip(qi * tq - WIN, 0, seq - tk)
    rows = pl.ds(pl.multiple_of(k0, tq), tk)
    qp = qi * tq + lax.broadcasted_iota(jnp.int32, (4 * tq, 1), 0) % tq
    dist = qp - (k0 + lax.broadcasted_iota(jnp.int32, (1, tk), 1))
    ok = (dist >= 0) & (dist <= WIN)
    gates = 1.0 / (1.0 + jnp.exp(-gate_ref[...]))
    for gl, (ka, vs) in enumerate(staged):
        q_all = _stack_heads(q_ref, gl, qtab_ref, (2 * gp + gl) * 4)
        s = jnp.where(ok, _nt_dot(q_all, ka[rows, :]), NEG)
        m = jnp.max(s, axis=-1, keepdims=True)
        p = jnp.where(ok, jnp.exp(s - m), 0.0)
        l = jnp.maximum(jnp.sum(p, axis=-1, keepdims=True), 1e-30)
        out = jnp.dot(p.astype(BF16), vs[rows, :], preferred_element_type=F32) / l
        for half in range(2):
            cols = slice(gl * 4 * HEAD_DIM + half * LANES, gl * 4 * HEAD_DIM + (half + 1) * LANES)
            o_win = jnp.where(lo, out[2 * half * tq:(2 * half + 1) * tq], out[(2 * half + 1) * tq:(2 * half + 2) * tq])
            c0 = (gl * 4 + 2 * half) * 3
            g_cmp, g_sel, g_win = (jnp.where(lo, gates[:, c0 + i:c0 + i + 1], gates[:, c0 + 3 + i:c0 + 4 + i])
                                   for i in range(3))
            o_ref[:, cols] = g_cmp * ocmp_ref[:, cols] + g_sel * osel_ref[:, cols] + g_win * o_win


def _win_combine(q, kv, gates, o_cmp, o_sel, n_groups, tq):
    b, t, width = q.shape
    n_gp = n_groups // 2
    n_heads = width // HEAD_DIM
    assert t % tq == 0 and t >= WIN + tq
    qtab, ktab = _alibi_tables(n_heads, jnp.arange(t))
    qblk = pl.BlockSpec((None, tq, 2 * 4 * HEAD_DIM), lambda i, j, k: (i, k, j))
    return pl.pallas_call(
        functools.partial(_win_combine_kernel, seq=t),
        grid=(b, n_gp, t // tq),
        in_specs=[_resident(qtab.shape), _resident(ktab.shape), qblk,
                  pl.BlockSpec((None, tq, LANES), lambda i, j, k: (i, k, j)),
                  qblk, qblk,
                  pl.BlockSpec((None, t, LANES), lambda i, j, k: (i, 0, j)),
                  pl.BlockSpec((None, t, LANES), lambda i, j, k: (i, 0, n_gp + j))],
        out_specs=qblk,
        out_shape=jax.ShapeDtypeStruct((b, t, width), F32),
        scratch_shapes=[pltpu.VMEM((t, LANES), BF16)] * 4,
        compiler_params=_cparams(3),
        name="nsa_win_combine",
    )(qtab, ktab, q, gates, o_cmp, o_sel, kv, kv)


def _nsa_prompt(q, kv_cmp, kv_sel, kv_win, gates, wbd, pe, n_groups):
    t = q.shape[1]
    kvc = _compress(kv_cmp, wbd, pe)
    n_c = t // CMP_STRIDE - CMP_LEN // CMP_STRIDE + 1
    n_sel = -(-t // SEL_BLOCK)
    o_cmp, sel = _cmp_select(q, kvc, n_c, n_sel, n_groups, LANES)
    o_sel = _sel_attn(q, kv_sel, sel, n_groups, LANES, 8 * LANES)
    return _win_combine(q, kv_win, gates, o_cmp, o_sel, n_groups, LANES)


def _head_block_diag(x, n_rep, lanes_per_head, rows_per_head):
    tiled = jnp.concatenate([x] * n_rep, axis=0)
    row_h = lax.broadcasted_iota(jnp.int32, tiled.shape, 0) // rows_per_head
    lane_h = lax.broadcasted_iota(jnp.int32, tiled.shape, 1) // lanes_per_head
    return jnp.where(row_h == lane_h, tiled, 0.0)


def _head_diag_rows(o, n_rep, lanes_per_head, rows_per_head):
    row_h = lax.broadcasted_iota(jnp.int32, o.shape, 0) // rows_per_head
    lane_h = lax.broadcasted_iota(jnp.int32, o.shape, 1) // lanes_per_head
    kept = jnp.where(row_h == lane_h, o, 0.0)
    return jnp.sum(kept.reshape(n_rep, rows_per_head, o.shape[1]), axis=0)


def _dil_count(dist):
    count = jnp.zeros(dist.shape, F32)
    for window, dil in DIL_PAIRS:
        assert dil & (dil - 1) == 0
        count = count + ((dist >= 0) & (dist <= window) & ((dist & (dil - 1)) == 0)).astype(F32)
    return count


def _dil_sample_kernel(slope_ref, q_ref, kvn_ref, cache_ref, o_ref, ctx_ref):
    ts, width = q_ref.shape
    past = cache_ref.shape[-1]
    n_heads = width // HEAD_DIM
    rows = n_heads * ts
    slope = slope_ref[:, 0:1]
    k_t = cache_ref[0].reshape(width, past)
    v_t = cache_ref[1].reshape(width, past)
    new = jnp.concatenate([kvn_ref[...], jnp.zeros((LANES - ts, 2 * width), F32)], axis=0)
    qbd = _head_block_diag(q_ref[...] * (HEAD_DIM ** -0.5), n_heads, HEAD_DIM, ts).astype(BF16)
    t_row = lax.broadcasted_iota(jnp.int32, (rows, 1), 0) % ts
    d_old = (past + t_row) - lax.broadcasted_iota(jnp.int32, (1, past), 1)
    j_new = lax.broadcasted_iota(jnp.int32, (1, LANES), 1)
    d_new = jnp.where(j_new < ts, t_row - j_new, -1)
    c_old, c_new = _dil_count(d_old), _dil_count(d_new)
    s_old = jnp.dot(qbd, k_t.astype(BF16), preferred_element_type=F32)
    s_new = _nt_dot(qbd, new[:, 0:width].astype(BF16))
    s_old = jnp.where(c_old > 0, s_old - slope * d_old.astype(F32), NEG)
    s_new = jnp.where(c_new > 0, s_new - slope * d_new.astype(F32), NEG)
    m = jnp.maximum(jnp.max(s_old, axis=-1, keepdims=True), jnp.max(s_new, axis=-1, keepdims=True))
    p_old = c_old * jnp.exp(s_old - m)
    p_new = c_new * jnp.exp(s_new - m)
    l = jnp.sum(p_old, axis=-1, keepdims=True) + jnp.sum(p_new, axis=-1, keepdims=True)
    o = _nt_dot(p_old.astype(BF16), v_t.astype(BF16)) + jnp.dot(p_new.astype(BF16), new[:, width:].astype(BF16),
                                                               preferred_element_type=F32)
    o_ref[...] = _head_diag_rows(o / l, n_heads, HEAD_DIM, ts)
    both = cache_ref[...].reshape(2 * width, past)
    rolled = pltpu.roll(both, past - ts, axis=1)
    new_t = pltpu.roll(new.T, LANES - ts, axis=1)
    ctx_ref[:, 0:past - LANES] = rolled[:, 0:past - LANES]
    ctx_ref[:, past - LANES:past] = jnp.where(j_new >= LANES - ts, new_t, rolled[:, past - LANES:past])


def _dil_sample(q, kv_new, cache_t, n_heads):
    b, ts, width = q.shape
    past = cache_t.shape[-1]
    assert ts % 8 == 0 and past % LANES == 0
    slopes = jnp.broadcast_to(jnp.repeat(_alibi_slopes(n_heads), ts)[:, None], (n_heads * ts, LANES))
    per_b = lambda rows, w: pl.BlockSpec((None, rows, w), lambda i: (i, 0, 0))
    return pl.pallas_call(
        _dil_sample_kernel,
        grid=(b,),
        in_specs=[_resident(slopes.shape), per_b(ts, width), per_b(ts, 2 * width),
                  pl.BlockSpec((None,) + cache_t.shape[1:], lambda i: (i, 0, 0, 0, 0))],
        out_specs=[per_b(ts, width), per_b(2 * width, past)],
        out_shape=[jax.ShapeDtypeStruct((b, ts, width), F32), jax.ShapeDtypeStruct((b, 2 * width, past), F32)],
        compiler_params=_cparams(1),
        name="dil_sample",
    )(slopes, q, kv_new, cache_t)


def _compress_pool_kernel(pool_ref, w_ref, pe_ref, o_ref, *slabs):
    pages, _, n_groups, hd, page_rows = pool_ref.shape
    pairs = n_groups * hd // LANES
    chunks = page_rows // CMP_STRIDE
    out_row = lax.broadcasted_iota(jnp.int32, (page_rows, page_rows), 0)
    src_row = (out_row % chunks) * CMP_STRIDE + out_row // chunks
    perm = (src_row == lax.broadcasted_iota(jnp.int32, (page_rows, page_rows), 1)).astype(BF16)

    batch = 8

    def to_rows(pb, carry):
        for i in range(batch):
            p = pb * batch + i
            for c in range(2):
                for gp in range(pairs):
                    tile_t = pool_ref[p, c, pl.ds(gp * (LANES // hd), LANES // hd)].reshape(LANES, page_rows)
                    rows = _nt_dot(perm, tile_t.astype(BF16))
                    for s in range(CMP_STRIDE):
                        slabs[c * pairs + gp][s, pl.ds(pl.multiple_of(p * chunks, chunks), chunks), :] = (
                            rows[s * chunks:(s + 1) * chunks, :])
        return carry

    assert pages % batch == 0 and chunks == 8
    lax.fori_loop(0, pages // batch, to_rows, 0)
    _compress_kernel(*slabs, w_ref, pe_ref, o_ref, n_ch=pages * chunks, merged=False, by_row=True)


def _compress_pool(pool, wbd, pe, pages_per_step):
    n_pages, _, n_groups, hd, page_rows = pool.shape
    width = 2 * n_groups * hd
    assert n_pages % pages_per_step == 0 and page_rows % (8 * CMP_STRIDE) == 0 and page_rows == LANES
    n_ch = pages_per_step * page_rows // CMP_STRIDE
    return pl.pallas_call(
        _compress_pool_kernel,
        grid=(n_pages // pages_per_step,),
        in_specs=[pl.BlockSpec((pages_per_step,) + pool.shape[1:], lambda i: (i, 0, 0, 0, 0)),
                  _resident(wbd.shape), _resident(pe.shape)],
        out_specs=pl.BlockSpec((n_ch, 2 * width), lambda i: (i, 0)),
        out_shape=jax.ShapeDtypeStruct((n_pages * page_rows // CMP_STRIDE, 2 * width), F32),
        scratch_shapes=[pltpu.VMEM((CMP_STRIDE, n_ch, LANES), F32)] * (width // LANES),
        compiler_params=_cparams(1),
        name="nsa_compress_pool",
    )(pool, wbd, pe)


def _sample_query(q_ref, ts, n_groups):
    gw = n_groups * HEAD_DIM
    n_rep = q_ref.shape[1] // gw
    return jnp.concatenate(
        [_head_block_diag(q_ref[:, r * gw:(r + 1) * gw] * (HEAD_DIM ** -0.5), n_groups, HEAD_DIM, ts) for r in range(n_rep)],
        axis=0).astype(BF16)


def _sample_rows_out(o, ts, n_groups):
    per_r = n_groups * ts
    return [_head_diag_rows(o[r * per_r:(r + 1) * per_r, :], n_groups, HEAD_DIM, ts) for r in range(o.shape[0] // per_r)]


def _nsa_sample_cmp_kernel(pt_ref, slope_ref, c2s_ref, q_ref, kvn_ref, cwin_ref, *rest, pages, past, n_c, n_sel):
    del pt_ref
    ab_refs = rest[:pages]
    xcmp_ref, xwin_ref, sel_ref, winout_ref, ab_scr, kwin_scr, vwin_scr = rest[pages:]
    j = pl.program_id(1)
    ts = q_ref.shape[0]
    n_groups = kvn_ref.shape[1] // (2 * HEAD_DIM)
    gw = n_groups * HEAD_DIM
    for k in range(pages):
        ab_scr[pl.ds(pl.multiple_of((j * pages + k) * 8, 8), 8), :] = ab_refs[k][...]

    @pl.when(j == pl.num_programs(1) - 1)
    def _():
        n_ch = ab_scr.shape[0]
        n_rows = slope_ref.shape[0]
        slope = slope_ref[:, 0:1]
        qbd = _sample_query(q_ref, ts, n_groups)
        t_row = lax.broadcasted_iota(jnp.int32, (n_rows, 1), 0) % ts
        qp = past + t_row
        kvc = ab_scr[:, 0:2 * gw] + pltpu.roll(ab_scr[:, 2 * gw:4 * gw], n_ch - 1, axis=0)
        n_idx = lax.broadcasted_iota(jnp.int32, (1, n_ch), 1)
        cmp_end = n_idx * CMP_STRIDE + (CMP_LEN - 1)
        valid = (cmp_end <= qp) & (n_idx < n_c)
        s = jnp.where(valid, _nt_dot(qbd, kvc[:, 0:gw].astype(BF16)) - slope * (qp - cmp_end).astype(F32), NEG)
        m = jnp.max(s, axis=-1, keepdims=True)
        p = jnp.where(valid, jnp.exp(s - m), 0.0)
        p = (p / jnp.maximum(jnp.sum(p, axis=-1, keepdims=True), 1e-30)).astype(BF16)
        for r, x in enumerate(_sample_rows_out(jnp.dot(p, kvc[:, gw:2 * gw].astype(BF16), preferred_element_type=F32), ts, n_groups)):
            xcmp_ref[r * ts:(r + 1) * ts, :] = x
        imp_rows = jnp.dot(p, c2s_ref[...], preferred_element_type=F32)
        gt = n_groups * ts
        imp = jnp.sum(imp_rows.reshape(n_rows // gt, gt, imp_rows.shape[1]), axis=0)
        blk = lax.broadcasted_iota(jnp.int32, (1, imp.shape[1]), 1)
        cb = (past + lax.broadcasted_iota(jnp.int32, (gt, 1), 0) % ts) // SEL_BLOCK
        forced = (blk == 0) | (blk == cb) | (blk == cb - 1)
        score = jnp.where((blk <= cb) & (blk < n_sel), imp + FORCE_BONUS * forced.astype(F32), NEG)
        removed = -3e38
        cur = jnp.where(blk < n_sel, score, removed)
        chosen = jnp.zeros(score.shape, F32)
        for _ in range(SEL_TOP):
            top = jnp.max(cur, axis=-1, keepdims=True)
            first = jnp.min(jnp.where(cur == top, blk, imp.shape[1]), axis=-1, keepdims=True)
            pick = blk == first
            chosen = jnp.where(pick & (top > 0.5 * NEG), 1.0, chosen)
            cur = jnp.where(pick, removed, cur)
        n_steps = sel_ref.shape[0]
        per_step = (past // SEL_BLOCK) // n_steps
        lane = lax.broadcasted_iota(jnp.int32, (1, LANES), 1)
        for st in range(n_steps):
            shifted = chosen if st == 0 else pltpu.roll(chosen, chosen.shape[1] - st * per_step, axis=1)
            n_here = per_step + (n_sel - n_steps * per_step if st == n_steps - 1 else 0)
            piece = jnp.where(lane < n_here, shifted[:, 0:LANES], 0.0)
            sel_ref[st] = jnp.concatenate([piece] * (n_rows // gt), axis=0).astype(BF16)
        w_past = cwin_ref.shape[0]
        pad = kwin_scr.shape[0] - w_past
        kwin_scr[0:w_past, :] = cwin_ref[:, 0:gw].astype(BF16)
        vwin_scr[0:w_past, :] = cwin_ref[:, gw:2 * gw].astype(BF16)
        zeros = jnp.zeros((pad - ts, gw), F32)
        kwin_scr[w_past:, :] = jnp.concatenate([kvn_ref[:, 0:gw], zeros], axis=0).astype(BF16)
        vwin_scr[w_past:, :] = jnp.concatenate([kvn_ref[:, gw:2 * gw], zeros], axis=0).astype(BF16)
        dist = (w_past + t_row) - lax.broadcasted_iota(jnp.int32, (1, kwin_scr.shape[0]), 1)
        ok = (dist >= 0) & (dist <= WIN)
        s = jnp.where(ok, _nt_dot(qbd, kwin_scr[...]) - slope * dist.astype(F32), NEG)
        m = jnp.max(s, axis=-1, keepdims=True)
        p = jnp.where(ok, jnp.exp(s - m), 0.0)
        l = jnp.maximum(jnp.sum(p, axis=-1, keepdims=True), 1e-30)
        o = jnp.dot(p.astype(BF16), vwin_scr[...], preferred_element_type=F32) / l
        for r, x in enumerate(_sample_rows_out(o, ts, n_groups)):
            xwin_ref[r * ts:(r + 1) * ts, :] = x
        winout_ref[0:w_past - ts, :] = cwin_ref[ts:w_past, :]
        winout_ref[w_past - ts:w_past, :] = kvn_ref[...]


def _sample_slopes(n_groups, n_rep, ts):
    slopes = _alibi_slopes(n_groups * n_rep).reshape(n_groups, n_rep).T
    return jnp.broadcast_to(jnp.repeat(slopes.reshape(-1), ts)[:, None], (n_rep * n_groups * ts, LANES))


def _nsa_sample_cmp(q, kv_win_new, cache_win, ab_pool, page_table, n_groups, sel_steps, pages):
    b, ts, qw = q.shape
    n_rep = qw // HEAD_DIM // n_groups
    gw = n_groups * HEAD_DIM
    n_pages = page_table.shape[1]
    chunks_per_page = ab_pool.shape[1]
    past = n_pages * chunks_per_page * CMP_STRIDE
    n_ch = (past + ts) // CMP_STRIDE
    assert n_ch == n_pages * chunks_per_page and n_pages % pages == 0
    n_c = n_ch - CMP_LEN // CMP_STRIDE + 1
    n_sel = -(-(past + ts) // SEL_BLOCK)
    assert n_sel <= 2 * LANES and n_sel - past // SEL_BLOCK + past // SEL_BLOCK // sel_steps <= LANES
    n_rows = n_rep * n_groups * ts
    slopes = _sample_slopes(n_groups, n_rep, ts)
    cs = jnp.arange(n_ch)[:, None] * CMP_STRIDE
    ss = jnp.arange(2 * LANES)[None, :] * SEL_BLOCK
    ov = jnp.maximum(jnp.minimum(cs + CMP_LEN, ss + SEL_BLOCK) - jnp.maximum(cs, ss), 0).astype(F32) / CMP_STRIDE
    c2s = jnp.where((jnp.arange(n_ch)[:, None] < n_c) & (jnp.arange(2 * LANES)[None, :] < n_sel), ov, 0.0).astype(BF16)
    w_past = cache_win.shape[1]
    per_b = lambda rows, w: pl.BlockSpec((None, rows, w), lambda i, j, pt: (i, 0, 0))
    page_spec = lambda k: pl.BlockSpec((None, chunks_per_page, ab_pool.shape[2]),
                                       lambda i, j, pt: (pt[i, j * pages + k], 0, 0))
    const = lambda shape: pl.BlockSpec(shape, lambda i, j, pt: (0,) * len(shape), pipeline_mode=pl.Buffered(1))
    return pl.pallas_call(
        functools.partial(_nsa_sample_cmp_kernel, pages=pages, past=past, n_c=n_c, n_sel=n_sel),
        grid_spec=pltpu.PrefetchScalarGridSpec(
            num_scalar_prefetch=1,
            grid=(b, n_pages // pages),
            in_specs=[const(slopes.shape), const(c2s.shape), per_b(ts, qw), per_b(ts, 2 * gw), per_b(w_past, 2 * gw)]
                     + [page_spec(k) for k in range(pages)],
            out_specs=[per_b(n_rep * ts, gw), per_b(n_rep * ts, gw),
                       pl.BlockSpec((None, sel_steps, n_rows, LANES), lambda i, j, pt: (i, 0, 0, 0)),
                       per_b(w_past, 2 * gw)],
            scratch_shapes=[pltpu.VMEM((n_ch, ab_pool.shape[2]), F32),
                            pltpu.VMEM((w_past + LANES, gw), BF16), pltpu.VMEM((w_past + LANES, gw), BF16)]),
        out_shape=[jax.ShapeDtypeStruct((b, n_rep * ts, gw), F32), jax.ShapeDtypeStruct((b, n_rep * ts, gw), F32),
                   jax.ShapeDtypeStruct((b, sel_steps, n_rows, LANES), BF16),
                   jax.ShapeDtypeStruct(cache_win.shape, F32)],
        compiler_params=_cparams(2),
        name="nsa_sample_cmp",
    )(page_table, slopes, c2s, q, kv_win_new, cache_win, *([ab_pool] * pages))


def _nsa_sample_sel_kernel(pt_ref, slope_ref, exp_ref, q_ref, kvn_ref, sel_ref, xcmp_ref, xwin_ref, gate_ref, *rest,
                           pages, past):
    del pt_ref
    page_refs = rest[:pages]
    o_ref, m_scr, l_scr, acc_scr, new_scr = rest[pages:]
    j = pl.program_id(1)
    ts = q_ref.shape[0]
    n_groups = kvn_ref.shape[1] // (2 * HEAD_DIM)
    gw = n_groups * HEAD_DIM
    n_rows = slope_ref.shape[0]
    page_rows = page_refs[0].shape[-1]
    tk = pages * page_rows
    slope = slope_ref[:, 0:1]
    qbd = _sample_query(q_ref, ts, n_groups)
    t_row = lax.broadcasted_iota(jnp.int32, (n_rows, 1), 0) % ts

    @pl.when(j == 0)
    def _():
        m_scr[...] = jnp.full(m_scr.shape, NEG, F32)
        l_scr[...] = jnp.zeros(l_scr.shape, F32)
        acc_scr[...] = jnp.zeros(acc_scr.shape, F32)

    def update(k, v, ok, dist, transposed):
        qk = jnp.dot(qbd, k, preferred_element_type=F32) if transposed else _nt_dot(qbd, k)
        s = jnp.where(ok, qk - slope * dist.astype(F32), NEG)
        m_old = m_scr[:, 0:1]
        m_new = jnp.maximum(m_old, jnp.max(s, axis=-1, keepdims=True))
        alpha = jnp.exp(m_old - m_new)
        p = jnp.where(ok, jnp.exp(s - m_new), 0.0)
        pv = _nt_dot(p.astype(BF16), v) if transposed else jnp.dot(p.astype(BF16), v, preferred_element_type=F32)
        l_scr[...] = jnp.broadcast_to(alpha * l_scr[:, 0:1] + jnp.sum(p, axis=-1, keepdims=True), l_scr.shape)
        acc_scr[...] = alpha * acc_scr[...] + pv
        m_scr[...] = jnp.broadcast_to(m_new, m_scr.shape)

    sel = sel_ref[j]
    k = jnp.concatenate([ref[0].reshape(gw, page_rows) for ref in page_refs], axis=1).astype(BF16)
    v = jnp.concatenate([ref[1].reshape(gw, page_rows) for ref in page_refs], axis=1).astype(BF16)
    picked = _nt_dot(sel, exp_ref[0:tk, :]) > 0.5
    kpos = j * tk + lax.broadcasted_iota(jnp.int32, (1, tk), 1)
    update(k, v, picked, (past + t_row) - kpos, True)

    @pl.when(j == pl.num_programs(1) - 1)
    def _():
        pad = new_scr.shape[0]
        new_scr[...] = jnp.concatenate([kvn_ref[...], jnp.zeros((pad - ts, 2 * gw), F32)], axis=0).astype(BF16)
        i_new = lax.broadcasted_iota(jnp.int32, (1, pad), 1)
        ok = (_nt_dot(sel, exp_ref[tk:tk + pad, :]) > 0.5) & (i_new <= t_row) & (i_new < ts)
        update(new_scr[:, 0:gw], new_scr[:, gw:2 * gw], ok, t_row - i_new, False)
        o = acc_scr[...] / jnp.maximum(l_scr[:, 0:1], 1e-30)
        n_rep = n_rows // (n_groups * ts)
        for r, x_sel in enumerate(_sample_rows_out(o, ts, n_groups)):
            g_cmp, g_sel, g_win = (1.0 / (1.0 + jnp.exp(-gate_ref[:, (i * n_rep + r) * gw:(i * n_rep + r + 1) * gw]))
                                   for i in range(3))
            rows = slice(r * ts, (r + 1) * ts)
            o_ref[:, r * gw:(r + 1) * gw] = g_cmp * xcmp_ref[rows, :] + g_sel * x_sel + g_win * xwin_ref[rows, :]


def _nsa_sample_sel(q, kv_sel_new, sel, x_cmp, x_win, gates, pool, page_table, n_groups, pages):
    b, ts, qw = q.shape
    n_rep = qw // HEAD_DIM // n_groups
    gw = n_groups * HEAD_DIM
    n_pages = page_table.shape[1]
    page_rows = pool.shape[-1]
    past = n_pages * page_rows
    n_steps = n_pages // pages
    tk = pages * page_rows
    n_rows = n_rep * n_groups * ts
    assert sel.shape[1] == n_steps and tk % SEL_BLOCK == 0 and tk // SEL_BLOCK < LANES
    slopes = _sample_slopes(n_groups, n_rep, ts)
    blk_of = jnp.concatenate([jnp.arange(tk) // SEL_BLOCK, jnp.full((LANES,), tk // SEL_BLOCK)])
    expand = (blk_of[:, None] == jnp.arange(LANES)[None, :]).astype(BF16)
    per_b = lambda rows, w: pl.BlockSpec((None, rows, w), lambda i, j, pt: (i, 0, 0))
    page_spec = lambda k: pl.BlockSpec((None,) + pool.shape[1:], lambda i, j, pt: (pt[i, j * pages + k], 0, 0, 0, 0))
    const = lambda shape: pl.BlockSpec(shape, lambda i, j, pt: (0,) * len(shape), pipeline_mode=pl.Buffered(1))
    return pl.pallas_call(
        functools.partial(_nsa_sample_sel_kernel, pages=pages, past=past),
        grid_spec=pltpu.PrefetchScalarGridSpec(
            num_scalar_prefetch=1,
            grid=(b, n_steps),
            in_specs=[const(slopes.shape), const(expand.shape), per_b(ts, qw), per_b(ts, 2 * gw),
                      pl.BlockSpec((None, n_steps, n_rows, LANES), lambda i, j, pt: (i, 0, 0, 0)),
                      per_b(n_rep * ts, gw), per_b(n_rep * ts, gw), per_b(ts, 3 * qw)]
                     + [page_spec(k) for k in range(pages)],
            out_specs=per_b(ts, qw),
            scratch_shapes=[pltpu.VMEM((n_rows, LANES), F32), pltpu.VMEM((n_rows, LANES), F32),
                            pltpu.VMEM((n_rows, gw), F32), pltpu.VMEM((LANES, 2 * gw), BF16)]),
        out_shape=jax.ShapeDtypeStruct((b, ts, qw), F32),
        compiler_params=_cparams(2),
        name="nsa_sample_sel",
    )(page_table, slopes, expand, q, kv_sel_new, sel, x_cmp, x_win, gates, *([pool] * pages))


def _gate_columns(w_gate, n_groups):
    d = w_gate.shape[0]
    per_pair = w_gate.reshape(d, n_groups // 2, 2 * 4 * 3)
    return jnp.pad(per_pair, ((0, 0), (0, 0), (0, LANES - 2 * 4 * 3))).reshape(d, -1)


def kernel(x_prompt, x_sample, cache_dil_kv, state_conv, state_rnn, cache_win_kv, cache_cmp_kv, cache_sel_kv, page_table, norm_mix, norm_ffn, norm_out, w_in_ab, w_out_ab, conv_w, conv_b, gate_a_w, gate_a_b, gate_x_w, gate_x_b, lru_lambda, w_in_c, w_out_c, w_cmp, pe_cmp, ffn_w1, ffn_w3, ffn_w2):
    bp, t, d = x_prompt.shape
    bs, ts, _ = x_sample.shape
    depth = norm_mix.shape[0]
    bf = lambda z: z.astype(BF16)
    tm_p, tm_s = 512, bs * ts
    yp = x_prompt.reshape(bp * t, d)
    ys = x_sample.reshape(bs * ts, d)
    outs = {k: [] for k in ("dil_p", "dil_s", "conv_p", "conv_s", "rnn_p", "rnn_s",
                            "win_p", "win_s", "cmp_p", "cmp_s", "sel_p", "sel_s")}
    for layer in range(depth):
        li = layer // 2
        last = layer == depth - 1
        ffn = (norm_ffn[layer], bf(ffn_w1[layer]), bf(ffn_w3[layer]), bf(ffn_w2[layer]), norm_out if last else None)
        if layer % 2 == 0:
            rw = conv_w.shape[2]
            aw = (w_in_ab.shape[2] - 2 * rw) // 3
            n_heads = aw // HEAD_DIM
            w_in = bf(w_in_ab[li])
            splits = (aw, 2 * aw, rw, rw)
            wa, wx = bf(_block_diag(gate_a_w[li])), bf(_block_diag(gate_x_w[li]))
            lru = (conv_w[li], conv_b[li], wa, gate_a_b[li], wx, gate_x_b[li], lru_lambda[li])
            w_outs = bf(w_out_ab[li])
            keep = min(DIL_MAX, t)
            as_cache = lambda z, groups, rows: jnp.transpose(z.reshape(bp, 2, groups, HEAD_DIM, rows), (0, 4, 1, 2, 3))
            q, kv, xr, gate, kv_t = _norm_proj(yp, norm_mix[layer], w_in, splits, tm_p,
                                               rows_minor=[(w_in[:, aw:3 * aw].T, keep)], seq=t)
            o_att = _dil_attn(q.reshape(bp, t, aw), kv.reshape(bp, t, 2 * aw), n_heads)
            o_rnn, conv_new, h_new = _rglru(xr.reshape(bp, t, rw), gate.reshape(bp, t, rw),
                                            jnp.zeros((bp, CONV_W - 1, rw), F32), jnp.zeros((bp, rw), F32),
                                            *lru, bp, 256)
            yp = _out_ffn(yp, [o_att.reshape(bp * t, aw), o_rnn.reshape(bp * t, rw)], w_outs, *ffn, tm_p)
            outs["dil_p"].append(as_cache(kv_t, n_heads, keep))
            outs["conv_p"].append(conv_new)
            outs["rnn_p"].append(h_new)
            q, kv, xr, gate = _norm_proj(ys, norm_mix[layer], w_in, splits, tm_s)
            cache = cache_dil_kv[li]
            o_att, ctx_new = _dil_sample(q.reshape(bs, ts, aw), kv.reshape(bs, ts, 2 * aw),
                                         jnp.transpose(cache, (0, 2, 3, 4, 1)), n_heads)
            ctx_new = jnp.transpose(ctx_new.reshape(bs, 2, n_heads, HEAD_DIM, cache.shape[1]), (0, 4, 1, 2, 3))
            o_rnn, conv_new, h_new = _rglru(xr.reshape(bs, ts, rw), gate.reshape(bs, ts, rw),
                                            state_conv[li], state_rnn[li], *lru, 8, ts)
            ys = _out_ffn(ys, [o_att.reshape(bs * ts, aw), o_rnn.reshape(bs * ts, rw)], w_outs, *ffn, tm_s)
            outs["dil_s"].append(ctx_new)
            outs["conv_s"].append(conv_new)
            outs["rnn_s"].append(h_new)
        else:
            n_groups = cache_win_kv.shape[4]
            kvw = 2 * n_groups * HEAD_DIM
            qw = w_out_c.shape[1]
            rep = qw // HEAD_DIM // n_groups
            w_in = bf(jnp.concatenate([w_in_c[li][:, :qw + 3 * kvw],
                                       _gate_columns(w_in_c[li][:, qw + 3 * kvw:], n_groups)], axis=1))
            gw = n_groups // 2 * LANES
            splits = (qw, kvw, kvw, kvw, gw)
            wbd, pe = _compress_weights(w_cmp[li], pe_cmp[li], n_groups)
            w_outs = bf(w_out_c[li])
            kv5 = lambda z, n: z.reshape(n, -1, 2, n_groups, HEAD_DIM)
            keep = min(WIN, t)
            as_cache = lambda z, rows: jnp.transpose(z.reshape(bp, 2, n_groups, HEAD_DIM, rows), (0, 4, 1, 2, 3))
            kv_t = lambda i: w_in[:, qw + i * kvw:qw + (i + 1) * kvw].T
            q, kv_cmp, kv_sel, kv_win, gates, cmp_t, sel_t, win_t = _norm_proj(
                yp, norm_mix[layer], w_in, splits, tm_p, rows_minor=[(kv_t(0), t), (kv_t(1), t), (kv_t(2), keep)], seq=t)
            o = _nsa_prompt(q.reshape(bp, t, qw), kv_cmp.reshape(bp, t, kvw), kv_sel.reshape(bp, t, kvw),
                            kv_win.reshape(bp, t, kvw), gates.reshape(bp, t, gw), wbd, pe, n_groups)
            yp = _out_ffn(yp, [o.reshape(bp * t, qw)], w_outs, *ffn, tm_p)
            outs["win_p"].append(as_cache(win_t, keep))
            outs["cmp_p"].append(as_cache(cmp_t, t))
            outs["sel_p"].append(as_cache(sel_t, t))
            wq = w_in_c[li][:, :qw].reshape(d, n_groups, rep, HEAD_DIM).transpose(0, 2, 1, 3).reshape(d, qw)
            wg = w_in_c[li][:, qw + 3 * kvw:].reshape(d, n_groups, rep, 3).transpose(0, 3, 2, 1)
            wg = jnp.broadcast_to(wg[..., None], wg.shape + (HEAD_DIM,)).reshape(d, 3 * qw)
            w_in_s = bf(jnp.concatenate([wq, w_in_c[li][:, qw:qw + 3 * kvw], wg], axis=1))
            w_out_s = bf(w_out_c[li].reshape(n_groups, rep, HEAD_DIM, d).transpose(1, 0, 2, 3).reshape(qw, d))
            q, kv_cmp, kv_sel, kv_win, gates = _norm_proj(ys, norm_mix[layer], w_in_s, (qw, kvw, kvw, kvw, 3 * qw), tm_s)
            n_phys, page_rows = cache_cmp_kv.shape[1], cache_cmp_kv.shape[2]
            rows_minor = lambda pool: jnp.transpose(pool, (0, 2, 3, 4, 1))
            ab_pool = _compress_pool(rows_minor(cache_cmp_kv[li]), wbd, pe, 32)
            ab_pool = ab_pool.reshape(n_phys, page_rows // CMP_STRIDE, 2 * kvw)
            cwin = cache_win_kv[li]
            x_cmp, x_win, sel, win_new = _nsa_sample_cmp(q.reshape(bs, ts, qw), kv_win.reshape(bs, ts, kvw),
                                                         cwin.reshape(bs, cwin.shape[1], kvw), ab_pool, page_table,
                                                         n_groups, 2, 32)
            o = _nsa_sample_sel(q.reshape(bs, ts, qw), kv_sel.reshape(bs, ts, kvw), sel, x_cmp, x_win,
                                gates.reshape(bs, ts, 3 * qw), rows_minor(cache_sel_kv[li]), page_table, n_groups, 32)
            ys = _out_ffn(ys, [o.reshape(bs * ts, qw)], w_out_s, *ffn, tm_s)
            outs["win_s"].append(win_new.reshape(cwin.shape))
            outs["cmp_s"].append(kv5(kv_cmp, bs))
            outs["sel_s"].append(kv5(kv_sel, bs))
    st = lambda k: jnp.stack(outs[k])
    return (yp.reshape(bp, t, d), ys.reshape(bs, ts, d), st("dil_p"), st("dil_s"), st("conv_p"), st("conv_s"),
            st("rnn_p"), st("rnn_s"), st("win_p"), st("win_s"), st("cmp_p"), st("cmp_s"), st("sel_p"), st("sel_s"))
```
